```python
import jax, jax.numpy as jnp
from jax import lax
import numpy as np

D_MODEL = 2048
BATCH = 1
SEQ = 16384
DEPTH = 2

GRID_W = 64
CTX_LEN = 256
HEAD_DIM = 128
N_ATT_HEADS = 8
N_KV_HEADS = 2
N_RET_HEADS = 8
ATT_W = N_ATT_HEADS * HEAD_DIM
KV_W = N_KV_HEADS * HEAD_DIM
RET_W = N_RET_HEADS * HEAD_DIM
MIX_W = ATT_W + RET_W
IN_COLS = ATT_W + 2 * KV_W + 4 * RET_W
WINDOW = 128
BLOCK = 128
RET_CHUNK = 128
ROPE_THETA = 10000.0
ROPE_PAIRS = HEAD_DIM // 4
D_FF = 256 * ((8 * D_MODEL // 3 + 255) // 256)
N_EXPERTS = 8
TOP_K = 2
D_FF_EXPERT = D_FF
N_DENSE = (DEPTH + 1) // 2
N_MOE = DEPTH // 2
EPS = 1e-6

kernel_name = "hymba_style_swa_retention_moe_dit_trunk"


def rmsnorm(x, g):
    x32 = x.astype(jnp.float32)
    y = x32 * lax.rsqrt(jnp.mean(x32 * x32, axis=-1, keepdims=True) + EPS)
    return (y * g.astype(jnp.float32)).astype(x.dtype)


def modulate(h, shift, scale):
    return h * (1 + scale) + shift


def axial_rope_tables(n_rows):
    row = jnp.repeat(jnp.arange(n_rows), GRID_W)
    col = jnp.tile(jnp.arange(GRID_W), n_rows)
    inv = ROPE_THETA ** (-jnp.arange(ROPE_PAIRS, dtype=jnp.float32) / ROPE_PAIRS)
    ang = jnp.stack([row, col], axis=-1).astype(jnp.float32)[:, :, None] * inv
    return jnp.cos(ang), jnp.sin(ang)


def apply_axial_rope(x, cos, sin):
    shp = x.shape
    xr = x.astype(jnp.float32).reshape(*shp[:-1], 2, 2, ROPE_PAIRS)
    x1, x2 = xr[..., 0, :], xr[..., 1, :]
    cs, sn = cos[None, :, None], sin[None, :, None]
    out = jnp.stack([x1 * cs - x2 * sn, x2 * cs + x1 * sn], axis=-2)
    return out.reshape(shp).astype(x.dtype)


def split_projection(p):
    cuts = np.cumsum([ATT_W, KV_W, KV_W, RET_W, RET_W, RET_W])
    return jnp.split(p, [int(v) for v in cuts], axis=-1)


def attention_heads(qa, ka, va, qg, kg):
    B, N, _ = qa.shape
    q = rmsnorm(qa.reshape(B, N, N_ATT_HEADS, HEAD_DIM), qg)
    k = rmsnorm(ka.reshape(B, N, N_KV_HEADS, HEAD_DIM), kg)
    v = va.reshape(B, N, N_KV_HEADS, HEAD_DIM)
    return q, k, v


def sink_column(sink, shape_prefix):
    g = N_ATT_HEADS // N_KV_HEADS
    s = sink.astype(jnp.float32).reshape(N_KV_HEADS, g)
    return jnp.broadcast_to(s.reshape((1,) * (len(shape_prefix) - 3) + (N_KV_HEADS, g, 1, 1)), shape_prefix + (1,))


def windowed_attention(q, k, v, k_ctx, v_ctx, sink):
    B, S, H, d = q.shape
    G = H // N_KV_HEADS
    nb = S // BLOCK
    scale = d ** -0.5
    qb = q.reshape(B, nb, BLOCK, N_KV_HEADS, G, d)
    pad = ((0, 0), (BLOCK, BLOCK), (0, 0), (0, 0))
    kp = jnp.pad(k, pad).reshape(B, nb + 2, BLOCK, N_KV_HEADS, d)
    vp = jnp.pad(v, pad).reshape(B, nb + 2, BLOCK, N_KV_HEADS, d)
    kb = jnp.concatenate([kp[:, :-2], kp[:, 1:-1], kp[:, 2:]], axis=2)
    vb = jnp.concatenate([vp[:, :-2], vp[:, 1:-1], vp[:, 2:]], axis=2)
    s_loc = jnp.einsum('bnqhgd,bnjhd->bnhgqj', qb, kb).astype(jnp.float32) * scale
    qi = jnp.arange(nb)[:, None, None] * BLOCK + jnp.arange(BLOCK)[None, :, None]
    kj = jnp.arange(nb)[:, None, None] * BLOCK - BLOCK + jnp.arange(3 * BLOCK)[None, None, :]
    valid = (jnp.abs(qi - kj) <= WINDOW) & (kj >= 0) & (kj < S)
    s_loc = jnp.where(valid[None, :, None, None], s_loc, -jnp.inf)
    s_ctx = jnp.einsum('bnqhgd,bchd->bnhgqc', qb, k_ctx).astype(jnp.float32) * scale
    C = k_ctx.shape[1]
    scores = jnp.concatenate([sink_column(sink, s_ctx.shape[:-1]), s_ctx, s_loc], axis=-1)
    p = jax.nn.softmax(scores, axis=-1).astype(v.dtype)
    o = (jnp.einsum('bnhgqc,bchd->bnqhgd', p[..., 1:1 + C], v_ctx)
         + jnp.einsum('bnhgqj,bnjhd->bnqhgd', p[..., 1 + C:], vb))
    return o.reshape(B, S, H * d)


def context_attention(q, k, v, sink):
    B, C, H, d = q.shape
    G = H // N_KV_HEADS
    qg = q.reshape(B, C, N_KV_HEADS, G, d)
    s = jnp.einsum('bqhgd,bchd->bhgqc', qg, k).astype(jnp.float32) * (d ** -0.5)
    s = jnp.concatenate([sink_column(sink, s.shape[:-1]), s], axis=-1)
    p = jax.nn.softmax(s, axis=-1)[..., 1:].astype(v.dtype)
    return jnp.einsum('bhgqc,bchd->bqhgd', p, v).reshape(B, C, H * d)


def retention_heads(qr, kr, vr):
    B, N, _ = qr.shape
    sh = (B, N, N_RET_HEADS, HEAD_DIM)
    return qr.reshape(sh), kr.reshape(sh), vr.reshape(sh)


def to_bhnd(t):
    return jnp.transpose(t, (0, 2, 1, 3)).astype(jnp.float32)


def retention_context(q, k, v, log_g):
    C = q.shape[2]
    pos = jnp.arange(C, dtype=jnp.float32)
    rel = pos[:, None] - pos[None, :]
    d_f = jnp.where(rel >= 0, jnp.exp(jnp.maximum(rel, 0.0)[None] * log_g[0][:, None, None]), 0.0)
    d_b = jnp.where(rel <= 0, jnp.exp(jnp.maximum(-rel, 0.0)[None] * log_g[1][:, None, None]), 0.0)
    s = jnp.einsum('bhnd,bhmd->bhnm', q, k) * (d_f + d_b)[None]
    return jnp.einsum('bhnm,bhme->bhne', s, v)


def retention_context_states(k, v, log_g):
    C = k.shape[2]
    pos = jnp.arange(C, dtype=jnp.float32)
    w_f = jnp.exp((C - 1 - pos)[None] * log_g[0][:, None])
    w_b = jnp.exp(pos[None] * log_g[1][:, None])
    s_f = jnp.einsum('bhmd,bhme->bhde', k * w_f[None, :, :, None], v)
    s_b = jnp.einsum('bhmd,bhme->bhde', k * w_b[None, :, :, None], v)
    return s_f, s_b


def retention_chunkwise(q, k, v, log_g, s0):
    B, H, L, d = q.shape
    nz = L // RET_CHUNK
    qc = q.reshape(B, H, nz, RET_CHUNK, d)
    kc = k.reshape(B, H, nz, RET_CHUNK, d)
    vc = v.reshape(B, H, nz, RET_CHUNK, d)
    pos = jnp.arange(RET_CHUNK, dtype=jnp.float32)
    lg = log_g[:, None]
    rel = pos[:, None] - pos[None, :]
    decay_in = jnp.where(rel >= 0, jnp.exp(jnp.maximum(rel, 0.0)[None] * lg[:, :, None]), 0.0)
    q_dec = jnp.exp((pos + 1.0)[None] * lg)
    k_dec = jnp.exp((RET_CHUNK - 1.0 - pos)[None] * lg)
    chunk_decay = jnp.exp(RET_CHUNK * log_g)
    s = jnp.einsum('bhzid,bhzjd->bhzij', qc, kc) * decay_in[None, :, None]
    inner = jnp.einsum('bhzij,bhzje->bhzie', s, vc)
    kv = jnp.einsum('bhzjd,bhzje->zbhde', kc * k_dec[None, :, None, :, None], vc)

    def step(state, kv_z):
        return chunk_decay[None, :, None, None] * state + kv_z, state

    _, s_prev = lax.scan(step, s0, kv)
    cross = jnp.einsum('bhzid,zbhde->bhzie', qc * q_dec[None, :, None, :, None], s_prev)
    return (inner + cross).reshape(B, H, L, d)


def retention_output(y, gate, gn_g):
    B, H, N, d = y.shape
    mu = jnp.mean(y, axis=-1, keepdims=True)
    var = jnp.mean(jnp.square(y - mu), axis=-1, keepdims=True)
    yn = (y - mu) * lax.rsqrt(var + EPS) * gn_g.astype(jnp.float32)[None, :, None, :]
    yn = jnp.transpose(yn, (0, 2, 1, 3)).reshape(B, N, H * d).astype(gate.dtype)
    return jax.nn.silu(gate) * yn


def swiglu(h, w1, w3, w2):
    return (jax.nn.silu(h @ w1) * (h @ w3)) @ w2


def moe_swiglu(h, router_w, router_b, w1, w3, w2):
    logits = (h @ router_w + router_b).astype(jnp.float32)
    top_val, top_idx = lax.top_k(logits, TOP_K)
    top_w = jax.nn.softmax(top_val, axis=-1)
    gates = jnp.sum(jax.nn.one_hot(top_idx, N_EXPERTS, dtype=jnp.float32) * top_w[..., None], axis=-2)
    y = jnp.zeros_like(h)
    for e in range(N_EXPERTS):
        y = y + gates[..., e:e + 1].astype(h.dtype) * swiglu(h, w1[e], w3[e], w2[e])
    return y


def channel_mixer(i, h, ffn_w1, ffn_w3, ffn_w2, router_w, router_b, moe_w1, moe_w3, moe_w2):
    j = i // 2
    if i % 2 == 0:
        return swiglu(h, ffn_w1[j], ffn_w3[j], ffn_w2[j])
    return moe_swiglu(h, router_w[j], router_b[j], moe_w1[j], moe_w3[j], moe_w2[j])


def setup_inputs(seed: int = 0) -> dict:
    key = jax.random.key(seed)
    ks = jax.random.split(key, 24)
    f32 = jnp.float32
    nrm = lambda k, shape, s: jax.random.normal(k, shape, f32) * s
    a = 5.0 + jnp.arange(N_RET_HEADS, dtype=f32)
    decay_logit = jnp.log(2.0 ** a - 1.0)
    return {
        "x": nrm(ks[0], (BATCH, SEQ, D_MODEL), 1.0),
        "c": nrm(ks[1], (BATCH, D_MODEL), 1.0),
        "ctx": nrm(ks[2], (BATCH, CTX_LEN, D_MODEL), 1.0),
        "c_ctx": nrm(ks[3], (D_MODEL,), 1.0),
        "ada_w": nrm(ks[4], (DEPTH, D_MODEL, 6 * D_MODEL), 0.5 * D_MODEL ** -0.5),
        "ada_b": nrm(ks[5], (DEPTH, 6 * D_MODEL), 0.01),
        "norm1_g": 1.0 + nrm(ks[6], (DEPTH, D_MODEL), 0.02),
        "norm2_g": 1.0 + nrm(ks[7], (DEPTH, D_MODEL), 0.02),
        "w_in": nrm(ks[8], (DEPTH, D_MODEL, IN_COLS), D_MODEL ** -0.5),
        "q_norm_g": 1.0 + nrm(ks[9], (DEPTH, HEAD_DIM), 0.02),
        "k_norm_g": 1.0 + nrm(ks[10], (DEPTH, HEAD_DIM), 0.02),
        "attn_sink": nrm(ks[11], (DEPTH, N_ATT_HEADS), 0.5),
        "ret_decay": decay_logit[None, None, :] + nrm(ks[12], (DEPTH, 2, N_RET_HEADS), 0.1),
        "ret_gn_g": 1.0 + nrm(ks[13], (DEPTH, N_RET_HEADS, HEAD_DIM), 0.02),
        "w_out": nrm(ks[14], (DEPTH, MIX_W, D_MODEL), MIX_W ** -0.5),
        "ffn_w1": nrm(ks[15], (N_DENSE, D_MODEL, D_FF), D_MODEL ** -0.5),
        "ffn_w3": nrm(ks[16], (N_DENSE, D_MODEL, D_FF), D_MODEL ** -0.5),
        "ffn_w2": nrm(ks[17], (N_DENSE, D_FF, D_MODEL), D_FF ** -0.5),
        "router_w": nrm(ks[18], (N_MOE, D_MODEL, N_EXPERTS), D_MODEL ** -0.5),
        "router_b": nrm(ks[19], (N_MOE, N_EXPERTS), 0.01),
        "moe_w1": nrm(ks[20], (N_MOE, N_EXPERTS, D_MODEL, D_FF_EXPERT), D_MODEL ** -0.5),
        "moe_w3": nrm(ks[21], (N_MOE, N_EXPERTS, D_MODEL, D_FF_EXPERT), D_MODEL ** -0.5),
        "moe_w2": nrm(ks[22], (N_MOE, N_EXPERTS, D_FF_EXPERT, D_MODEL), D_FF_EXPERT ** -0.5),
    }


def reference(x, c, ctx, c_ctx, ada_w, ada_b, norm1_g, norm2_g, w_in, q_norm_g, k_norm_g,
              attn_sink, ret_decay, ret_gn_g, w_out, ffn_w1, ffn_w3, ffn_w2,
              router_w, router_b, moe_w1, moe_w3, moe_w2):
    B, S, _ = x.shape
    n_rows = S // GRID_W
    cos, sin = axial_rope_tables(n_rows)
    k_scale = HEAD_DIM ** -0.5
    x_lat, x_ctx = x, ctx
    for i in range(DEPTH):
        last = i == DEPTH - 1
        mod_l = jnp.split((jax.nn.silu(c) @ ada_w[i] + ada_b[i])[:, None, :], 6, axis=-1)
        mod_c = jnp.split(jax.nn.silu(c_ctx) @ ada_w[i] + ada_b[i], 6, axis=-1)
        log_g = jax.nn.log_sigmoid(ret_decay[i].astype(jnp.float32))

        h_l = modulate(rmsnorm(x_lat, norm1_g[i]), mod_l[0], mod_l[1])
        h_c = modulate(rmsnorm(x_ctx, norm1_g[i]), mod_c[0], mod_c[1])
        qa_l, ka_l, va_l, qr_l, kr_l, vr_l, gr_l = split_projection(h_l @ w_in[i])
        qa_c, ka_c, va_c, qr_c, kr_c, vr_c, gr_c = split_projection(h_c @ w_in[i])

        q_l, k_l, v_l = attention_heads(qa_l, ka_l, va_l, q_norm_g[i], k_norm_g[i])
        q_l = apply_axial_rope(q_l, cos, sin)
        k_l = apply_axial_rope(k_l, cos, sin)
        q_c, k_c, v_c = attention_heads(qa_c, ka_c, va_c, q_norm_g[i], k_norm_g[i])
        att_l = windowed_attention(q_l, k_l, v_l, k_c, v_c, attn_sink[i])

        rq_l, rk_l, rv_l = retention_heads(qr_l, kr_l, vr_l)
        rq_l = to_bhnd(apply_axial_rope(rq_l, cos, sin))
        rk_l = to_bhnd(apply_axial_rope(rk_l, cos, sin)) * k_scale
        rv_l = to_bhnd(rv_l)
        rq_c, rk_c, rv_c = retention_heads(qr_c, kr_c, vr_c)
        rq_c, rk_c, rv_c = to_bhnd(rq_c), to_bhnd(rk_c) * k_scale, to_bhnd(rv_c)
        s_f, s_b = retention_context_states(rk_c, rv_c, log_g)
        y_f = retention_chunkwise(rq_l, rk_l, rv_l, log_g[0], s_f)
        y_b = retention_chunkwise(rq_l[:, :, ::-1], rk_l[:, :, ::-1], rv_l[:, :, ::-1], log_g[1], s_b)[:, :, ::-1]
        ret_l = retention_output(y_f + y_b, gr_l, ret_gn_g[i])

        x_lat = x_lat + mod_l[2] * (jnp.concatenate([att_l, ret_l], axis=-1) @ w_out[i])

        h2_l = modulate(rmsnorm(x_lat, norm2_g[i]), mod_l[3], mod_l[4])
        x_lat = x_lat + mod_l[5] * channel_mixer(i, h2_l, ffn_w1, ffn_w3, ffn_w2,
                                                 router_w, router_b, moe_w1, moe_w3, moe_w2)

        if not last:
            att_c = context_attention(q_c, k_c, v_c, attn_sink[i])
            ret_c = retention_output(retention_context(rq_c, rk_c, rv_c, log_g), gr_c, ret_gn_g[i])
            x_ctx = x_ctx + mod_c[2] * (jnp.concatenate([att_c, ret_c], axis=-1) @ w_out[i])
            h2_c = modulate(rmsnorm(x_ctx, norm2_g[i]), mod_c[3], mod_c[4])
            x_ctx = x_ctx + mod_c[5] * channel_mixer(i, h2_c, ffn_w1, ffn_w3, ffn_w2,
                                                     router_w, router_b, moe_w1, moe_w3, moe_w2)
    return x_lat
```

```python
import functools

import jax
import jax.numpy as jnp
from jax import lax
from jax.experimental import pallas as pl
from jax.experimental.pallas import tpu as pltpu

F32 = jnp.float32
BF16 = jnp.bfloat16

HEAD_DIM = 128
N_ATT_HEADS = 8
N_KV_HEADS = 2
N_RET_HEADS = 8
GQA_GROUP = N_ATT_HEADS // N_KV_HEADS
ATT_W = N_ATT_HEADS * HEAD_DIM
KV_W = N_KV_HEADS * HEAD_DIM
RET_W = N_RET_HEADS * HEAD_DIM
IN_COLS = ATT_W + 2 * KV_W + 4 * RET_W
WINDOW = 128
RET_CHUNK = 128
GRID_W = 64
ROPE_THETA = 10000.0
ROPE_PAIRS = HEAD_DIM // 4
N_EXPERTS = 8
EPS = 1e-6
ATT_SCALE = HEAD_DIM ** -0.5

COL_Q = 0
COL_K = ATT_W // HEAD_DIM
COL_V = COL_K + N_KV_HEADS
COL_RQ = COL_V + N_KV_HEADS
COL_RK = COL_RQ + N_RET_HEADS
COL_RV = COL_RK + N_RET_HEADS
COL_G = COL_RV + N_RET_HEADS

VMEM_LIMIT_BYTES = 56 * 1024 * 1024
LANES = 128

NEG_INF = float("-inf")


def _cparams(*sem):
    return pltpu.CompilerParams(dimension_semantics=sem, vmem_limit_bytes=VMEM_LIMIT_BYTES)


def _silu(x):
    return x * (1.0 / (1.0 + jnp.exp(-x)))


def _dot(a, b):
    return jnp.dot(a, b, preferred_element_type=F32)


def _dot_nt(a, b):
    return lax.dot_general(a, b, (((1,), (1,)), ((), ())), preferred_element_type=F32)


def _dot_tn(a, b):
    return lax.dot_general(a, b, (((0,), (0,)), ((), ())), preferred_element_type=F32)


def _norm_modulate(x, g, shift, scale):
    ms = jnp.mean(x * x, axis=-1, keepdims=True)
    return (x * lax.rsqrt(ms + EPS) * g) * (1.0 + scale) + shift


def _ada_kernel(r_ref, w_ref, b_ref, o_ref):
    s = _silu(r_ref[...])
    o_ref[...] = jnp.dot(s, w_ref[...], preferred_element_type=F32,
                         precision=lax.Precision.HIGHEST) + b_ref[...]


def _ada_mods(rows, ada_w, ada_b):
    depth, d, n = ada_w.shape
    tn = d // 2
    return pl.pallas_call(
        _ada_kernel,
        grid=(depth, n // tn),
        in_specs=[
            pl.BlockSpec((8, d), lambda l, j: (0, 0)),
            pl.BlockSpec((None, d, tn), lambda l, j: (l, 0, j)),
            pl.BlockSpec((None, 1, tn), lambda l, j: (l, 0, j)),
        ],
        out_specs=pl.BlockSpec((None, 8, tn), lambda l, j: (l, 0, j)),
        out_shape=jax.ShapeDtypeStruct((depth, 8, n), F32),
        compiler_params=_cparams("arbitrary", "arbitrary"),
        name="ada_mods",
    )(rows, ada_w, ada_b.reshape(depth, 1, n))


def _rope(y, cos, sa, sb):
    return y * cos + pltpu.roll(y, 96, 1) * sa + pltpu.roll(y, 32, 1) * sb


def _head_norm(p, g):
    ms = jnp.mean(p * p, axis=-1, keepdims=True)
    return p * lax.rsqrt(ms + EPS) * g


def _inproj_kernel(x_ref, sh_ref, sc_ref, g1_ref, w_ref, cos_ref, sa_ref, sb_ref,
                   qg_ref, kg_ref, o_ref, h_scr, *, row, tn):
    j = pl.program_id(1)
    nchunk = tn // HEAD_DIM

    @pl.when(j == 0)
    def _():
        h = _norm_modulate(x_ref[...], g1_ref[...], sh_ref[row:row + 1, :], sc_ref[row:row + 1, :])
        h_scr[...] = h.astype(BF16)

    p = _dot(h_scr[...], w_ref[...])

    def chunk(c):
        return p[:, c * HEAD_DIM:(c + 1) * HEAD_DIM]

    def put(c, y):
        o_ref[:, c * HEAD_DIM:(c + 1) * HEAD_DIM] = y.astype(o_ref.dtype)

    def rope(y):
        return _rope(y, cos_ref[...], sa_ref[...], sb_ref[...])

    jq = ATT_W // tn
    jr0 = (ATT_W + 2 * KV_W) // tn
    jr1 = jr0 + RET_W // tn
    jr2 = jr1 + RET_W // tn

    @pl.when(j < jq)
    def _():
        for c in range(nchunk):
            put(c, rope(_head_norm(chunk(c), qg_ref[...]) * ATT_SCALE))

    @pl.when(j == jq)
    def _():
        for c in range(nchunk):
            if c < N_KV_HEADS:
                put(c, rope(_head_norm(chunk(c), kg_ref[...])))
            else:
                put(c, chunk(c))

    @pl.when((j >= jr0) & (j < jr1))
    def _():
        for c in range(nchunk):
            put(c, rope(chunk(c)))

    @pl.when((j >= jr1) & (j < jr2))
    def _():
        for c in range(nchunk):
            put(c, rope(chunk(c)) * ATT_SCALE)

    @pl.when(j >= jr2)
    def _():
        for c in range(nchunk):
            put(c, chunk(c))


def _in_projection(x, mods, layer, row, g1, w, cos, sa, sb, qg, kg, tm):
    m, d = x.shape
    n = w.shape[1]
    tn = 2 * KV_W
    assert n == IN_COLS and ATT_W % tn == 0 and RET_W % tn == 0 and m % tm == 0
    return pl.pallas_call(
        functools.partial(_inproj_kernel, row=row, tn=tn),
        grid=(m // tm, n // tn),
        in_specs=[
            pl.BlockSpec((tm, d), lambda i, j: (i, 0)),
            pl.BlockSpec((None, 8, d), lambda i, j: (layer, 0, 0)),
            pl.BlockSpec((None, 8, d), lambda i, j: (layer, 0, 1)),
            pl.BlockSpec((1, d), lambda i, j: (0, 0)),
            pl.BlockSpec((d, tn), lambda i, j: (0, j)),
            pl.BlockSpec((tm, HEAD_DIM), lambda i, j: (i, 0)),
            pl.BlockSpec((tm, HEAD_DIM), lambda i, j: (i, 0)),
            pl.BlockSpec((tm, HEAD_DIM), lambda i, j: (i, 0)),
            pl.BlockSpec((1, HEAD_DIM), lambda i, j: (0, 0)),
            pl.BlockSpec((1, HEAD_DIM), lambda i, j: (0, 0)),
        ],
        out_specs=pl.BlockSpec((tm, tn), lambda i, j: (i, j)),
        out_shape=jax.ShapeDtypeStruct((m, n), BF16),
        scratch_shapes=[pltpu.VMEM((tm, d), BF16)],
        compiler_params=_cparams("arbitrary", "arbitrary"),
        name="in_projection",
    )(x, mods, mods, g1, w, cos, sa, sb, qg, kg)


def _attn_kernel(sink_ref, q_ref, km_ref, vm_ref, kp_ref, vp_ref, kn_ref, vn_ref,
                 kc_ref, vc_ref, bias_ref, o_ref, kbuf, vbuf, *, tq):
    i = pl.program_id(0)
    last = pl.num_programs(0) - 1
    nsub = tq // WINDOW
    kbuf[0:WINDOW] = kp_ref[...]
    kbuf[WINDOW:WINDOW + tq] = km_ref[...]
    kbuf[WINDOW + tq:] = kn_ref[...]
    vbuf[0:WINDOW] = vp_ref[...]
    vbuf[WINDOW:WINDOW + tq] = vm_ref[...]
    vbuf[WINDOW + tq:] = vn_ref[...]
    col = lax.broadcasted_iota(jnp.int32, (WINDOW, 3 * WINDOW), 1)
    for s in range(nsub):
        bias = bias_ref[...]
        if s == 0:
            bias = bias + jnp.where(col < WINDOW, jnp.where(i == 0, NEG_INF, 0.0), 0.0)
        if s == nsub - 1:
            bias = bias + jnp.where(col >= 2 * WINDOW, jnp.where(i == last, NEG_INF, 0.0), 0.0)
        bias4 = jnp.concatenate([bias] * GQA_GROUP, axis=0)
        for h in range(N_KV_HEADS):
            hs = slice(h * HEAD_DIM, (h + 1) * HEAD_DIM)
            q = jnp.concatenate(
                [q_ref[s * WINDOW:(s + 1) * WINDOW,
                       (h * GQA_GROUP + g) * HEAD_DIM:(h * GQA_GROUP + g + 1) * HEAD_DIM]
                 for g in range(GQA_GROUP)], axis=0)
            kl = kbuf[s * WINDOW:(s + 3) * WINDOW, hs]
            vl = vbuf[s * WINDOW:(s + 3) * WINDOW, hs]
            s_loc = _dot_nt(q, kl) + bias4
            s_ctx = _dot_nt(q, kc_ref[:, hs])
            sink = jnp.concatenate(
                [jnp.full((WINDOW, 1), sink_ref[h * GQA_GROUP + g], F32) for g in range(GQA_GROUP)],
                axis=0)
            m = jnp.maximum(jnp.maximum(jnp.max(s_loc, axis=-1, keepdims=True),
                                        jnp.max(s_ctx, axis=-1, keepdims=True)), sink)
            p_loc = jnp.exp(s_loc - m)
            p_ctx = jnp.exp(s_ctx - m)
            den = (jnp.sum(p_loc, axis=-1, keepdims=True) + jnp.sum(p_ctx, axis=-1, keepdims=True)
                   + jnp.exp(sink - m))
            o = (_dot(p_loc.astype(BF16), vl) + _dot(p_ctx.astype(BF16), vc_ref[:, hs])) / den
            for g in range(GQA_GROUP):
                o_ref[s * WINDOW:(s + 1) * WINDOW,
                      (h * GQA_GROUP + g) * HEAD_DIM:(h * GQA_GROUP + g + 1) * HEAD_DIM] = (
                    o[g * WINDOW:(g + 1) * WINDOW].astype(o_ref.dtype))


def _band_bias():
    r = jnp.arange(WINDOW)[:, None]
    c = jnp.arange(3 * WINDOW)[None, :]
    ok = (c - r >= 0) & (c - r <= 2 * WINDOW)
    return jnp.where(ok, 0.0, NEG_INF).astype(F32)


def _window_attention(p_lat, p_ctx, sink, tq):
    s = p_lat.shape[0]
    c = p_ctx.shape[0]
    r = tq // WINDOW
    nb = s // WINDOW
    kvb = KV_W // HEAD_DIM
    ck, cv = COL_K // kvb, COL_V // kvb
    return pl.pallas_call(
        functools.partial(_attn_kernel, tq=tq),
        grid=(s // tq,),
        in_specs=[
            pl.BlockSpec(memory_space=pltpu.SMEM),
            pl.BlockSpec((tq, ATT_W), lambda i: (i, 0)),
            pl.BlockSpec((tq, KV_W), lambda i: (i, ck)),
            pl.BlockSpec((tq, KV_W), lambda i: (i, cv)),
            pl.BlockSpec((WINDOW, KV_W), lambda i: (jnp.maximum(i * r - 1, 0), ck)),
            pl.BlockSpec((WINDOW, KV_W), lambda i: (jnp.maximum(i * r - 1, 0), cv)),
            pl.BlockSpec((WINDOW, KV_W), lambda i: (jnp.minimum((i + 1) * r, nb - 1), ck)),
            pl.BlockSpec((WINDOW, KV_W), lambda i: (jnp.minimum((i + 1) * r, nb - 1), cv)),
            pl.BlockSpec((c, KV_W), lambda i: (0, ck)),
            pl.BlockSpec((c, KV_W), lambda i: (0, cv)),
            pl.BlockSpec((WINDOW, 3 * WINDOW), lambda i: (0, 0)),
        ],
        out_specs=pl.BlockSpec((tq, ATT_W), lambda i: (i, 0)),
        out_shape=jax.ShapeDtypeStruct((s, ATT_W), BF16),
        scratch_shapes=[pltpu.VMEM((tq + 2 * WINDOW, KV_W), BF16),
                        pltpu.VMEM((tq + 2 * WINDOW, KV_W), BF16)],
        compiler_params=_cparams("arbitrary"),
        name="window_attention",
    )(sink, p_lat, p_lat, p_lat, p_lat, p_lat, p_lat, p_lat, p_ctx, p_ctx, _band_bias())


def _ctx_mixer_kernel(sink_ref, lg_ref, q_ref, k_ref, v_ref, rq_ref, rk_ref, rv_ref, g_ref, gn_ref,
                      att_ref, ret_ref, sf_ref, sb_ref):
    h = pl.program_id(0)
    c = q_ref.shape[0]
    lg_f = lg_ref[0, h]
    lg_b = lg_ref[1, h]
    s = _dot_nt(q_ref[...], k_ref[...])
    sink = jnp.full((c, 1), sink_ref[h], F32)
    m = jnp.maximum(jnp.max(s, axis=-1, keepdims=True), sink)
    p = jnp.exp(s - m)
    den = jnp.sum(p, axis=-1, keepdims=True) + jnp.exp(sink - m)
    att_ref[...] = (_dot(p.astype(BF16), v_ref[...]) / den).astype(att_ref.dtype)
    n_i = lax.broadcasted_iota(jnp.int32, (c, c), 0)
    m_i = lax.broadcasted_iota(jnp.int32, (c, c), 1)
    rel = (n_i - m_i).astype(F32)
    dec = (jnp.where(rel >= 0, jnp.exp(jnp.maximum(rel, 0.0) * lg_f), 0.0)
           + jnp.where(rel <= 0, jnp.exp(jnp.maximum(-rel, 0.0) * lg_b), 0.0))
    sc = _dot_nt(rq_ref[...], rk_ref[...]) * dec
    y = _dot(sc.astype(BF16), rv_ref[...])
    mu = jnp.mean(y, axis=-1, keepdims=True)
    var = jnp.mean(jnp.square(y - mu), axis=-1, keepdims=True)
    yn = (y - mu) * lax.rsqrt(var + EPS) * gn_ref[...]
    ret_ref[...] = (_silu(g_ref[...].astype(F32)) * yn).astype(ret_ref.dtype)
    pos = lax.broadcasted_iota(jnp.int32, (c, HEAD_DIM), 0).astype(F32)
    kf = rk_ref[...].astype(F32)
    k_f = (kf * jnp.exp((c - 1.0 - pos) * lg_f)).astype(BF16)
    k_b = (kf * jnp.exp(pos * lg_b)).astype(BF16)
    sf_ref[...] = _dot_tn(k_f, rv_ref[...])
    sb_ref[...] = _dot_tn(k_b, rv_ref[...])


def _ctx_mixer(p_ctx, sink, log_g, gn_g):
    c = p_ctx.shape[0]
    hb = lambda off: pl.BlockSpec((c, HEAD_DIM), lambda h: (0, off + h))
    return pl.pallas_call(
        _ctx_mixer_kernel,
        grid=(N_RET_HEADS,),
        in_specs=[
            pl.BlockSpec(memory_space=pltpu.SMEM),
            pl.BlockSpec(memory_space=pltpu.SMEM),
            hb(COL_Q),
            pl.BlockSpec((c, HEAD_DIM), lambda h: (0, COL_K + h // GQA_GROUP)),
            pl.BlockSpec((c, HEAD_DIM), lambda h: (0, COL_V + h // GQA_GROUP)),
            hb(COL_RQ), hb(COL_RK), hb(COL_RV), hb(COL_G),
            pl.BlockSpec((None, 1, HEAD_DIM), lambda h: (h, 0, 0)),
        ],
        out_specs=[
            pl.BlockSpec((c, HEAD_DIM), lambda h: (0, h)),
            pl.BlockSpec((c, HEAD_DIM), lambda h: (0, h)),
            pl.BlockSpec((None, HEAD_DIM, HEAD_DIM), lambda h: (h, 0, 0)),
            pl.BlockSpec((None, HEAD_DIM, HEAD_DIM), lambda h: (h, 0, 0)),
        ],
        out_shape=[
            jax.ShapeDtypeStruct((c, ATT_W), BF16),
            jax.ShapeDtypeStruct((c, RET_W), BF16),
            jax.ShapeDtypeStruct((N_RET_HEADS, HEAD_DIM, HEAD_DIM), F32),
            jax.ShapeDtypeStruct((N_RET_HEADS, HEAD_DIM, HEAD_DIM), F32),
        ],
        compiler_params=_cparams("arbitrary"),
        name="ctx_mixer",
    )(sink, log_g, p_ctx, p_ctx, p_ctx, p_ctx, p_ctx, p_ctx, p_ctx,
      gn_g.reshape(N_RET_HEADS, 1, HEAD_DIM))


def _ret_fwd_kernel(lg_ref, q_ref, k_ref, v_ref, s0_ref, y_ref, st, dmat, qdec, kdec, cdec, *, nz):
    h = pl.program_id(0)
    t = pl.program_id(1)

    @pl.when(t == 0)
    def _():
        lg_f = lg_ref[0, h]
        lg_b = lg_ref[1, h]
        ii = lax.broadcasted_iota(jnp.int32, (RET_CHUNK, RET_CHUNK), 0)
        jj = lax.broadcasted_iota(jnp.int32, (RET_CHUNK, RET_CHUNK), 1)
        rel = (ii - jj).astype(F32)
        dmat[...] = (jnp.where(rel >= 0, jnp.exp(jnp.maximum(rel, 0.0) * lg_f), 0.0)
                     + jnp.where(rel <= 0, jnp.exp(jnp.maximum(-rel, 0.0) * lg_b), 0.0))
        pos = ii.astype(F32)
        qdec[...] = jnp.exp((pos + 1.0) * lg_f)
        kdec[...] = jnp.exp((RET_CHUNK - 1.0 - pos) * lg_f)
        cdec[...] = jnp.exp(jnp.full((RET_CHUNK, RET_CHUNK), float(RET_CHUNK), F32) * lg_f)
        st[...] = s0_ref[...]

    def body(z, carry):
        rows = pl.ds(pl.multiple_of(z * RET_CHUNK, RET_CHUNK), RET_CHUNK)
        q = q_ref[rows, :]
        k = k_ref[rows, :]
        v = v_ref[rows, :]
        state = st[...]
        s = _dot_nt(q, k) * dmat[...]
        y = _dot(s.astype(BF16), v) + _dot(q, state.astype(BF16)) * qdec[...]
        y_ref[rows, :] = y
        kd = (k.astype(F32) * kdec[...]).astype(BF16)
        st[...] = cdec[...] * state + _dot_tn(kd, v)
        return carry

    lax.fori_loop(0, nz, body, 0)


def _ret_bwd_kernel(lg_ref, q_ref, k_ref, v_ref, g_ref, yf_ref, s0_ref, gn_ref, o_ref,
                    st, qdec, kdec, cdec, *, nz):
    h = pl.program_id(0)
    t = pl.program_id(1)

    @pl.when(t == 0)
    def _():
        lg_b = lg_ref[1, h]
        pos = lax.broadcasted_iota(jnp.int32, (RET_CHUNK, RET_CHUNK), 0).astype(F32)
        qdec[...] = jnp.exp((RET_CHUNK - pos) * lg_b)
        kdec[...] = jnp.exp(pos * lg_b)
        cdec[...] = jnp.exp(jnp.full((RET_CHUNK, RET_CHUNK), float(RET_CHUNK), F32) * lg_b)
        st[...] = s0_ref[...]

    def body(zz, carry):
        z = nz - 1 - zz
        rows = pl.ds(pl.multiple_of(z * RET_CHUNK, RET_CHUNK), RET_CHUNK)
        q = q_ref[rows, :]
        k = k_ref[rows, :]
        v = v_ref[rows, :]
        state = st[...]
        y = yf_ref[rows, :] + _dot(q, state.astype(BF16)) * qdec[...]
        kd = (k.astype(F32) * kdec[...]).astype(BF16)
        st[...] = cdec[...] * state + _dot_tn(kd, v)
        mu = jnp.mean(y, axis=-1, keepdims=True)
        var = jnp.mean(jnp.square(y - mu), axis=-1, keepdims=True)
        yn = (y - mu) * lax.rsqrt(var + EPS) * gn_ref[...]
        o_ref[rows, :] = (_silu(g_ref[rows, :].astype(F32)) * yn).astype(o_ref.dtype)
        return carry

    lax.fori_loop(0, nz, body, 0)


def _retention(p_lat, log_g, s_f, s_b, gn_g, tz):
    s = p_lat.shape[0]
    nt = s // tz
    nz = tz // RET_CHUNK
    sq = pltpu.VMEM((RET_CHUNK, RET_CHUNK), F32)
    smem = pl.BlockSpec(memory_space=pltpu.SMEM)
    state_spec = pl.BlockSpec((None, HEAD_DIM, HEAD_DIM), lambda h, t: (h, 0, 0))
    fwd = lambda off: pl.BlockSpec((tz, HEAD_DIM), lambda h, t: (t, off + h))
    bwd = lambda off: pl.BlockSpec((tz, HEAD_DIM), lambda h, t: (nt - 1 - t, off + h))
    y_f = pl.pallas_call(
        functools.partial(_ret_fwd_kernel, nz=nz),
        grid=(N_RET_HEADS, nt),
        in_specs=[smem, fwd(COL_RQ), fwd(COL_RK), fwd(COL_RV), state_spec],
        out_specs=fwd(0),
        out_shape=jax.ShapeDtypeStruct((s, RET_W), F32),
        scratch_shapes=[sq, sq, sq, sq, sq],
        compiler_params=_cparams("arbitrary", "arbitrary"),
        name="retention_fwd",
    )(log_g, p_lat, p_lat, p_lat, s_f)
    return pl.pallas_call(
        functools.partial(_ret_bwd_kernel, nz=nz),
        grid=(N_RET_HEADS, nt),
        in_specs=[smem, bwd(COL_RQ), bwd(COL_RK), bwd(COL_RV), bwd(COL_G), bwd(0), state_spec,
                  pl.BlockSpec((None, 1, HEAD_DIM), lambda h, t: (h, 0, 0))],
        out_specs=bwd(0),
        out_shape=jax.ShapeDtypeStruct((s, RET_W), BF16),
        scratch_shapes=[sq, sq, sq, sq],
        compiler_params=_cparams("arbitrary", "arbitrary"),
        name="retention_bwd",
    )(log_g, p_lat, p_lat, p_lat, p_lat, y_f, s_b, gn_g.reshape(N_RET_HEADS, 1, HEAD_DIM))


def _outproj_kernel(x_ref, a_ref, r_ref, wa_ref, wr_ref, gate_ref, o_ref, *, row):
    y = _dot(a_ref[...], wa_ref[...]) + _dot(r_ref[...], wr_ref[...])
    o_ref[...] = x_ref[...] + gate_ref[row:row + 1, :] * y


def _out_projection(x, att, ret, w_out, mods, layer, row, tm):
    m, d = x.shape
    return pl.pallas_call(
        functools.partial(_outproj_kernel, row=row),
        grid=(m // tm,),
        in_specs=[
            pl.BlockSpec((tm, d), lambda i: (i, 0)),
            pl.BlockSpec((tm, ATT_W), lambda i: (i, 0)),
            pl.BlockSpec((tm, RET_W), lambda i: (i, 0)),
            pl.BlockSpec((ATT_W, d), lambda i: (0, 0)),
            pl.BlockSpec((RET_W, d), lambda i: (ATT_W // RET_W, 0)),
            pl.BlockSpec((None, 8, d), lambda i: (layer, 0, 2)),
        ],
        out_specs=pl.BlockSpec((tm, d), lambda i: (i, 0)),
        out_shape=jax.ShapeDtypeStruct((m, d), F32),
        compiler_params=_cparams("arbitrary"),
        name="out_projection",
    )(x, att, ret, w_out, w_out, mods)


def _ffn_kernel(x_ref, sh_ref, sc_ref, gate_ref, g2_ref, w1_ref, w3_ref, w2_ref, o_ref,
                h_scr, acc, *, row):
    j = pl.program_id(1)

    @pl.when(j == 0)
    def _():
        h = _norm_modulate(x_ref[...], g2_ref[...], sh_ref[row:row + 1, :], sc_ref[row:row + 1, :])
        h_scr[...] = h.astype(BF16)
        acc[...] = jnp.zeros_like(acc)

    h = h_scr[...]
    u = (_silu(_dot(h, w1_ref[...])) * _dot(h, w3_ref[...])).astype(BF16)
    acc[...] += _dot(u, w2_ref[...])

    @pl.when(j == pl.num_programs(1) - 1)
    def _():
        o_ref[...] = x_ref[...] + gate_ref[row:row + 1, :] * acc[...]


def _dense_ffn(x, mods, layer, row, g2, w1, w3, w2, tm, tf):
    m, d = x.shape
    f = w1.shape[1]
    mod = lambda k: pl.BlockSpec((None, 8, d), lambda i, j: (layer, 0, k))
    return pl.pallas_call(
        functools.partial(_ffn_kernel, row=row),
        grid=(m // tm, f // tf),
        in_specs=[
            pl.BlockSpec((tm, d), lambda i, j: (i, 0)),
            mod(3), mod(4), mod(5),
            pl.BlockSpec((1, d), lambda i, j: (0, 0)),
            pl.BlockSpec((d, tf), lambda i, j: (0, j)),
            pl.BlockSpec((d, tf), lambda i, j: (0, j)),
            pl.BlockSpec((tf, d), lambda i, j: (j, 0)),
        ],
        out_specs=pl.BlockSpec((tm, d), lambda i, j: (i, 0)),
        out_shape=jax.ShapeDtypeStruct((m, d), F32),
        scratch_shapes=[pltpu.VMEM((tm, d), BF16), pltpu.VMEM((tm, d), F32)],
        compiler_params=_cparams("arbitrary", "arbitrary"),
        name="dense_ffn",
    )(x, mods, mods, mods, g2, w1, w3, w2)


def _router_kernel(x_ref, sh_ref, sc_ref, g2_ref, rw_ref, rb_ref, h_ref, info_ref, cnt_ref, carry, *, row):
    i = pl.program_id(0)
    tm = x_ref.shape[0]

    @pl.when(i == 0)
    def _():
        carry[...] = jnp.zeros_like(carry)

    h = _norm_modulate(x_ref[...], g2_ref[...], sh_ref[row:row + 1, :], sc_ref[row:row + 1, :])
    h_ref[...] = h
    logits = jnp.dot(h, rw_ref[...], preferred_element_type=F32,
                     precision=lax.Precision.HIGHEST) + rb_ref[...]
    lane = lax.broadcasted_iota(jnp.int32, (tm, LANES), 1).astype(F32)
    logits = jnp.where(lane < N_EXPERTS, logits, NEG_INF)
    v1 = jnp.max(logits, axis=-1, keepdims=True)
    e1 = jnp.min(jnp.where(logits == v1, lane, float(LANES)), axis=-1, keepdims=True)
    rest = jnp.where(lane == e1, NEG_INF, logits)
    v2 = jnp.max(rest, axis=-1, keepdims=True)
    e2 = jnp.min(jnp.where(rest == v2, lane, float(LANES)), axis=-1, keepdims=True)
    t = jnp.exp(v2 - v1)
    w1 = 1.0 / (1.0 + t)
    w2 = t / (1.0 + t)
    oh1 = jnp.where(lane == e1, 1.0, 0.0)
    oh2 = jnp.where(lane == e2, 1.0, 0.0)
    oh = oh1 + oh2
    r_i = lax.broadcasted_iota(jnp.int32, (tm, tm), 0)
    c_i = lax.broadcasted_iota(jnp.int32, (tm, tm), 1)
    tri = jnp.where(c_i < r_i, 1.0, 0.0).astype(BF16)
    before = _dot(tri, oh.astype(BF16)) + carry[0:1, :]
    rank1 = jnp.sum(before * oh1, axis=-1, keepdims=True)
    rank2 = jnp.sum(before * oh2, axis=-1, keepdims=True)
    carry[...] = carry[...] + jnp.sum(oh, axis=0, keepdims=True)
    info = jnp.where(lane == 0.0, e1, 0.0)
    info = jnp.where(lane == 1.0, e2, info)
    info = jnp.where(lane == 2.0, rank1, info)
    info = jnp.where(lane == 3.0, rank2, info)
    info = jnp.where(lane == 4.0, w1, info)
    info = jnp.where(lane == 5.0, w2, info)
    info_ref[...] = info
    cnt_ref[...] = carry[...]


def _router(x, mods, layer, row, g2, rw_pad, rb_pad, tm):
    m, d = x.shape
    mod = lambda k: pl.BlockSpec((None, 8, d), lambda i: (layer, 0, k))
    return pl.pallas_call(
        functools.partial(_router_kernel, row=row),
        grid=(m // tm,),
        in_specs=[
            pl.BlockSpec((tm, d), lambda i: (i, 0)),
            mod(3), mod(4),
            pl.BlockSpec((1, d), lambda i: (0, 0)),
            pl.BlockSpec((d, LANES), lambda i: (0, 0)),
            pl.BlockSpec((1, LANES), lambda i: (0, 0)),
        ],
        out_specs=[
            pl.BlockSpec((tm, d), lambda i: (i, 0)),
            pl.BlockSpec((tm, LANES), lambda i: (i, 0)),
            pl.BlockSpec((8, LANES), lambda i: (0, 0)),
        ],
        out_shape=[
            jax.ShapeDtypeStruct((m, d), F32),
            jax.ShapeDtypeStruct((m, LANES), F32),
            jax.ShapeDtypeStruct((8, LANES), F32),
        ],
        scratch_shapes=[pltpu.VMEM((8, LANES), F32)],
        compiler_params=_cparams("arbitrary"),
        name="moe_router",
    )(x, mods, mods, g2, rw_pad, rb_pad)


def _gather_rows(idx_ref, src_hbm, dst, sem, n):
    def body(k, carry):
        pltpu.make_async_copy(src_hbm.at[pl.ds(idx_ref[0, 0, k], 1)], dst.at[pl.ds(k, 1)], sem).start()
        return carry
    lax.fori_loop(0, n, body, 0)


def _wait_rows(src_hbm, dst, sem, n):
    pltpu.make_async_copy(src_hbm.at[pl.ds(0, n)], dst, sem).wait()


def _moe_ffn_kernel(te_ref, nu_ref, cur_ref, nxt_ref, h_hbm, w1_ref, w3_ref, w2_ref, o_ref,
                    buf, sems, h_scr, acc, *, tm):
    r = pl.program_id(0)
    j = pl.program_id(1)
    n_used = nu_ref[0]
    slot = r % 2

    @pl.when(r < n_used)
    def _():
        @pl.when(j == 0)
        def _():
            @pl.when(r == 0)
            def _():
                _gather_rows(cur_ref, h_hbm, buf.at[0], sems.at[0], tm)

            _wait_rows(h_hbm, buf.at[slot], sems.at[slot], tm)

            @pl.when(r + 1 < n_used)
            def _():
                _gather_rows(nxt_ref, h_hbm, buf.at[1 - slot], sems.at[1 - slot], tm)

            h_scr[...] = buf[slot].astype(BF16)
            acc[...] = jnp.zeros_like(acc)

        h = h_scr[...]
        u = (_silu(_dot(h, w1_ref[...])) * _dot(h, w3_ref[...])).astype(BF16)
        acc[...] += _dot(u, w2_ref[...])

        @pl.when(j == pl.num_programs(1) - 1)
        def _():
            o_ref[...] = acc[...]

    @pl.when((r >= n_used) & (j == 0))
    def _():
        o_ref[...] = jnp.zeros_like(o_ref)


def _moe_ffn(h2, tile_expert, n_used, src_rows, w1, w3, w2, tm, tf):
    d = h2.shape[1]
    f = w1.shape[2]
    n_tiles = src_rows.shape[0]
    clamp = lambda r, nu: jnp.minimum(r, nu[0] - 1)
    grid_spec = pltpu.PrefetchScalarGridSpec(
        num_scalar_prefetch=2,
        grid=(n_tiles, f // tf),
        in_specs=[
            pl.BlockSpec((1, 1, tm), lambda r, j, te, nu: (clamp(r, nu), 0, 0), memory_space=pltpu.SMEM),
            pl.BlockSpec((1, 1, tm), lambda r, j, te, nu: (clamp(r + 1, nu), 0, 0), memory_space=pltpu.SMEM),
            pl.BlockSpec(memory_space=pl.ANY),
            pl.BlockSpec((None, d, tf), lambda r, j, te, nu: (te[clamp(r, nu)], 0, jnp.where(r < nu[0], j, f // tf - 1))),
            pl.BlockSpec((None, d, tf), lambda r, j, te, nu: (te[clamp(r, nu)], 0, jnp.where(r < nu[0], j, f // tf - 1))),
            pl.BlockSpec((None, tf, d), lambda r, j, te, nu: (te[clamp(r, nu)], jnp.where(r < nu[0], j, f // tf - 1), 0)),
        ],
        out_specs=pl.BlockSpec((tm, d), lambda r, j, te, nu: (r, 0)),
        scratch_shapes=[
            pltpu.VMEM((2, tm, d), F32),
            pltpu.SemaphoreType.DMA((2,)),
            pltpu.VMEM((tm, d), BF16),
            pltpu.VMEM((tm, d), F32),
        ],
    )
    return pl.pallas_call(
        functools.partial(_moe_ffn_kernel, tm=tm),
        grid_spec=grid_spec,
        out_shape=jax.ShapeDtypeStruct((n_tiles * tm, d), F32),
        compiler_params=_cparams("arbitrary", "arbitrary"),
        name="moe_ffn",
    )(tile_expert, n_used, src_rows, src_rows, h2, w1, w3, w2)


def _combine_kernel(cur_ref, nxt_ref, x_ref, info_ref, gate_ref, y_hbm, o_ref, buf, sems, *, tm, row):
    i = pl.program_id(0)
    slot = i % 2

    @pl.when(i == 0)
    def _():
        _gather_rows(cur_ref, y_hbm, buf.at[0], sems.at[0], 2 * tm)

    _wait_rows(y_hbm, buf.at[slot], sems.at[slot], 2 * tm)

    @pl.when(i + 1 < pl.num_programs(0))
    def _():
        _gather_rows(nxt_ref, y_hbm, buf.at[1 - slot], sems.at[1 - slot], 2 * tm)

    info = info_ref[...]
    w1 = info[:, 4:5]
    w2 = info[:, 5:6]
    y = w1 * buf[slot, 0:tm, :] + w2 * buf[slot, tm:2 * tm, :]
    o_ref[...] = x_ref[...] + gate_ref[row:row + 1, :] * y


def _moe_combine(x, info, mods, layer, row, y_rows, pos_tiles, tm):
    m, d = x.shape
    nt = m // tm
    return pl.pallas_call(
        functools.partial(_combine_kernel, tm=tm, row=row),
        grid=(nt,),
        in_specs=[
            pl.BlockSpec((1, 1, 2 * tm), lambda i: (i, 0, 0), memory_space=pltpu.SMEM),
            pl.BlockSpec((1, 1, 2 * tm), lambda i: (jnp.minimum(i + 1, nt - 1), 0, 0), memory_space=pltpu.SMEM),
            pl.BlockSpec((tm, d), lambda i: (i, 0)),
            pl.BlockSpec((tm, LANES), lambda i: (i, 0)),
            pl.BlockSpec((None, 8, d), lambda i: (layer, 0, 5)),
            pl.BlockSpec(memory_space=pl.ANY),
        ],
        out_specs=pl.BlockSpec((tm, d), lambda i: (i, 0)),
        out_shape=jax.ShapeDtypeStruct((m, d), F32),
        scratch_shapes=[pltpu.VMEM((2, 2 * tm, d), F32), pltpu.SemaphoreType.DMA((2,))],
        compiler_params=_cparams("arbitrary"),
        name="moe_combine",
    )(pos_tiles, pos_tiles, x, info, mods, y_rows)


def _moe_layer(x, mods, layer, row, g2, router_w, router_b, w1, w3, w2, tm_route, tm_exp, tf):
    m, d = x.shape
    rw_pad = jnp.pad(router_w, ((0, 0), (0, LANES - N_EXPERTS)))
    rb_pad = jnp.pad(router_b, (0, LANES - N_EXPERTS)).reshape(1, LANES)
    h2, info, cnt = _router(x, mods, layer, row, g2, rw_pad, rb_pad, tm_route)
    counts = cnt[0, :N_EXPERTS].astype(jnp.int32)
    tiles_per = (counts + tm_exp - 1) // tm_exp
    tile_end = jnp.cumsum(tiles_per)
    group_start = (tile_end - tiles_per) * tm_exp
    n_tiles = (2 * m) // tm_exp + N_EXPERTS
    n_used = tile_end[-1:].astype(jnp.int32)
    tile_expert = jnp.minimum(
        jnp.searchsorted(tile_end, jnp.arange(n_tiles, dtype=jnp.int32), side="right"),
        N_EXPERTS - 1).astype(jnp.int32)
    e12 = info[:, 0:2].astype(jnp.int32)
    rank12 = info[:, 2:4].astype(jnp.int32)
    pos = group_start[e12] + rank12
    tok = jnp.broadcast_to(jnp.arange(m, dtype=jnp.int32)[:, None], (m, 2))
    src_rows = jnp.zeros((n_tiles * tm_exp,), jnp.int32).at[pos.reshape(-1)].set(tok.reshape(-1))
    y_rows = _moe_ffn(h2, tile_expert, n_used, src_rows.reshape(n_tiles, 1, tm_exp), w1, w3, w2, tm_exp, tf)
    nt = m // tm_route
    pos_tiles = pos.reshape(nt, tm_route, 2).transpose(0, 2, 1).reshape(nt, 1, 2 * tm_route)
    return _moe_combine(x, info, mods, layer, row, y_rows, pos_tiles, tm_route)


def _rope_tables(n_tokens):
    n_rows = n_tokens // GRID_W
    row = jnp.repeat(jnp.arange(n_rows), GRID_W)
    col = jnp.tile(jnp.arange(GRID_W), n_rows)
    inv = ROPE_THETA ** (-jnp.arange(ROPE_PAIRS, dtype=F32) / ROPE_PAIRS)
    ang = jnp.stack([row, col], axis=-1).astype(F32)[:, :, None] * inv
    cos, sin = jnp.cos(ang), jnp.sin(ang)
    zero = jnp.zeros_like(sin[:, 0])
    cos_t = jnp.concatenate([cos[:, 0], cos[:, 0], cos[:, 1], cos[:, 1]], axis=-1)
    sa_t = jnp.concatenate([-sin[:, 0], zero, -sin[:, 1], zero], axis=-1)
    sb_t = jnp.concatenate([zero, sin[:, 0], zero, sin[:, 1]], axis=-1)
    return cos_t, sa_t, sb_t


def kernel(x, c, ctx, c_ctx, ada_w, ada_b, norm1_g, norm2_g, w_in, q_norm_g, k_norm_g, attn_sink,
           ret_decay, ret_gn_g, w_out, ffn_w1, ffn_w3, ffn_w2, router_w, router_b, moe_w1, moe_w3, moe_w2):
    b, s, d = x.shape
    n_ctx = ctx.shape[1]
    depth = ada_w.shape[0]
    assert b == 1, "one latent sequence per call"
    x_lat = x[0]
    x_ctx = ctx[0]

    tm_lat = min(1024, s)
    tm_ffn = min(512, s)
    tq = min(512, s)
    tz = min(2048, s)
    tf = 512

    rows = jnp.concatenate([c[0:1], c_ctx[None, :], jnp.zeros((6, d), F32)], axis=0)
    mods = _ada_mods(rows, ada_w, ada_b)
    cos_l, sa_l, sb_l = _rope_tables(s)
    cos_c = jnp.ones((n_ctx, HEAD_DIM), F32)
    zer_c = jnp.zeros((n_ctx, HEAD_DIM), F32)
    log_g_all = jax.nn.log_sigmoid(ret_decay.astype(F32))

    for i in range(depth):
        last = i == depth - 1
        w_in_i = w_in[i].astype(BF16)
        w_out_i = w_out[i].astype(BF16)
        g1 = norm1_g[i].reshape(1, d)
        g2 = norm2_g[i].reshape(1, d)
        qg = q_norm_g[i].reshape(1, HEAD_DIM)
        kg = k_norm_g[i].reshape(1, HEAD_DIM)
        log_g = log_g_all[i]

        p_lat = _in_projection(x_lat, mods, i, 0, g1, w_in_i, cos_l, sa_l, sb_l, qg, kg, tm_lat)
        p_ctx = _in_projection(x_ctx, mods, i, 1, g1, w_in_i, cos_c, zer_c, zer_c, qg, kg, n_ctx)

        att_c, ret_c, s_f, s_b = _ctx_mixer(p_ctx, attn_sink[i], log_g, ret_gn_g[i])
        att_l = _window_attention(p_lat, p_ctx, attn_sink[i], tq)
        ret_l = _retention(p_lat, log_g, s_f, s_b, ret_gn_g[i], tz)
        x_lat = _out_projection(x_lat, att_l, ret_l, w_out_i, mods, i, 0, tm_ffn)

        j = i // 2
        if i % 2 == 0:
            w1 = ffn_w1[j].astype(BF16)
            w3 = ffn_w3[j].astype(BF16)
            w2 = ffn_w2[j].astype(BF16)
            x_lat = _dense_ffn(x_lat, mods, i, 0, g2, w1, w3, w2, tm_ffn, tf)
        else:
            x_lat = _moe_layer(x_lat, mods, i, 0, g2, router_w[j], router_b[j],
                               moe_w1[j].astype(BF16), moe_w3[j].astype(BF16), moe_w2[j].astype(BF16),
                               tm_ffn, tm_ffn, tf)

        if not last:
            x_ctx = _out_projection(x_ctx, att_c, ret_c, w_out_i, mods, i, 1, n_ctx)
            if i % 2 == 0:
                x_ctx = _dense_ffn(x_ctx, mods, i, 1, g2, w1, w3, w2, n_ctx, tf)
            else:
                x_ctx = _moe_layer(x_ctx, mods, i, 1, g2, router_w[j], router_b[j],
                                   moe_w1[j].astype(BF16), moe_w3[j].astype(BF16), moe_w2[j].astype(BF16),
                                   n_ctx, n_ctx, tf)
    return x_lat[None]
```

```python
import functools

import jax
import jax.numpy as jnp
from jax import lax
from jax.experimental import pallas as pl
from jax.experimental.pallas import tpu as pltpu

F32 = jnp.float32
BF16 = jnp.bfloat16

HEAD_DIM = 128
N_ATT_HEADS = 8
N_KV_HEADS = 2
N_RET_HEADS = 8
GQA_GROUP = N_ATT_HEADS // N_KV_HEADS
ATT_W = N_ATT_HEADS * HEAD_DIM
KV_W = N_KV_HEADS * HEAD_DIM
RET_W = N_RET_HEADS * HEAD_DIM
IN_COLS = ATT_W + 2 * KV_W + 4 * RET_W
WINDOW = 128
RET_CHUNK = 128
GRID_W = 64
ROPE_THETA = 10000.0
ROPE_PAIRS = HEAD_DIM // 4
N_EXPERTS = 8
EPS = 1e-6
ATT_SCALE = HEAD_DIM ** -0.5

COL_Q = 0
COL_K = ATT_W // HEAD_DIM
COL_V = COL_K + N_KV_HEADS
COL_RQ = COL_V + N_KV_HEADS
COL_RK = COL_RQ + N_RET_HEADS
COL_RV = COL_RK + N_RET_HEADS
COL_G = COL_RV + N_RET_HEADS

VMEM_LIMIT_BYTES = 56 * 1024 * 1024
LANES = 128

NEG_INF = float("-inf")


def _cparams(*sem):
    return pltpu.CompilerParams(dimension_semantics=sem, vmem_limit_bytes=VMEM_LIMIT_BYTES)


def _silu(x):
    return x * (1.0 / (1.0 + jnp.exp(-x)))


def _dot(a, b):
    return jnp.dot(a, b, preferred_element_type=F32)


def _dot_nt(a, b):
    return lax.dot_general(a, b, (((1,), (1,)), ((), ())), preferred_element_type=F32)


def _dot_tn(a, b):
    return lax.dot_general(a, b, (((0,), (0,)), ((), ())), preferred_element_type=F32)


def _norm_modulate(x, g, shift, scale):
    ms = jnp.mean(x * x, axis=-1, keepdims=True)
    return (x * lax.rsqrt(ms + EPS) * g) * (1.0 + scale) + shift


def _ada_kernel(r_ref, w_ref, b_ref, o_ref):
    s = _silu(r_ref[...])
    o_ref[...] = jnp.dot(s, w_ref[...], preferred_element_type=F32,
                         precision=lax.Precision.HIGHEST) + b_ref[...]


def _ada_mods(rows, ada_w, ada_b):
    depth, d, n = ada_w.shape
    tn = d // 2
    return pl.pallas_call(
        _ada_kernel,
        grid=(depth, n // tn),
        in_specs=[
            pl.BlockSpec((8, d), lambda l, j: (0, 0)),
            pl.BlockSpec((None, d, tn), lambda l, j: (l, 0, j)),
            pl.BlockSpec((None, 1, tn), lambda l, j: (l, 0, j)),
        ],
        out_specs=pl.BlockSpec((None, 8, tn), lambda l, j: (l, 0, j)),
        out_shape=jax.ShapeDtypeStruct((depth, 8, n), F32),
        compiler_params=_cparams("arbitrary", "arbitrary"),
        name="ada_mods",
    )(rows, ada_w, ada_b.reshape(depth, 1, n))


def _rope(y, cos, sa, sb):
    return y * cos + pltpu.roll(y, 96, 1) * sa + pltpu.roll(y, 32, 1) * sb


def _head_norm(p, g):
    ms = jnp.mean(p * p, axis=-1, keepdims=True)
    return p * lax.rsqrt(ms + EPS) * g


def _inproj_kernel(x_ref, sh_ref, sc_ref, g1_ref, w_ref, cos_ref, sa_ref, sb_ref,
                   qg_ref, kg_ref, o_ref, h_scr, *, row, tn):
    j = pl.program_id(1)
    nchunk = tn // HEAD_DIM

    @pl.when(j == 0)
    def _():
        h = _norm_modulate(x_ref[...], g1_ref[...], sh_ref[row:row + 1, :], sc_ref[row:row + 1, :])
        h_scr[...] = h.astype(BF16)

    p = _dot(h_scr[...], w_ref[...])

    def chunk(c):
        return p[:, c * HEAD_DIM:(c + 1) * HEAD_DIM]

    def put(c, y):
        o_ref[:, c * HEAD_DIM:(c + 1) * HEAD_DIM] = y.astype(o_ref.dtype)

    def rope(y):
        return _rope(y, cos_ref[...], sa_ref[...], sb_ref[...])

    jq = ATT_W // tn
    jr0 = (ATT_W + 2 * KV_W) // tn
    jr1 = jr0 + RET_W // tn
    jr2 = jr1 + RET_W // tn

    @pl.when(j < jq)
    def _():
        for c in range(nchunk):
            put(c, rope(_head_norm(chunk(c), qg_ref[...]) * ATT_SCALE))

    @pl.when(j == jq)
    def _():
        for c in range(nchunk):
            if c < N_KV_HEADS:
                put(c, rope(_head_norm(chunk(c), kg_ref[...])))
            else:
                put(c, chunk(c))

    @pl.when((j >= jr0) & (j < jr1))
    def _():
        for c in range(nchunk):
            put(c, rope(chunk(c)))

    @pl.when((j >= jr1) & (j < jr2))
    def _():
        for c in range(nchunk):
            put(c, rope(chunk(c)) * ATT_SCALE)

    @pl.when(j >= jr2)
    def _():
        for c in range(nchunk):
            put(c, chunk(c))


def _in_projection(x, mods, layer, row, g1, w, cos, sa, sb, qg, kg, tm):
    m, d = x.shape
    n = w.shape[1]
    tn = 2 * KV_W
    assert n == IN_COLS and ATT_W % tn == 0 and RET_W % tn == 0 and m % tm == 0
    return pl.pallas_call(
        functools.partial(_inproj_kernel, row=row, tn=tn),
        grid=(m // tm, n // tn),
        in_specs=[
            pl.BlockSpec((tm, d), lambda i, j: (i, 0)),
            pl.BlockSpec((None, 8, d), lambda i, j: (layer, 0, 0)),
            pl.BlockSpec((None, 8, d), lambda i, j: (layer, 0, 1)),
            pl.BlockSpec((1, d), lambda i, j: (0, 0)),
            pl.BlockSpec((d, tn), lambda i, j: (0, j)),
            pl.BlockSpec((tm, HEAD_DIM), lambda i, j: (i, 0)),
            pl.BlockSpec((tm, HEAD_DIM), lambda i, j: (i, 0)),
            pl.BlockSpec((tm, HEAD_DIM), lambda i, j: (i, 0)),
            pl.BlockSpec((1, HEAD_DIM), lambda i, j: (0, 0)),
            pl.BlockSpec((1, HEAD_DIM), lambda i, j: (0, 0)),
        ],
        out_specs=pl.BlockSpec((tm, tn), lambda i, j: (i, j)),
        out_shape=jax.ShapeDtypeStruct((m, n), BF16),
        scratch_shapes=[pltpu.VMEM((tm, d), BF16)],
        compiler_params=_cparams("arbitrary", "arbitrary"),
        name="in_projection",
    )(x, mods, mods, g1, w, cos, sa, sb, qg, kg)


def _attn_kernel(sink_ref, q_ref, km_ref, vm_ref, kp_ref, vp_ref, kn_ref, vn_ref,
                 kc_ref, vc_ref, bias_ref, o_ref, kbuf, vbuf, *, tq):
    i = pl.program_id(0)
    last = pl.num_programs(0) - 1
    nsub = tq // WINDOW
    kbuf[0:WINDOW] = kp_ref[...]
    kbuf[WINDOW:WINDOW + tq] = km_ref[...]
    kbuf[WINDOW + tq:] = kn_ref[...]
    vbuf[0:WINDOW] = vp_ref[...]
    vbuf[WINDOW:WINDOW + tq] = vm_ref[...]
    vbuf[WINDOW + tq:] = vn_ref[...]
    col = lax.broadcasted_iota(jnp.int32, (WINDOW, 3 * WINDOW), 1)
    for s in range(nsub):
        bias = bias_ref[...]
        if s == 0:
            bias = bias + jnp.where(col < WINDOW, jnp.where(i == 0, NEG_INF, 0.0), 0.0)
        if s == nsub - 1:
            bias = bias + jnp.where(col >= 2 * WINDOW, jnp.where(i == last, NEG_INF, 0.0), 0.0)
        bias4 = jnp.concatenate([bias] * GQA_GROUP, axis=0)
        for h in range(N_KV_HEADS):
            hs = slice(h * HEAD_DIM, (h + 1) * HEAD_DIM)
            q = jnp.concatenate(
                [q_ref[s * WINDOW:(s + 1) * WINDOW,
                       (h * GQA_GROUP + g) * HEAD_DIM:(h * GQA_GROUP + g + 1) * HEAD_DIM]
                 for g in range(GQA_GROUP)], axis=0)
            kl = kbuf[s * WINDOW:(s + 3) * WINDOW, hs]
            vl = vbuf[s * WINDOW:(s + 3) * WINDOW, hs]
            s_loc = _dot_nt(q, kl) + bias4
            s_ctx = _dot_nt(q, kc_ref[:, hs])
            sink = jnp.concatenate(
                [jnp.full((WINDOW, 1), sink_ref[h * GQA_GROUP + g], F32) for g in range(GQA_GROUP)],
                axis=0)
            m = jnp.maximum(jnp.maximum(jnp.max(s_loc, axis=-1, keepdims=True),
                                        jnp.max(s_ctx, axis=-1, keepdims=True)), sink)
            p_loc = jnp.exp(s_loc - m)
            p_ctx = jnp.exp(s_ctx - m)
            den = (jnp.sum(p_loc, axis=-1, keepdims=True) + jnp.sum(p_ctx, axis=-1, keepdims=True)
                   + jnp.exp(sink - m))
            o = (_dot(p_loc.astype(BF16), vl) + _dot(p_ctx.astype(BF16), vc_ref[:, hs])) / den
            for g in range(GQA_GROUP):
                o_ref[s * WINDOW:(s + 1) * WINDOW,
                      (h * GQA_GROUP + g) * HEAD_DIM:(h * GQA_GROUP + g + 1) * HEAD_DIM] = (
                    o[g * WINDOW:(g + 1) * WINDOW].astype(o_ref.dtype))


def _band_bias():
    r = jnp.arange(WINDOW)[:, None]
    c = jnp.arange(3 * WINDOW)[None, :]
    ok = (c - r >= 0) & (c - r <= 2 * WINDOW)
    return jnp.where(ok, 0.0, NEG_INF).astype(F32)


def _window_attention(p_lat, p_ctx, sink, tq):
    s = p_lat.shape[0]
    c = p_ctx.shape[0]
    r = tq // WINDOW
    nb = s // WINDOW
    kvb = KV_W // HEAD_DIM
    ck, cv = COL_K // kvb, COL_V // kvb
    return pl.pallas_call(
        functools.partial(_attn_kernel, tq=tq),
        grid=(s // tq,),
        in_specs=[
            pl.BlockSpec(memory_space=pltpu.SMEM),
            pl.BlockSpec((tq, ATT_W), lambda i: (i, 0)),
            pl.BlockSpec((tq, KV_W), lambda i: (i, ck)),
            pl.BlockSpec((tq, KV_W), lambda i: (i, cv)),
            pl.BlockSpec((WINDOW, KV_W), lambda i: (jnp.maximum(i * r - 1, 0), ck)),
            pl.BlockSpec((WINDOW, KV_W), lambda i: (jnp.maximum(i * r - 1, 0), cv)),
            pl.BlockSpec((WINDOW, KV_W), lambda i: (jnp.minimum((i + 1) * r, nb - 1), ck)),
            pl.BlockSpec((WINDOW, KV_W), lambda i: (jnp.minimum((i + 1) * r, nb - 1), cv)),
            pl.BlockSpec((c, KV_W), lambda i: (0, ck)),
            pl.BlockSpec((c, KV_W), lambda i: (0, cv)),
            pl.BlockSpec((WINDOW, 3 * WINDOW), lambda i: (0, 0)),
        ],
        out_specs=pl.BlockSpec((tq, ATT_W), lambda i: (i, 0)),
        out_shape=jax.ShapeDtypeStruct((s, ATT_W), BF16),
        scratch_shapes=[pltpu.VMEM((tq + 2 * WINDOW, KV_W), BF16),
                        pltpu.VMEM((tq + 2 * WINDOW, KV_W), BF16)],
        compiler_params=_cparams("arbitrary"),
        name="window_attention",
    )(sink, p_lat, p_lat, p_lat, p_lat, p_lat, p_lat, p_lat, p_ctx, p_ctx, _band_bias())


def _ctx_mixer_kernel(sink_ref, lg_ref, q_ref, k_ref, v_ref, rq_ref, rk_ref, rv_ref, g_ref, gn_ref,
                      att_ref, ret_ref, sf_ref, sb_ref):
    h = pl.program_id(0)
    c = q_ref.shape[0]
    lg_f = lg_ref[0, h]
    lg_b = lg_ref[1, h]
    s = _dot_nt(q_ref[...], k_ref[...])
    sink = jnp.full((c, 1), sink_ref[h], F32)
    m = jnp.maximum(jnp.max(s, axis=-1, keepdims=True), sink)
    p = jnp.exp(s - m)
    den = jnp.sum(p, axis=-1, keepdims=True) + jnp.exp(sink - m)
    att_ref[...] = (_dot(p.astype(BF16), v_ref[...]) / den).astype(att_ref.dtype)
    n_i = lax.broadcasted_iota(jnp.int32, (c, c), 0)
    m_i = lax.broadcasted_iota(jnp.int32, (c, c), 1)
    rel = (n_i - m_i).astype(F32)
    dec = (jnp.where(rel >= 0, jnp.exp(jnp.maximum(rel, 0.0) * lg_f), 0.0)
           + jnp.where(rel <= 0, jnp.exp(jnp.maximum(-rel, 0.0) * lg_b), 0.0))
    sc = _dot_nt(rq_ref[...], rk_ref[...]) * dec
    y = _dot(sc.astype(BF16), rv_ref[...])
    mu = jnp.mean(y, axis=-1, keepdims=True)
    var = jnp.mean(jnp.square(y - mu), axis=-1, keepdims=True)
    yn = (y - mu) * lax.rsqrt(var + EPS) * gn_ref[...]
    ret_ref[...] = (_silu(g_ref[...].astype(F32)) * yn).astype(ret_ref.dtype)
    pos = lax.broadcasted_iota(jnp.int32, (c, HEAD_DIM), 0).astype(F32)
    kf = rk_ref[...].astype(F32)
    k_f = (kf * jnp.exp((c - 1.0 - pos) * lg_f)).astype(BF16)
    k_b = (kf * jnp.exp(pos * lg_b)).astype(BF16)
    sf_ref[...] = _dot_tn(k_f, rv_ref[...])
    sb_ref[...] = _dot_tn(k_b, rv_ref[...])


def _ctx_mixer(p_ctx, sink, log_g, gn_g):
    c = p_ctx.shape[0]
    hb = lambda off: pl.BlockSpec((c, HEAD_DIM), lambda h: (0, off + h))
    return pl.pallas_call(
        _ctx_mixer_kernel,
        grid=(N_RET_HEADS,),
        in_specs=[
            pl.BlockSpec(memory_space=pltpu.SMEM),
            pl.BlockSpec(memory_space=pltpu.SMEM),
            hb(COL_Q),
            pl.BlockSpec((c, HEAD_DIM), lambda h: (0, COL_K + h // GQA_GROUP)),
            pl.BlockSpec((c, HEAD_DIM), lambda h: (0, COL_V + h // GQA_GROUP)),
            hb(COL_RQ), hb(COL_RK), hb(COL_RV), hb(COL_G),
            pl.BlockSpec((None, 1, HEAD_DIM), lambda h: (h, 0, 0)),
        ],
        out_specs=[
            pl.BlockSpec((c, HEAD_DIM), lambda h: (0, h)),
            pl.BlockSpec((c, HEAD_DIM), lambda h: (0, h)),
            pl.BlockSpec((None, HEAD_DIM, HEAD_DIM), lambda h: (h, 0, 0)),
            pl.BlockSpec((None, HEAD_DIM, HEAD_DIM), lambda h: (h, 0, 0)),
        ],
        out_shape=[
            jax.ShapeDtypeStruct((c, ATT_W), BF16),
            jax.ShapeDtypeStruct((c, RET_W), BF16),
            jax.ShapeDtypeStruct((N_RET_HEADS, HEAD_DIM, HEAD_DIM), F32),
            jax.ShapeDtypeStruct((N_RET_HEADS, HEAD_DIM, HEAD_DIM), F32),
        ],
        compiler_params=_cparams("arbitrary"),
        name="ctx_mixer",
    )(sink, log_g, p_ctx, p_ctx, p_ctx, p_ctx, p_ctx, p_ctx, p_ctx,
      gn_g.reshape(N_RET_HEADS, 1, HEAD_DIM))


RET_KERNEL_CHUNK = 256
RET_UNROLL = 4


def _ret_kernel(lg_ref, q_ref, k_ref, v_ref, g_ref, sf_ref, sb_ref, gn_ref, o_ref,
                y_scr, dmat, qdf, kdf, qdb, kdb, cdf, cdb):
    h = pl.program_id(0)
    ck = RET_KERNEL_CHUNK
    nit = q_ref.shape[0] // (ck * RET_UNROLL)
    lg_f = lg_ref[0, h]
    lg_b = lg_ref[1, h]
    ii = lax.broadcasted_iota(jnp.int32, (ck, ck), 0)
    jj = lax.broadcasted_iota(jnp.int32, (ck, ck), 1)
    rel = (ii - jj).astype(F32)
    dmat[...] = (jnp.where(rel >= 0, jnp.exp(jnp.maximum(rel, 0.0) * lg_f), 0.0)
                 + jnp.where(rel <= 0, jnp.exp(jnp.maximum(-rel, 0.0) * lg_b), 0.0))
    pos = lax.broadcasted_iota(jnp.int32, (ck, HEAD_DIM), 0).astype(F32)
    qdf[...] = jnp.exp((pos + 1.0) * lg_f)
    kdf[...] = jnp.exp((ck - 1.0 - pos) * lg_f)
    qdb[...] = jnp.exp((ck - pos) * lg_b)
    kdb[...] = jnp.exp(pos * lg_b)
    full = jnp.full((HEAD_DIM, HEAD_DIM), float(ck), F32)
    cdf[...] = jnp.exp(full * lg_f)
    cdb[...] = jnp.exp(full * lg_b)

    def rows_of(z):
        return pl.ds(pl.multiple_of(z * ck, ck), ck)

    def fwd(it, state):
        zs = [it * RET_UNROLL + u for u in range(RET_UNROLL)]
        qs = [q_ref[rows_of(z), :] for z in zs]
        ks = [k_ref[rows_of(z), :] for z in zs]
        vs = [v_ref[rows_of(z), :] for z in zs]
        inner = [_dot((_dot_nt(q, k) * dmat[...]).astype(BF16), v) for q, k, v in zip(qs, ks, vs)]
        kvs = [_dot_tn((k.astype(F32) * kdf[...]).astype(BF16), v) for k, v in zip(ks, vs)]
        for u, z in enumerate(zs):
            y_scr[rows_of(z), :] = inner[u] + _dot(qs[u], state.astype(BF16)) * qdf[...]
            state = cdf[...] * state + kvs[u]
        return state

    lax.fori_loop(0, nit, fwd, sf_ref[...])

    def bwd(it, state):
        zs = [nit * RET_UNROLL - 1 - (it * RET_UNROLL + u) for u in range(RET_UNROLL)]
        qs = [q_ref[rows_of(z), :] for z in zs]
        kvs = [_dot_tn((k_ref[rows_of(z), :].astype(F32) * kdb[...]).astype(BF16), v_ref[rows_of(z), :])
               for z in zs]
        for u, z in enumerate(zs):
            y = y_scr[rows_of(z), :] + _dot(qs[u], state.astype(BF16)) * qdb[...]
            state = cdb[...] * state + kvs[u]
            mu = jnp.mean(y, axis=-1, keepdims=True)
            var = jnp.mean(jnp.square(y - mu), axis=-1, keepdims=True)
            yn = (y - mu) * lax.rsqrt(var + EPS) * gn_ref[...]
            o_ref[rows_of(z), :] = (_silu(g_ref[rows_of(z), :].astype(F32)) * yn).astype(o_ref.dtype)
        return state

    lax.fori_loop(0, nit, bwd, sb_ref[...])


def _retention(p_lat, log_g, s_f, s_b, gn_g):
    s = p_lat.shape[0]
    ck = RET_KERNEL_CHUNK
    assert s % (ck * RET_UNROLL) == 0
    col = lambda off: pl.BlockSpec((s, HEAD_DIM), lambda h: (0, off + h))
    state_spec = pl.BlockSpec((None, HEAD_DIM, HEAD_DIM), lambda h: (h, 0, 0))
    vec = pltpu.VMEM((ck, HEAD_DIM), F32)
    sq = pltpu.VMEM((HEAD_DIM, HEAD_DIM), F32)
    return pl.pallas_call(
        _ret_kernel,
        grid=(N_RET_HEADS,),
        in_specs=[pl.BlockSpec(memory_space=pltpu.SMEM),
                  col(COL_RQ), col(COL_RK), col(COL_RV), col(COL_G), state_spec, state_spec,
                  pl.BlockSpec((None, 1, HEAD_DIM), lambda h: (h, 0, 0))],
        out_specs=col(0),
        out_shape=jax.ShapeDtypeStruct((s, RET_W), BF16),
        scratch_shapes=[pltpu.VMEM((s, HEAD_DIM), F32), pltpu.VMEM((ck, ck), F32),
                        vec, vec, vec, vec, sq, sq],
        compiler_params=_cparams("arbitrary"),
        name="retention",
    )(log_g, p_lat, p_lat, p_lat, p_lat, s_f, s_b, gn_g.reshape(N_RET_HEADS, 1, HEAD_DIM))


def _outproj_kernel(x_ref, a_ref, r_ref, wa_ref, wr_ref, gate_ref, o_ref, *, row):
    y = _dot(a_ref[...], wa_ref[...]) + _dot(r_ref[...], wr_ref[...])
    o_ref[...] = x_ref[...] + gate_ref[row:row + 1, :] * y


def _out_projection(x, att, ret, w_out, mods, layer, row, tm):
    m, d = x.shape
    return pl.pallas_call(
        functools.partial(_outproj_kernel, row=row),
        grid=(m // tm,),
        in_specs=[
            pl.BlockSpec((tm, d), lambda i: (i, 0)),
            pl.BlockSpec((tm, ATT_W), lambda i: (i, 0)),
            pl.BlockSpec((tm, RET_W), lambda i: (i, 0)),
            pl.BlockSpec((ATT_W, d), lambda i: (0, 0)),
            pl.BlockSpec((RET_W, d), lambda i: (ATT_W // RET_W, 0)),
            pl.BlockSpec((None, 8, d), lambda i: (layer, 0, 2)),
        ],
        out_specs=pl.BlockSpec((tm, d), lambda i: (i, 0)),
        out_shape=jax.ShapeDtypeStruct((m, d), F32),
        compiler_params=_cparams("arbitrary"),
        name="out_projection",
    )(x, att, ret, w_out, w_out, mods)


def _ffn_kernel(x_ref, sh_ref, sc_ref, gate_ref, g2_ref, w1_ref, w3_ref, w2_ref, o_ref,
                h_scr, acc, *, row):
    j = pl.program_id(1)

    @pl.when(j == 0)
    def _():
        h = _norm_modulate(x_ref[...], g2_ref[...], sh_ref[row:row + 1, :], sc_ref[row:row + 1, :])
        h_scr[...] = h.astype(BF16)
        acc[...] = jnp.zeros_like(acc)

    h = h_scr[...]
    u = (_silu(_dot(h, w1_ref[...])) * _dot(h, w3_ref[...])).astype(BF16)
    acc[...] += _dot(u, w2_ref[...])

    @pl.when(j == pl.num_programs(1) - 1)
    def _():
        o_ref[...] = x_ref[...] + gate_ref[row:row + 1, :] * acc[...]


def _dense_ffn(x, mods, layer, row, g2, w1, w3, w2, tm, tf):
    m, d = x.shape
    f = w1.shape[1]
    mod = lambda k: pl.BlockSpec((None, 8, d), lambda i, j: (layer, 0, k))
    return pl.pallas_call(
        functools.partial(_ffn_kernel, row=row),
        grid=(m // tm, f // tf),
        in_specs=[
            pl.BlockSpec((tm, d), lambda i, j: (i, 0)),
            mod(3), mod(4), mod(5),
            pl.BlockSpec((1, d), lambda i, j: (0, 0)),
            pl.BlockSpec((d, tf), lambda i, j: (0, j)),
            pl.BlockSpec((d, tf), lambda i, j: (0, j)),
            pl.BlockSpec((tf, d), lambda i, j: (j, 0)),
        ],
        out_specs=pl.BlockSpec((tm, d), lambda i, j: (i, 0)),
        out_shape=jax.ShapeDtypeStruct((m, d), F32),
        scratch_shapes=[pltpu.VMEM((tm, d), BF16), pltpu.VMEM((tm, d), F32)],
        compiler_params=_cparams("arbitrary", "arbitrary"),
        name="dense_ffn",
    )(x, mods, mods, mods, g2, w1, w3, w2)


def _router_kernel(x_ref, sh_ref, sc_ref, g2_ref, rw_ref, rb_ref, h_ref, info_ref, cnt_ref, carry, *, row):
    i = pl.program_id(0)
    tm = x_ref.shape[0]

    @pl.when(i == 0)
    def _():
        carry[...] = jnp.zeros_like(carry)

    h = _norm_modulate(x_ref[...], g2_ref[...], sh_ref[row:row + 1, :], sc_ref[row:row + 1, :])
    h_ref[...] = h
    logits = jnp.dot(h, rw_ref[...], preferred_element_type=F32,
                     precision=lax.Precision.HIGHEST) + rb_ref[...]
    lane = lax.broadcasted_iota(jnp.int32, (tm, LANES), 1).astype(F32)
    logits = jnp.where(lane < N_EXPERTS, logits, NEG_INF)
    v1 = jnp.max(logits, axis=-1, keepdims=True)
    e1 = jnp.min(jnp.where(logits == v1, lane, float(LANES)), axis=-1, keepdims=True)
    rest = jnp.where(lane == e1, NEG_INF, logits)
    v2 = jnp.max(rest, axis=-1, keepdims=True)
    e2 = jnp.min(jnp.where(rest == v2, lane, float(LANES)), axis=-1, keepdims=True)
    t = jnp.exp(v2 - v1)
    w1 = 1.0 / (1.0 + t)
    w2 = t / (1.0 + t)
    oh1 = jnp.where(lane == e1, 1.0, 0.0)
    oh2 = jnp.where(lane == e2, 1.0, 0.0)
    oh = oh1 + oh2
    r_i = lax.broadcasted_iota(jnp.int32, (tm, tm), 0)
    c_i = lax.broadcasted_iota(jnp.int32, (tm, tm), 1)
    tri = jnp.where(c_i < r_i, 1.0, 0.0).astype(BF16)
    before = _dot(tri, oh.astype(BF16)) + carry[0:1, :]
    rank1 = jnp.sum(before * oh1, axis=-1, keepdims=True)
    rank2 = jnp.sum(before * oh2, axis=-1, keepdims=True)
    carry[...] = carry[...] + jnp.sum(oh, axis=0, keepdims=True)
    info = jnp.where(lane == 0.0, e1, 0.0)
    info = jnp.where(lane == 1.0, e2, info)
    info = jnp.where(lane == 2.0, rank1, info)
    info = jnp.where(lane == 3.0, rank2, info)
    info = jnp.where(lane == 4.0, w1, info)
    info = jnp.where(lane == 5.0, w2, info)
    info_ref[...] = info
    cnt_ref[...] = carry[...]


def _router(x, mods, layer, row, g2, rw_pad, rb_pad, tm):
    m, d = x.shape
    mod = lambda k: pl.BlockSpec((None, 8, d), lambda i: (layer, 0, k))
    return pl.pallas_call(
        functools.partial(_router_kernel, row=row),
        grid=(m // tm,),
        in_specs=[
            pl.BlockSpec((tm, d), lambda i: (i, 0)),
            mod(3), mod(4),
            pl.BlockSpec((1, d), lambda i: (0, 0)),
            pl.BlockSpec((d, LANES), lambda i: (0, 0)),
            pl.BlockSpec((1, LANES), lambda i: (0, 0)),
        ],
        out_specs=[
            pl.BlockSpec((tm, d), lambda i: (i, 0)),
            pl.BlockSpec((tm, LANES), lambda i: (i, 0)),
            pl.BlockSpec((8, LANES), lambda i: (0, 0)),
        ],
        out_shape=[
            jax.ShapeDtypeStruct((m, d), F32),
            jax.ShapeDtypeStruct((m, LANES), F32),
            jax.ShapeDtypeStruct((8, LANES), F32),
        ],
        scratch_shapes=[pltpu.VMEM((8, LANES), F32)],
        compiler_params=_cparams("arbitrary"),
        name="moe_router",
    )(x, mods, mods, g2, rw_pad, rb_pad)


GATHER_UNROLL = 8


def _gather_rows(idx_ref, src_hbm, dst, sem, n):
    def body(kb, carry):
        for u in range(GATHER_UNROLL):
            k = kb * GATHER_UNROLL + u
            pltpu.make_async_copy(src_hbm.at[pl.ds(idx_ref[0, 0, k], 1)], dst.at[pl.ds(k, 1)], sem).start()
        return carry
    lax.fori_loop(0, n // GATHER_UNROLL, body, 0)


def _wait_rows(src_hbm, dst, sem, n):
    pltpu.make_async_copy(src_hbm.at[pl.ds(0, n)], dst, sem).wait()


def _moe_ffn_kernel(te_ref, nu_ref, cur_ref, nxt_ref, h_hbm, w1_ref, w3_ref, w2_ref, o_ref,
                    buf, sems, h_scr, acc, *, tm):
    r = pl.program_id(0)
    j = pl.program_id(1)
    n_used = nu_ref[0]
    slot = r % 2

    @pl.when(r < n_used)
    def _():
        @pl.when(j == 0)
        def _():
            @pl.when(r == 0)
            def _():
                _gather_rows(cur_ref, h_hbm, buf.at[0], sems.at[0], tm)

            _wait_rows(h_hbm, buf.at[slot], sems.at[slot], tm)

            @pl.when(r + 1 < n_used)
            def _():
                _gather_rows(nxt_ref, h_hbm, buf.at[1 - slot], sems.at[1 - slot], tm)

            h_scr[...] = buf[slot].astype(BF16)
            acc[...] = jnp.zeros_like(acc)

        h = h_scr[...]
        u = (_silu(_dot(h, w1_ref[...])) * _dot(h, w3_ref[...])).astype(BF16)
        acc[...] += _dot(u, w2_ref[...])

        @pl.when(j == pl.num_programs(1) - 1)
        def _():
            o_ref[...] = acc[...]

    @pl.when((r >= n_used) & (j == 0))
    def _():
        o_ref[...] = jnp.zeros_like(o_ref)


def _moe_ffn(h2, tile_expert, n_used, src_rows, w1, w3, w2, tm, tf):
    d = h2.shape[1]
    f = w1.shape[2]
    n_tiles = src_rows.shape[0]
    clamp = lambda r, nu: jnp.minimum(r, nu[0] - 1)
    grid_spec = pltpu.PrefetchScalarGridSpec(
        num_scalar_prefetch=2,
        grid=(n_tiles, f // tf),
        in_specs=[
            pl.BlockSpec((1, 1, tm), lambda r, j, te, nu: (clamp(r, nu), 0, 0), memory_space=pltpu.SMEM),
            pl.BlockSpec((1, 1, tm), lambda r, j, te, nu: (clamp(r + 1, nu), 0, 0), memory_space=pltpu.SMEM),
            pl.BlockSpec(memory_space=pl.ANY),
            pl.BlockSpec((None, d, tf), lambda r, j, te, nu: (te[clamp(r, nu)], 0, jnp.where(r < nu[0], j, f // tf - 1))),
            pl.BlockSpec((None, d, tf), lambda r, j, te, nu: (te[clamp(r, nu)], 0, jnp.where(r < nu[0], j, f // tf - 1))),
            pl.BlockSpec((None, tf, d), lambda r, j, te, nu: (te[clamp(r, nu)], jnp.where(r < nu[0], j, f // tf - 1), 0)),
        ],
        out_specs=pl.BlockSpec((tm, d), lambda r, j, te, nu: (r, 0)),
        scratch_shapes=[
            pltpu.VMEM((2, tm, d), F32),
            pltpu.SemaphoreType.DMA((2,)),
            pltpu.VMEM((tm, d), BF16),
            pltpu.VMEM((tm, d), F32),
        ],
    )
    return pl.pallas_call(
        functools.partial(_moe_ffn_kernel, tm=tm),
        grid_spec=grid_spec,
        out_shape=jax.ShapeDtypeStruct((n_tiles * tm, d), F32),
        compiler_params=_cparams("arbitrary", "arbitrary"),
        name="moe_ffn",
    )(tile_expert, n_used, src_rows, src_rows, h2, w1, w3, w2)


def _combine_kernel(cur_ref, nxt_ref, x_ref, info_ref, gate_ref, y_hbm, o_ref, buf, sems, *, tm, row):
    i = pl.program_id(0)
    slot = i % 2

    @pl.when(i == 0)
    def _():
        _gather_rows(cur_ref, y_hbm, buf.at[0], sems.at[0], 2 * tm)

    _wait_rows(y_hbm, buf.at[slot], sems.at[slot], 2 * tm)

    @pl.when(i + 1 < pl.num_programs(0))
    def _():
        _gather_rows(nxt_ref, y_hbm, buf.at[1 - slot], sems.at[1 - slot], 2 * tm)

    info = info_ref[...]
    w1 = info[:, 4:5]
    w2 = info[:, 5:6]
    y = w1 * buf[slot, 0:tm, :] + w2 * buf[slot, tm:2 * tm, :]
    o_ref[...] = x_ref[...] + gate_ref[row:row + 1, :] * y


def _moe_combine(x, info, mods, layer, row, y_rows, pos_tiles, tm):
    m, d = x.shape
    nt = m // tm
    return pl.pallas_call(
        functools.partial(_combine_kernel, tm=tm, row=row),
        grid=(nt,),
        in_specs=[
            pl.BlockSpec((1, 1, 2 * tm), lambda i: (i, 0, 0), memory_space=pltpu.SMEM),
            pl.BlockSpec((1, 1, 2 * tm), lambda i: (jnp.minimum(i + 1, nt - 1), 0, 0), memory_space=pltpu.SMEM),
            pl.BlockSpec((tm, d), lambda i: (i, 0)),
            pl.BlockSpec((tm, LANES), lambda i: (i, 0)),
            pl.BlockSpec((None, 8, d), lambda i: (layer, 0, 5)),
            pl.BlockSpec(memory_space=pl.ANY),
        ],
        out_specs=pl.BlockSpec((tm, d), lambda i: (i, 0)),
        out_shape=jax.ShapeDtypeStruct((m, d), F32),
        scratch_shapes=[pltpu.VMEM((2, 2 * tm, d), F32), pltpu.SemaphoreType.DMA((2,))],
        compiler_params=_cparams("arbitrary"),
        name="moe_combine",
    )(pos_tiles, pos_tiles, x, info, mods, y_rows)


def _moe_layer(x, mods, layer, row, g2, router_w, router_b, w1, w3, w2, tm_route, tm_exp, tf):
    m, d = x.shape
    rw_pad = jnp.pad(router_w, ((0, 0), (0, LANES - N_EXPERTS)))
    rb_pad = jnp.pad(router_b, (0, LANES - N_EXPERTS)).reshape(1, LANES)
    h2, info, cnt = _router(x, mods, layer, row, g2, rw_pad, rb_pad, tm_route)
    counts = cnt[0, :N_EXPERTS].astype(jnp.int32)
    tiles_per = (counts + tm_exp - 1) // tm_exp
    tile_end = jnp.cumsum(tiles_per)
    group_start = (tile_end - tiles_per) * tm_exp
    n_tiles = (2 * m) // tm_exp + N_EXPERTS
    n_used = tile_end[-1:].astype(jnp.int32)
    tile_ids = jnp.arange(n_tiles, dtype=jnp.int32)
    tile_expert = jnp.minimum(jnp.sum(tile_end[None, :] <= tile_ids[:, None], axis=1),
                              N_EXPERTS - 1).astype(jnp.int32)
    e12 = info[:, 0:2].astype(jnp.int32)
    rank12 = info[:, 2:4].astype(jnp.int32)
    pos = group_start[e12] + rank12
    tok = jnp.broadcast_to(jnp.arange(m, dtype=jnp.int32)[:, None], (m, 2))
    src_rows = jnp.zeros((n_tiles * tm_exp,), jnp.int32).at[pos.reshape(-1)].set(tok.reshape(-1))
    y_rows = _moe_ffn(h2, tile_expert, n_used, src_rows.reshape(n_tiles, 1, tm_exp), w1, w3, w2, tm_exp, tf)
    nt = m // tm_route
    pos_tiles = pos.reshape(nt, tm_route, 2).transpose(0, 2, 1).reshape(nt, 1, 2 * tm_route)
    return _moe_combine(x, info, mods, layer, row, y_rows, pos_tiles, tm_route)


def _rope_tables(n_tokens):
    n_rows = n_tokens // GRID_W
    inv = ROPE_THETA ** (-jnp.arange(ROPE_PAIRS, dtype=F32) / ROPE_PAIRS)
    ang_r = jnp.arange(n_rows).astype(F32)[:, None] * inv
    ang_c = jnp.arange(GRID_W).astype(F32)[:, None] * inv
    cos_r = jnp.repeat(jnp.cos(ang_r), GRID_W, axis=0)
    sin_r = jnp.repeat(jnp.sin(ang_r), GRID_W, axis=0)
    cos_c = jnp.tile(jnp.cos(ang_c), (n_rows, 1))
    sin_c = jnp.tile(jnp.sin(ang_c), (n_rows, 1))
    zero = jnp.zeros_like(sin_r)
    cos_t = jnp.concatenate([cos_r, cos_r, cos_c, cos_c], axis=-1)
    sa_t = jnp.concatenate([-sin_r, zero, -sin_c, zero], axis=-1)
    sb_t = jnp.concatenate([zero, sin_r, zero, sin_c], axis=-1)
    return cos_t, sa_t, sb_t


def kernel(x, c, ctx, c_ctx, ada_w, ada_b, norm1_g, norm2_g, w_in, q_norm_g, k_norm_g, attn_sink,
           ret_decay, ret_gn_g, w_out, ffn_w1, ffn_w3, ffn_w2, router_w, router_b, moe_w1, moe_w3, moe_w2):
    b, s, d = x.shape
    n_ctx = ctx.shape[1]
    depth = ada_w.shape[0]
    assert b == 1, "one latent sequence per call"
    x_lat = x[0]
    x_ctx = ctx[0]

    tm_lat = min(1024, s)
    tm_ffn = min(512, s)
    tq = min(512, s)
    tf = 512

    rows = jnp.concatenate([c[0:1], c_ctx[None, :], jnp.zeros((6, d), F32)], axis=0)
    mods = _ada_mods(rows, ada_w, ada_b)
    cos_l, sa_l, sb_l = _rope_tables(s)
    cos_c = jnp.ones((n_ctx, HEAD_DIM), F32)
    zer_c = jnp.zeros((n_ctx, HEAD_DIM), F32)
    log_g_all = jax.nn.log_sigmoid(ret_decay.astype(F32))

    for i in range(depth):
        last = i == depth - 1
        w_in_i = w_in[i].astype(BF16)
        w_out_i = w_out[i].astype(BF16)
        g1 = norm1_g[i].reshape(1, d)
        g2 = norm2_g[i].reshape(1, d)
        qg = q_norm_g[i].reshape(1, HEAD_DIM)
        kg = k_norm_g[i].reshape(1, HEAD_DIM)
        log_g = log_g_all[i]

        p_lat = _in_projection(x_lat, mods, i, 0, g1, w_in_i, cos_l, sa_l, sb_l, qg, kg, tm_lat)
        p_ctx = _in_projection(x_ctx, mods, i, 1, g1, w_in_i, cos_c, zer_c, zer_c, qg, kg, n_ctx)

        att_c, ret_c, s_f, s_b = _ctx_mixer(p_ctx, attn_sink[i], log_g, ret_gn_g[i])
        att_l = _window_attention(p_lat, p_ctx, attn_sink[i], tq)
        ret_l = _retention(p_lat, log_g, s_f, s_b, ret_gn_g[i])
        x_lat = _out_projection(x_lat, att_l, ret_l, w_out_i, mods, i, 0, tm_ffn)

        j = i // 2
        if i % 2 == 0:
            w1 = ffn_w1[j].astype(BF16)
            w3 = ffn_w3[j].astype(BF16)
            w2 = ffn_w2[j].astype(BF16)
            x_lat = _dense_ffn(x_lat, mods, i, 0, g2, w1, w3, w2, tm_ffn, tf)
        else:
            x_lat = _moe_layer(x_lat, mods, i, 0, g2, router_w[j], router_b[j],
                               moe_w1[j].astype(BF16), moe_w3[j].astype(BF16), moe_w2[j].astype(BF16),
                               tm_ffn, tm_ffn, tf)

        if not last:
            x_ctx = _out_projection(x_ctx, att_c, ret_c, w_out_i, mods, i, 1, n_ctx)
            if i % 2 == 0:
                x_ctx = _dense_ffn(x_ctx, mods, i, 1, g2, w1, w3, w2, n_ctx, tf)
            else:
                x_ctx = _moe_layer(x_ctx, mods, i, 1, g2, router_w[j], router_b[j],
                                   moe_w1[j].astype(BF16), moe_w3[j].astype(BF16), moe_w2[j].astype(BF16),
                                   n_ctx, n_ctx, tf)
    return x_lat[None]
```

```python
import functools

import jax
import jax.numpy as jnp
from jax import lax
from jax.experimental import pallas as pl
from jax.experimental.pallas import tpu as pltpu

F32 = jnp.float32
BF16 = jnp.bfloat16

HEAD_DIM = 128
N_ATT_HEADS = 8
N_KV_HEADS = 2
N_RET_HEADS = 8
GQA_GROUP = N_ATT_HEADS // N_KV_HEADS
ATT_W = N_ATT_HEADS * HEAD_DIM
KV_W = N_KV_HEADS * HEAD_DIM
RET_W = N_RET_HEADS * HEAD_DIM
IN_COLS = ATT_W + 2 * KV_W + 4 * RET_W
WINDOW = 128
RET_CHUNK = 128
GRID_W = 64
ROPE_THETA = 10000.0
ROPE_PAIRS = HEAD_DIM // 4
N_EXPERTS = 8
EPS = 1e-6
ATT_SCALE = HEAD_DIM ** -0.5

COL_Q = 0
COL_K = ATT_W // HEAD_DIM
COL_V = COL_K + N_KV_HEADS
COL_RQ = COL_V + N_KV_HEADS
COL_RK = COL_RQ + N_RET_HEADS
COL_RV = COL_RK + N_RET_HEADS
COL_G = COL_RV + N_RET_HEADS

VMEM_LIMIT_BYTES = 56 * 1024 * 1024
LANES = 128

NEG_INF = float("-inf")


def _cparams(*sem):
    return pltpu.CompilerParams(dimension_semantics=sem, vmem_limit_bytes=VMEM_LIMIT_BYTES)


def _silu(x):
    return x * (1.0 / (1.0 + jnp.exp(-x)))


def _dot(a, b):
    return jnp.dot(a, b, preferred_element_type=F32)


def _dot_nt(a, b):
    return lax.dot_general(a, b, (((1,), (1,)), ((), ())), preferred_element_type=F32)


def _dot_tn(a, b):
    return lax.dot_general(a, b, (((0,), (0,)), ((), ())), preferred_element_type=F32)


def _norm_modulate(x, g, shift, scale):
    ms = jnp.mean(x * x, axis=-1, keepdims=True)
    return (x * lax.rsqrt(ms + EPS) * g) * (1.0 + scale) + shift


def _ada_kernel(r_ref, w_ref, b_ref, o_ref):
    s = _silu(r_ref[...])
    o_ref[...] = jnp.dot(s, w_ref[...], preferred_element_type=F32,
                         precision=lax.Precision.HIGHEST) + b_ref[...]


def _ada_mods(rows, ada_w, ada_b):
    depth, d, n = ada_w.shape
    tn = d // 2
    return pl.pallas_call(
        _ada_kernel,
        grid=(depth, n // tn),
        in_specs=[
            pl.BlockSpec((8, d), lambda l, j: (0, 0)),
            pl.BlockSpec((None, d, tn), lambda l, j: (l, 0, j)),
            pl.BlockSpec((None, 1, tn), lambda l, j: (l, 0, j)),
        ],
        out_specs=pl.BlockSpec((None, 8, tn), lambda l, j: (l, 0, j)),
        out_shape=jax.ShapeDtypeStruct((depth, 8, n), F32),
        compiler_params=_cparams("arbitrary", "arbitrary"),
        name="ada_mods",
    )(rows, ada_w, ada_b.reshape(depth, 1, n))


def _rope(y, cos, sa, sb):
    return y * cos + pltpu.roll(y, 96, 1) * sa + pltpu.roll(y, 32, 1) * sb


def _inproj_kernel(x_ref, sh_ref, sc_ref, g1_ref, w_ref, gain_ref, cos_ref, sa_ref, sb_ref, o_ref,
                   h_scr, p_scr, *, row, tn, n_j, n_steps):
    t = pl.program_id(0)
    j = t % n_j
    nchunk = tn // HEAD_DIM

    @pl.when(t == 0)
    def _():
        p_scr[...] = jnp.zeros_like(p_scr)

    @pl.when((j == 0) & (t < n_steps))
    def _():
        h = _norm_modulate(x_ref[...], g1_ref[...], sh_ref[row:row + 1, :], sc_ref[row:row + 1, :])
        h_scr[...] = h.astype(BF16)

    p_new = _dot(h_scr[...], w_ref[...])

    jp = (t + n_j - 1) % n_j
    tabs = None
    for c in range(nchunk):
        gc = jp * nchunk + c
        norm_on = gc < COL_V
        rope_on = norm_on | ((gc >= COL_RQ) & (gc < COL_RV))
        if c % N_KV_HEADS == 0:
            tabs = (jnp.where(rope_on, cos_ref[...], 1.0), jnp.where(rope_on, sa_ref[...], 0.0),
                    jnp.where(rope_on, sb_ref[...], 0.0))
        cols = slice(c * HEAD_DIM, (c + 1) * HEAD_DIM)
        pc = p_scr[:, cols]
        r = lax.rsqrt(jnp.mean(pc * pc, axis=-1, keepdims=True) + EPS)
        y = pc * jnp.where(norm_on, r, 1.0) * gain_ref[:, cols]
        o_ref[:, cols] = _rope(y, *tabs).astype(o_ref.dtype)
    p_scr[...] = p_new


def _in_projection(x, mods, layer, row, g1, w, gain, cos, sa, sb, tm):
    m, d = x.shape
    n = w.shape[1]
    tn = 2 * KV_W
    assert n == IN_COLS and ATT_W % tn == 0 and RET_W % tn == 0 and m % tm == 0
    n_i, n_j = m // tm, n // tn
    n_steps = n_i * n_j
    prev = lambda t: jnp.maximum(t - 1, 0)
    tab = pl.BlockSpec((tm, HEAD_DIM), lambda t: (prev(t) // n_j, 0))
    return pl.pallas_call(
        functools.partial(_inproj_kernel, row=row, tn=tn, n_j=n_j, n_steps=n_steps),
        grid=(n_steps + 1,),
        in_specs=[
            pl.BlockSpec((tm, d), lambda t: (jnp.minimum(t // n_j, n_i - 1), 0)),
            pl.BlockSpec((None, 8, d), lambda t: (layer, 0, 0)),
            pl.BlockSpec((None, 8, d), lambda t: (layer, 0, 1)),
            pl.BlockSpec((1, d), lambda t: (0, 0)),
            pl.BlockSpec((d, tn), lambda t: (0, t % n_j)),
            pl.BlockSpec((1, tn), lambda t: (0, prev(t) % n_j)),
            tab, tab, tab,
        ],
        out_specs=pl.BlockSpec((tm, tn), lambda t: (prev(t) // n_j, prev(t) % n_j)),
        out_shape=jax.ShapeDtypeStruct((m, n), BF16),
        scratch_shapes=[pltpu.VMEM((tm, d), BF16), pltpu.VMEM((tm, tn), F32)],
        compiler_params=_cparams("arbitrary"),
        name="in_projection",
    )(x, mods, mods, g1, w, gain, cos, sa, sb)


def _attn_kernel(sink_ref, q_ref, km_ref, vm_ref, kp_ref, vp_ref, kn_ref, vn_ref,
                 kc_ref, vc_ref, bias_ref, o_ref, kbuf, vbuf, *, tq):
    i = pl.program_id(0)
    last = pl.num_programs(0) - 1
    nsub = tq // WINDOW
    kbuf[0:WINDOW] = kp_ref[...]
    kbuf[WINDOW:WINDOW + tq] = km_ref[...]
    kbuf[WINDOW + tq:] = kn_ref[...]
    vbuf[0:WINDOW] = vp_ref[...]
    vbuf[WINDOW:WINDOW + tq] = vm_ref[...]
    vbuf[WINDOW + tq:] = vn_ref[...]
    col = lax.broadcasted_iota(jnp.int32, (WINDOW, 3 * WINDOW), 1)
    for s in range(nsub):
        bias = bias_ref[...]
        if s == 0:
            bias = bias + jnp.where(col < WINDOW, jnp.where(i == 0, NEG_INF, 0.0), 0.0)
        if s == nsub - 1:
            bias = bias + jnp.where(col >= 2 * WINDOW, jnp.where(i == last, NEG_INF, 0.0), 0.0)
        bias4 = jnp.concatenate([bias] * GQA_GROUP, axis=0)
        for h in range(N_KV_HEADS):
            hs = slice(h * HEAD_DIM, (h + 1) * HEAD_DIM)
            q = jnp.concatenate(
                [q_ref[s * WINDOW:(s + 1) * WINDOW,
                       (h * GQA_GROUP + g) * HEAD_DIM:(h * GQA_GROUP + g + 1) * HEAD_DIM]
                 for g in range(GQA_GROUP)], axis=0)
            kl = kbuf[s * WINDOW:(s + 3) * WINDOW, hs]
            vl = vbuf[s * WINDOW:(s + 3) * WINDOW, hs]
            s_loc = _dot_nt(q, kl) + bias4
            s_ctx = _dot_nt(q, kc_ref[:, hs])
            sink = jnp.concatenate(
                [jnp.full((WINDOW, 1), sink_ref[h * GQA_GROUP + g], F32) for g in range(GQA_GROUP)],
                axis=0)
            m = jnp.maximum(jnp.maximum(jnp.max(s_loc, axis=-1, keepdims=True),
                                        jnp.max(s_ctx, axis=-1, keepdims=True)), sink)
            p_loc = jnp.exp(s_loc - m)
            p_ctx = jnp.exp(s_ctx - m)
            den = (jnp.sum(p_loc, axis=-1, keepdims=True) + jnp.sum(p_ctx, axis=-1, keepdims=True)
                   + jnp.exp(sink - m))
            o = (_dot(p_loc.astype(BF16), vl) + _dot(p_ctx.astype(BF16), vc_ref[:, hs])) / den
            for g in range(GQA_GROUP):
                o_ref[s * WINDOW:(s + 1) * WINDOW,
                      (h * GQA_GROUP + g) * HEAD_DIM:(h * GQA_GROUP + g + 1) * HEAD_DIM] = (
                    o[g * WINDOW:(g + 1) * WINDOW].astype(o_ref.dtype))


def _band_bias():
    r = jnp.arange(WINDOW)[:, None]
    c = jnp.arange(3 * WINDOW)[None, :]
    ok = (c - r >= 0) & (c - r <= 2 * WINDOW)
    return jnp.where(ok, 0.0, NEG_INF).astype(F32)


def _window_attention(p_lat, p_ctx, sink, tq):
    s = p_lat.shape[0]
    c = p_ctx.shape[0]
    r = tq // WINDOW
    nb = s // WINDOW
    kvb = KV_W // HEAD_DIM
    ck, cv = COL_K // kvb, COL_V // kvb
    return pl.pallas_call(
        functools.partial(_attn_kernel, tq=tq),
        grid=(s // tq,),
        in_specs=[
            pl.BlockSpec(memory_space=pltpu.SMEM),
            pl.BlockSpec((tq, ATT_W), lambda i: (i, 0)),
            pl.BlockSpec((tq, KV_W), lambda i: (i, ck)),
            pl.BlockSpec((tq, KV_W), lambda i: (i, cv)),
            pl.BlockSpec((WINDOW, KV_W), lambda i: (jnp.maximum(i * r - 1, 0), ck)),
            pl.BlockSpec((WINDOW, KV_W), lambda i: (jnp.maximum(i * r - 1, 0), cv)),
            pl.BlockSpec((WINDOW, KV_W), lambda i: (jnp.minimum((i + 1) * r, nb - 1), ck)),
            pl.BlockSpec((WINDOW, KV_W), lambda i: (jnp.minimum((i + 1) * r, nb - 1), cv)),
            pl.BlockSpec((c, KV_W), lambda i: (0, ck)),
            pl.BlockSpec((c, KV_W), lambda i: (0, cv)),
            pl.BlockSpec((WINDOW, 3 * WINDOW), lambda i: (0, 0)),
        ],
        out_specs=pl.BlockSpec((tq, ATT_W), lambda i: (i, 0)),
        out_shape=jax.ShapeDtypeStruct((s, ATT_W), BF16),
        scratch_shapes=[pltpu.VMEM((tq + 2 * WINDOW, KV_W), BF16),
                        pltpu.VMEM((tq + 2 * WINDOW, KV_W), BF16)],
        compiler_params=_cparams("arbitrary"),
        name="window_attention",
    )(sink, p_lat, p_lat, p_lat, p_lat, p_lat, p_lat, p_lat, p_ctx, p_ctx, _band_bias())


def _ctx_mixer_kernel(sink_ref, lg_ref, q_ref, k_ref, v_ref, rq_ref, rk_ref, rv_ref, g_ref, gn_ref,
                      att_ref, ret_ref, sf_ref, sb_ref):
    h = pl.program_id(0)
    c = q_ref.shape[0]
    lg_f = lg_ref[0, h]
    lg_b = lg_ref[1, h]
    s = _dot_nt(q_ref[...], k_ref[...])
    sink = jnp.full((c, 1), sink_ref[h], F32)
    m = jnp.maximum(jnp.max(s, axis=-1, keepdims=True), sink)
    p = jnp.exp(s - m)
    den = jnp.sum(p, axis=-1, keepdims=True) + jnp.exp(sink - m)
    att_ref[...] = (_dot(p.astype(BF16), v_ref[...]) / den).astype(att_ref.dtype)
    n_i = lax.broadcasted_iota(jnp.int32, (c, c), 0)
    m_i = lax.broadcasted_iota(jnp.int32, (c, c), 1)
    rel = (n_i - m_i).astype(F32)
    dec = (jnp.where(rel >= 0, jnp.exp(jnp.maximum(rel, 0.0) * lg_f), 0.0)
           + jnp.where(rel <= 0, jnp.exp(jnp.maximum(-rel, 0.0) * lg_b), 0.0))
    sc = _dot_nt(rq_ref[...], rk_ref[...]) * dec
    y = _dot(sc.astype(BF16), rv_ref[...])
    mu = jnp.mean(y, axis=-1, keepdims=True)
    var = jnp.mean(jnp.square(y - mu), axis=-1, keepdims=True)
    yn = (y - mu) * lax.rsqrt(var + EPS) * gn_ref[...]
    ret_ref[...] = (_silu(g_ref[...].astype(F32)) * yn).astype(ret_ref.dtype)
    pos = lax.broadcasted_iota(jnp.int32, (c, HEAD_DIM), 0).astype(F32)
    kf = rk_ref[...].astype(F32)
    k_f = (kf * jnp.exp((c - 1.0 - pos) * lg_f)).astype(BF16)
    k_b = (kf * jnp.exp(pos * lg_b)).astype(BF16)
    sf_ref[...] = _dot_tn(k_f, rv_ref[...])
    sb_ref[...] = _dot_tn(k_b, rv_ref[...])


def _ctx_mixer(p_ctx, sink, log_g, gn_g):
    c = p_ctx.shape[0]
    hb = lambda off: pl.BlockSpec((c, HEAD_DIM), lambda h: (0, off + h))
    return pl.pallas_call(
        _ctx_mixer_kernel,
        grid=(N_RET_HEADS,),
        in_specs=[
            pl.BlockSpec(memory_space=pltpu.SMEM),
            pl.BlockSpec(memory_space=pltpu.SMEM),
            hb(COL_Q),
            pl.BlockSpec((c, HEAD_DIM), lambda h: (0, COL_K + h // GQA_GROUP)),
            pl.BlockSpec((c, HEAD_DIM), lambda h: (0, COL_V + h // GQA_GROUP)),
            hb(COL_RQ), hb(COL_RK), hb(COL_RV), hb(COL_G),
            pl.BlockSpec((None, 1, HEAD_DIM), lambda h: (h, 0, 0)),
        ],
        out_specs=[
            pl.BlockSpec((c, HEAD_DIM), lambda h: (0, h)),
            pl.BlockSpec((c, HEAD_DIM), lambda h: (0, h)),
            pl.BlockSpec((None, HEAD_DIM, HEAD_DIM), lambda h: (h, 0, 0)),
            pl.BlockSpec((None, HEAD_DIM, HEAD_DIM), lambda h: (h, 0, 0)),
        ],
        out_shape=[
            jax.ShapeDtypeStruct((c, ATT_W), BF16),
            jax.ShapeDtypeStruct((c, RET_W), BF16),
            jax.ShapeDtypeStruct((N_RET_HEADS, HEAD_DIM, HEAD_DIM), F32),
            jax.ShapeDtypeStruct((N_RET_HEADS, HEAD_DIM, HEAD_DIM), F32),
        ],
        compiler_params=_cparams("arbitrary"),
        name="ctx_mixer",
    )(sink, log_g, p_ctx, p_ctx, p_ctx, p_ctx, p_ctx, p_ctx, p_ctx,
      gn_g.reshape(N_RET_HEADS, 1, HEAD_DIM))


RET_KERNEL_CHUNK = 256
RET_UNROLL = 4


def _ret_kernel(lg_ref, q_ref, k_ref, v_ref, g_ref, sf_ref, sb_ref, gn_ref, o_ref,
                y_scr, dmat, qdf, kdf, qdb, kdb, cdf, cdb):
    h = pl.program_id(0)
    ck = RET_KERNEL_CHUNK
    nit = q_ref.shape[0] // (ck * RET_UNROLL)
    lg_f = lg_ref[0, h]
    lg_b = lg_ref[1, h]
    ii = lax.broadcasted_iota(jnp.int32, (ck, ck), 0)
    jj = lax.broadcasted_iota(jnp.int32, (ck, ck), 1)
    rel = (ii - jj).astype(F32)
    dmat[...] = (jnp.where(rel >= 0, jnp.exp(jnp.maximum(rel, 0.0) * lg_f), 0.0)
                 + jnp.where(rel <= 0, jnp.exp(jnp.maximum(-rel, 0.0) * lg_b), 0.0))
    pos = lax.broadcasted_iota(jnp.int32, (ck, HEAD_DIM), 0).astype(F32)
    qdf[...] = jnp.exp((pos + 1.0) * lg_f)
    kdf[...] = jnp.exp((ck - 1.0 - pos) * lg_f)
    qdb[...] = jnp.exp((ck - pos) * lg_b)
    kdb[...] = jnp.exp(pos * lg_b)
    full = jnp.full((HEAD_DIM, HEAD_DIM), float(ck), F32)
    cdf[...] = jnp.exp(full * lg_f)
    cdb[...] = jnp.exp(full * lg_b)

    def rows_of(z):
        return pl.ds(pl.multiple_of(z * ck, ck), ck)

    def fwd(it, state):
        zs = [it * RET_UNROLL + u for u in range(RET_UNROLL)]
        qs = [q_ref[rows_of(z), :] for z in zs]
        ks = [k_ref[rows_of(z), :] for z in zs]
        vs = [v_ref[rows_of(z), :] for z in zs]
        inner = [_dot((_dot_nt(q, k) * dmat[...]).astype(BF16), v) for q, k, v in zip(qs, ks, vs)]
        kvs = [_dot_tn((k.astype(F32) * kdf[...]).astype(BF16), v) for k, v in zip(ks, vs)]
        for u, z in enumerate(zs):
            y_scr[rows_of(z), :] = inner[u] + _dot(qs[u], state.astype(BF16)) * qdf[...]
            state = cdf[...] * state + kvs[u]
        return state

    lax.fori_loop(0, nit, fwd, sf_ref[...])

    def bwd(it, state):
        zs = [nit * RET_UNROLL - 1 - (it * RET_UNROLL + u) for u in range(RET_UNROLL)]
        qs = [q_ref[rows_of(z), :] for z in zs]
        kvs = [_dot_tn((k_ref[rows_of(z), :].astype(F32) * kdb[...]).astype(BF16), v_ref[rows_of(z), :])
               for z in zs]
        for u, z in enumerate(zs):
            y = y_scr[rows_of(z), :] + _dot(qs[u], state.astype(BF16)) * qdb[...]
            state = cdb[...] * state + kvs[u]
            mu = jnp.mean(y, axis=-1, keepdims=True)
            var = jnp.mean(jnp.square(y - mu), axis=-1, keepdims=True)
            yn = (y - mu) * lax.rsqrt(var + EPS) * gn_ref[...]
            o_ref[rows_of(z), :] = (_silu(g_ref[rows_of(z), :].astype(F32)) * yn).astype(o_ref.dtype)
        return state

    lax.fori_loop(0, nit, bwd, sb_ref[...])


def _retention(p_lat, log_g, s_f, s_b, gn_g):
    s = p_lat.shape[0]
    ck = RET_KERNEL_CHUNK
    assert s % (ck * RET_UNROLL) == 0
    col = lambda off: pl.BlockSpec((s, HEAD_DIM), lambda h: (0, off + h))
    state_spec = pl.BlockSpec((None, HEAD_DIM, HEAD_DIM), lambda h: (h, 0, 0))
    vec = pltpu.VMEM((ck, HEAD_DIM), F32)
    sq = pltpu.VMEM((HEAD_DIM, HEAD_DIM), F32)
    return pl.pallas_call(
        _ret_kernel,
        grid=(N_RET_HEADS,),
        in_specs=[pl.BlockSpec(memory_space=pltpu.SMEM),
                  col(COL_RQ), col(COL_RK), col(COL_RV), col(COL_G), state_spec, state_spec,
                  pl.BlockSpec((None, 1, HEAD_DIM), lambda h: (h, 0, 0))],
        out_specs=col(0),
        out_shape=jax.ShapeDtypeStruct((s, RET_W), BF16),
        scratch_shapes=[pltpu.VMEM((s, HEAD_DIM), F32), pltpu.VMEM((ck, ck), F32),
                        vec, vec, vec, vec, sq, sq],
        compiler_params=_cparams("arbitrary"),
        name="retention",
    )(log_g, p_lat, p_lat, p_lat, p_lat, s_f, s_b, gn_g.reshape(N_RET_HEADS, 1, HEAD_DIM))


def _outproj_kernel(x_ref, a_ref, r_ref, wa_ref, wr_ref, gate_ref, o_ref, *, row):
    y = _dot(a_ref[...], wa_ref[...]) + _dot(r_ref[...], wr_ref[...])
    o_ref[...] = x_ref[...] + gate_ref[row:row + 1, :] * y


def _out_projection(x, att, ret, w_out, mods, layer, row, tm):
    m, d = x.shape
    return pl.pallas_call(
        functools.partial(_outproj_kernel, row=row),
        grid=(m // tm,),
        in_specs=[
            pl.BlockSpec((tm, d), lambda i: (i, 0)),
            pl.BlockSpec((tm, ATT_W), lambda i: (i, 0)),
            pl.BlockSpec((tm, RET_W), lambda i: (i, 0)),
            pl.BlockSpec((ATT_W, d), lambda i: (0, 0)),
            pl.BlockSpec((RET_W, d), lambda i: (ATT_W // RET_W, 0)),
            pl.BlockSpec((None, 8, d), lambda i: (layer, 0, 2)),
        ],
        out_specs=pl.BlockSpec((tm, d), lambda i: (i, 0)),
        out_shape=jax.ShapeDtypeStruct((m, d), F32),
        compiler_params=_cparams("arbitrary"),
        name="out_projection",
    )(x, att, ret, w_out, w_out, mods)


def _ffn_kernel(x_ref, sh_ref, sc_ref, gate_ref, g2_ref, w1_ref, w3_ref, w2_ref, o_ref,
                h_scr, acc, *, row):
    j = pl.program_id(1)

    @pl.when(j == 0)
    def _():
        h = _norm_modulate(x_ref[...], g2_ref[...], sh_ref[row:row + 1, :], sc_ref[row:row + 1, :])
        h_scr[...] = h.astype(BF16)
        acc[...] = jnp.zeros_like(acc)

    h = h_scr[...]
    u = (_silu(_dot(h, w1_ref[...])) * _dot(h, w3_ref[...])).astype(BF16)
    acc[...] += _dot(u, w2_ref[...])

    @pl.when(j == pl.num_programs(1) - 1)
    def _():
        o_ref[...] = x_ref[...] + gate_ref[row:row + 1, :] * acc[...]


def _dense_ffn(x, mods, layer, row, g2, w1, w3, w2, tm, tf):
    m, d = x.shape
    f = w1.shape[1]
    mod = lambda k: pl.BlockSpec((None, 8, d), lambda i, j: (layer, 0, k))
    return pl.pallas_call(
        functools.partial(_ffn_kernel, row=row),
        grid=(m // tm, f // tf),
        in_specs=[
            pl.BlockSpec((tm, d), lambda i, j: (i, 0)),
            mod(3), mod(4), mod(5),
            pl.BlockSpec((1, d), lambda i, j: (0, 0)),
            pl.BlockSpec((d, tf), lambda i, j: (0, j)),
            pl.BlockSpec((d, tf), lambda i, j: (0, j)),
            pl.BlockSpec((tf, d), lambda i, j: (j, 0)),
        ],
        out_specs=pl.BlockSpec((tm, d), lambda i, j: (i, 0)),
        out_shape=jax.ShapeDtypeStruct((m, d), F32),
        scratch_shapes=[pltpu.VMEM((tm, d), BF16), pltpu.VMEM((tm, d), F32)],
        compiler_params=_cparams("arbitrary", "arbitrary"),
        name="dense_ffn",
    )(x, mods, mods, mods, g2, w1, w3, w2)


def _router_kernel(x_ref, sh_ref, sc_ref, g2_ref, rw_ref, rb_ref, h_ref, info_ref, cnt_ref, carry, *, row):
    i = pl.program_id(0)
    tm = x_ref.shape[0]

    @pl.when(i == 0)
    def _():
        carry[...] = jnp.zeros_like(carry)

    h = _norm_modulate(x_ref[...], g2_ref[...], sh_ref[row:row + 1, :], sc_ref[row:row + 1, :])
    h_ref[...] = h
    logits = jnp.dot(h, rw_ref[...], preferred_element_type=F32,
                     precision=lax.Precision.HIGHEST) + rb_ref[...]
    lane = lax.broadcasted_iota(jnp.int32, (tm, LANES), 1).astype(F32)
    logits = jnp.where(lane < N_EXPERTS, logits, NEG_INF)
    v1 = jnp.max(logits, axis=-1, keepdims=True)
    e1 = jnp.min(jnp.where(logits == v1, lane, float(LANES)), axis=-1, keepdims=True)
    rest = jnp.where(lane == e1, NEG_INF, logits)
    v2 = jnp.max(rest, axis=-1, keepdims=True)
    e2 = jnp.min(jnp.where(rest == v2, lane, float(LANES)), axis=-1, keepdims=True)
    t = jnp.exp(v2 - v1)
    w1 = 1.0 / (1.0 + t)
    w2 = t / (1.0 + t)
    oh1 = jnp.where(lane == e1, 1.0, 0.0)
    oh2 = jnp.where(lane == e2, 1.0, 0.0)
    oh = oh1 + oh2
    r_i = lax.broadcasted_iota(jnp.int32, (tm, tm), 0)
    c_i = lax.broadcasted_iota(jnp.int32, (tm, tm), 1)
    tri = jnp.where(c_i < r_i, 1.0, 0.0).astype(BF16)
    before = _dot(tri, oh.astype(BF16)) + carry[0:1, :]
    rank1 = jnp.sum(before * oh1, axis=-1, keepdims=True)
    rank2 = jnp.sum(before * oh2, axis=-1, keepdims=True)
    carry[...] = carry[...] + jnp.sum(oh, axis=0, keepdims=True)
    info = jnp.where(lane == 0.0, e1, 0.0)
    info = jnp.where(lane == 1.0, e2, info)
    info = jnp.where(lane == 2.0, rank1, info)
    info = jnp.where(lane == 3.0, rank2, info)
    info = jnp.where(lane == 4.0, w1, info)
    info = jnp.where(lane == 5.0, w2, info)
    info_ref[...] = info
    cnt_ref[...] = carry[...]


def _router(x, mods, layer, row, g2, rw_pad, rb_pad, tm):
    m, d = x.shape
    mod = lambda k: pl.BlockSpec((None, 8, d), lambda i: (layer, 0, k))
    return pl.pallas_call(
        functools.partial(_router_kernel, row=row),
        grid=(m // tm,),
        in_specs=[
            pl.BlockSpec((tm, d), lambda i: (i, 0)),
            mod(3), mod(4),
            pl.BlockSpec((1, d), lambda i: (0, 0)),
            pl.BlockSpec((d, LANES), lambda i: (0, 0)),
            pl.BlockSpec((1, LANES), lambda i: (0, 0)),
        ],
        out_specs=[
            pl.BlockSpec((tm, d), lambda i: (i, 0)),
            pl.BlockSpec((tm, LANES), lambda i: (i, 0)),
            pl.BlockSpec((8, LANES), lambda i: (0, 0)),
        ],
        out_shape=[
            jax.ShapeDtypeStruct((m, d), F32),
            jax.ShapeDtypeStruct((m, LANES), F32),
            jax.ShapeDtypeStruct((8, LANES), F32),
        ],
        scratch_shapes=[pltpu.VMEM((8, LANES), F32)],
        compiler_params=_cparams("arbitrary"),
        name="moe_router",
    )(x, mods, mods, g2, rw_pad, rb_pad)


GATHER_UNROLL = 8


def _gather_rows(idx_ref, src_hbm, dst, sem, n):
    def body(kb, carry):
        for u in range(GATHER_UNROLL):
            k = kb * GATHER_UNROLL + u
            pltpu.make_async_copy(src_hbm.at[pl.ds(idx_ref[0, 0, k], 1)], dst.at[pl.ds(k, 1)], sem).start()
        return carry
    lax.fori_loop(0, n // GATHER_UNROLL, body, 0)


def _wait_rows(src_hbm, dst, sem, n):
    pltpu.make_async_copy(src_hbm.at[pl.ds(0, n)], dst, sem).wait()


def _moe_ffn_kernel(te_ref, nu_ref, nv_ref, cur_ref, nxt_ref, h_hbm, w1_ref, w3_ref, w2_ref, o_ref,
                    buf, sem, h_scr, *, tm):
    r = pl.program_id(0)
    j = pl.program_id(1)
    n_used = nu_ref[0]
    half = tm // 2

    @pl.when(r < n_used)
    def _():
        @pl.when(j == 0)
        def _():
            @pl.when(r == 0)
            def _():
                _gather_rows(cur_ref, h_hbm, buf, sem, tm)

            _wait_rows(h_hbm, buf, sem, tm)
            h_scr[...] = buf[...].astype(BF16)

            @pl.when(r + 1 < n_used)
            def _():
                _gather_rows(nxt_ref, h_hbm, buf, sem, tm)

            o_ref[...] = jnp.zeros_like(o_ref)

        w1 = w1_ref[...].astype(BF16)
        w3 = w3_ref[...].astype(BF16)
        w2 = w2_ref[...].astype(BF16)

        def ffn_rows(lo):
            h = h_scr[lo:lo + half, :]
            u = (_silu(_dot(h, w1)) * _dot(h, w3)).astype(BF16)
            o_ref[lo:lo + half, :] += _dot(u, w2)

        ffn_rows(0)

        @pl.when(nv_ref[r] > half)
        def _():
            ffn_rows(half)

    @pl.when((r >= n_used) & (j == 0))
    def _():
        o_ref[...] = jnp.zeros_like(o_ref)


def _moe_ffn(h2, tile_expert, n_used, n_valid, src_rows, w1, w3, w2, tm, tf):
    d = h2.shape[1]
    f = w1.shape[2]
    n_tiles = src_rows.shape[0]
    n_f = f // tf
    clamp = lambda r, nu: jnp.maximum(jnp.minimum(r, nu[0] - 1), 0)
    fcol = lambda r, j, nu: jnp.where(r < nu[0], j, n_f - 1)
    grid_spec = pltpu.PrefetchScalarGridSpec(
        num_scalar_prefetch=3,
        grid=(n_tiles, n_f),
        in_specs=[
            pl.BlockSpec((1, 1, tm), lambda r, j, te, nu, nv: (clamp(r, nu), 0, 0), memory_space=pltpu.SMEM),
            pl.BlockSpec((1, 1, tm), lambda r, j, te, nu, nv: (clamp(r + 1, nu), 0, 0), memory_space=pltpu.SMEM),
            pl.BlockSpec(memory_space=pl.ANY),
            pl.BlockSpec((None, d, tf), lambda r, j, te, nu, nv: (te[clamp(r, nu)], 0, fcol(r, j, nu))),
            pl.BlockSpec((None, d, tf), lambda r, j, te, nu, nv: (te[clamp(r, nu)], 0, fcol(r, j, nu))),
            pl.BlockSpec((None, tf, d), lambda r, j, te, nu, nv: (te[clamp(r, nu)], fcol(r, j, nu), 0)),
        ],
        out_specs=pl.BlockSpec((tm, d), lambda r, j, te, nu, nv: (r, 0)),
        scratch_shapes=[
            pltpu.VMEM((tm, d), F32),
            pltpu.SemaphoreType.DMA(()),
            pltpu.VMEM((tm, d), BF16),
        ],
    )
    return pl.pallas_call(
        functools.partial(_moe_ffn_kernel, tm=tm),
        grid_spec=grid_spec,
        out_shape=jax.ShapeDtypeStruct((n_tiles * tm, d), F32),
        compiler_params=_cparams("arbitrary", "arbitrary"),
        name="moe_ffn",
    )(tile_expert, n_used, n_valid, src_rows, src_rows, h2, w1, w3, w2)


def _combine_kernel(cur_ref, nxt_ref, x_ref, info_ref, gate_ref, y_hbm, o_ref, buf, sems, *, tm, row):
    i = pl.program_id(0)
    slot = i % 2

    @pl.when(i == 0)
    def _():
        _gather_rows(cur_ref, y_hbm, buf.at[0], sems.at[0], 2 * tm)

    _wait_rows(y_hbm, buf.at[slot], sems.at[slot], 2 * tm)

    @pl.when(i + 1 < pl.num_programs(0))
    def _():
        _gather_rows(nxt_ref, y_hbm, buf.at[1 - slot], sems.at[1 - slot], 2 * tm)

    info = info_ref[...]
    w1 = info[:, 4:5]
    w2 = info[:, 5:6]
    y = w1 * buf[slot, 0:tm, :] + w2 * buf[slot, tm:2 * tm, :]
    o_ref[...] = x_ref[...] + gate_ref[row:row + 1, :] * y


def _moe_combine(x, info, mods, layer, row, y_rows, pos_tiles, tm):
    m, d = x.shape
    nt = m // tm
    return pl.pallas_call(
        functools.partial(_combine_kernel, tm=tm, row=row),
        grid=(nt,),
        in_specs=[
            pl.BlockSpec((1, 1, 2 * tm), lambda i: (i, 0, 0), memory_space=pltpu.SMEM),
            pl.BlockSpec((1, 1, 2 * tm), lambda i: (jnp.minimum(i + 1, nt - 1), 0, 0), memory_space=pltpu.SMEM),
            pl.BlockSpec((tm, d), lambda i: (i, 0)),
            pl.BlockSpec((tm, LANES), lambda i: (i, 0)),
            pl.BlockSpec((None, 8, d), lambda i: (layer, 0, 5)),
            pl.BlockSpec(memory_space=pl.ANY),
        ],
        out_specs=pl.BlockSpec((tm, d), lambda i: (i, 0)),
        out_shape=jax.ShapeDtypeStruct((m, d), F32),
        scratch_shapes=[pltpu.VMEM((2, 2 * tm, d), F32), pltpu.SemaphoreType.DMA((2,))],
        compiler_params=_cparams("arbitrary"),
        name="moe_combine",
    )(pos_tiles, pos_tiles, x, info, mods, y_rows)


def _moe_layer(x, mods, layer, row, g2, router_w, router_b, w1, w3, w2, tm_route, tm_exp, tf):
    m, d = x.shape
    rw_pad = jnp.pad(router_w, ((0, 0), (0, LANES - N_EXPERTS)))
    rb_pad = jnp.pad(router_b, (0, LANES - N_EXPERTS)).reshape(1, LANES)
    h2, info, cnt = _router(x, mods, layer, row, g2, rw_pad, rb_pad, tm_route)
    counts = cnt[0, :N_EXPERTS].astype(jnp.int32)
    tiles_per = (counts + tm_exp - 1) // tm_exp
    tile_end = jnp.cumsum(tiles_per)
    group_start = (tile_end - tiles_per) * tm_exp
    n_tiles = (2 * m) // tm_exp + N_EXPERTS
    n_used = tile_end[-1:].astype(jnp.int32)
    tile_ids = jnp.arange(n_tiles, dtype=jnp.int32)
    tile_expert = jnp.minimum(jnp.sum(tile_end[None, :] <= tile_ids[:, None], axis=1),
                              N_EXPERTS - 1).astype(jnp.int32)
    rows_before = (tile_ids - (tile_end - tiles_per)[tile_expert]) * tm_exp
    n_valid = jnp.clip(counts[tile_expert] - rows_before, 0, tm_exp).astype(jnp.int32)
    e12 = info[:, 0:2].astype(jnp.int32)
    rank12 = info[:, 2:4].astype(jnp.int32)
    pos = group_start[e12] + rank12
    tok = jnp.broadcast_to(jnp.arange(m, dtype=jnp.int32)[:, None], (m, 2))
    src_rows = jnp.zeros((n_tiles * tm_exp,), jnp.int32).at[pos.reshape(-1)].set(tok.reshape(-1))
    y_rows = _moe_ffn(h2, tile_expert, n_used, n_valid, src_rows.reshape(n_tiles, 1, tm_exp),
                      w1, w3, w2, tm_exp, tf)
    nt = m // tm_route
    pos_tiles = pos.reshape(nt, tm_route, 2).transpose(0, 2, 1).reshape(nt, 1, 2 * tm_route)
    return _moe_combine(x, info, mods, layer, row, y_rows, pos_tiles, tm_route)


def _rope_tables(n_tokens):
    n_rows = n_tokens // GRID_W
    inv = ROPE_THETA ** (-jnp.arange(ROPE_PAIRS, dtype=F32) / ROPE_PAIRS)
    ang_r = jnp.arange(n_rows).astype(F32)[:, None] * inv
    ang_c = jnp.arange(GRID_W).astype(F32)[:, None] * inv
    cos_r = jnp.repeat(jnp.cos(ang_r), GRID_W, axis=0)
    sin_r = jnp.repeat(jnp.sin(ang_r), GRID_W, axis=0)
    cos_c = jnp.tile(jnp.cos(ang_c), (n_rows, 1))
    sin_c = jnp.tile(jnp.sin(ang_c), (n_rows, 1))
    zero = jnp.zeros_like(sin_r)
    cos_t = jnp.concatenate([cos_r, cos_r, cos_c, cos_c], axis=-1)
    sa_t = jnp.concatenate([-sin_r, zero, -sin_c, zero], axis=-1)
    sb_t = jnp.concatenate([zero, sin_r, zero, sin_c], axis=-1)
    return cos_t, sa_t, sb_t


def kernel(x, c, ctx, c_ctx, ada_w, ada_b, norm1_g, norm2_g, w_in, q_norm_g, k_norm_g, attn_sink,
           ret_decay, ret_gn_g, w_out, ffn_w1, ffn_w3, ffn_w2, router_w, router_b, moe_w1, moe_w3, moe_w2):
    b, s, d = x.shape
    n_ctx = ctx.shape[1]
    depth = ada_w.shape[0]
    assert b == 1, "one latent sequence per call"
    x_lat = x[0]
    x_ctx = ctx[0]

    tm_lat = min(1024, s)
    tm_ffn = min(512, s)
    tq = min(512, s)
    tf = 512
    tm_moe = 1024
    tf_moe = 256

    rows = jnp.concatenate([c[0:1], c_ctx[None, :], jnp.zeros((6, d), F32)], axis=0)
    mods = _ada_mods(rows, ada_w, ada_b)
    cos_l, sa_l, sb_l = _rope_tables(s)
    cos_c = jnp.ones((n_ctx, HEAD_DIM), F32)
    zer_c = jnp.zeros((n_ctx, HEAD_DIM), F32)
    log_g_all = jax.nn.log_sigmoid(ret_decay.astype(F32))

    for i in range(depth):
        last = i == depth - 1
        w_in_i = w_in[i].astype(BF16)
        w_out_i = w_out[i].astype(BF16)
        g1 = norm1_g[i].reshape(1, d)
        g2 = norm2_g[i].reshape(1, d)
        log_g = log_g_all[i]
        gain = jnp.concatenate([
            jnp.tile(q_norm_g[i].astype(F32) * ATT_SCALE, N_ATT_HEADS),
            jnp.tile(k_norm_g[i].astype(F32), N_KV_HEADS),
            jnp.ones((KV_W + RET_W,), F32),
            jnp.full((RET_W,), ATT_SCALE, F32),
            jnp.ones((2 * RET_W,), F32)]).reshape(1, IN_COLS)

        p_lat = _in_projection(x_lat, mods, i, 0, g1, w_in_i, gain, cos_l, sa_l, sb_l, tm_lat)
        p_ctx = _in_projection(x_ctx, mods, i, 1, g1, w_in_i, gain, cos_c, zer_c, zer_c, n_ctx)

        att_c, ret_c, s_f, s_b = _ctx_mixer(p_ctx, attn_sink[i], log_g, ret_gn_g[i])
        att_l = _window_attention(p_lat, p_ctx, attn_sink[i], tq)
        ret_l = _retention(p_lat, log_g, s_f, s_b, ret_gn_g[i])
        x_lat = _out_projection(x_lat, att_l, ret_l, w_out_i, mods, i, 0, tm_ffn)

        j = i // 2
        if i % 2 == 0:
            w1 = ffn_w1[j].astype(BF16)
            w3 = ffn_w3[j].astype(BF16)
            w2 = ffn_w2[j].astype(BF16)
            x_lat = _dense_ffn(x_lat, mods, i, 0, g2, w1, w3, w2, tm_ffn, tf)
        else:
            x_lat = _moe_layer(x_lat, mods, i, 0, g2, router_w[j], router_b[j],
                               moe_w1[j], moe_w3[j], moe_w2[j], tm_ffn, min(tm_moe, s), tf_moe)

        if not last:
            x_ctx = _out_projection(x_ctx, att_c, ret_c, w_out_i, mods, i, 1, n_ctx)
            if i % 2 == 0:
                x_ctx = _dense_ffn(x_ctx, mods, i, 1, g2, w1, w3, w2, n_ctx, tf)
            else:
                x_ctx = _moe_layer(x_ctx, mods, i, 1, g2, router_w[j], router_b[j],
                                   moe_w1[j], moe_w3[j], moe_w2[j], n_ctx, n_ctx, tf_moe)
    return x_lat[None]
```

```python
import functools

import jax
import jax.numpy as jnp
from jax import lax
from jax.experimental import pallas as pl
from jax.experimental.pallas import tpu as pltpu

F32 = jnp.float32
BF16 = jnp.bfloat16

HEAD_DIM = 128
N_ATT_HEADS = 8
N_KV_HEADS = 2
N_RET_HEADS = 8
GQA_GROUP = N_ATT_HEADS // N_KV_HEADS
ATT_W = N_ATT_HEADS * HEAD_DIM
KV_W = N_KV_HEADS * HEAD_DIM
RET_W = N_RET_HEADS * HEAD_DIM
IN_COLS = ATT_W + 2 * KV_W + 4 * RET_W
WINDOW = 128
RET_CHUNK = 128
GRID_W = 64
ROPE_THETA = 10000.0
ROPE_PAIRS = HEAD_DIM // 4
N_EXPERTS = 8
EPS = 1e-6
ATT_SCALE = HEAD_DIM ** -0.5

COL_Q = 0
COL_K = ATT_W // HEAD_DIM
COL_V = COL_K + N_KV_HEADS
COL_RQ = COL_V + N_KV_HEADS
COL_RK = COL_RQ + N_RET_HEADS
COL_RV = COL_RK + N_RET_HEADS
COL_G = COL_RV + N_RET_HEADS

VMEM_LIMIT_BYTES = 56 * 1024 * 1024
LANES = 128

NEG_INF = float("-inf")


def _cparams(*sem):
    return pltpu.CompilerParams(dimension_semantics=sem, vmem_limit_bytes=VMEM_LIMIT_BYTES)


def _silu(x):
    return x * (1.0 / (1.0 + jnp.exp(-x)))


def _dot(a, b):
    return jnp.dot(a, b, preferred_element_type=F32)


def _dot_nt(a, b):
    return lax.dot_general(a, b, (((1,), (1,)), ((), ())), preferred_element_type=F32)


def _dot_tn(a, b):
    return lax.dot_general(a, b, (((0,), (0,)), ((), ())), preferred_element_type=F32)


def _norm_modulate(x, g, shift, scale):
    ms = jnp.mean(x * x, axis=-1, keepdims=True)
    return (x * lax.rsqrt(ms + EPS) * g) * (1.0 + scale) + shift


def _ada_kernel(r_ref, w_ref, b_ref, o_ref):
    s = _silu(r_ref[...])
    o_ref[...] = jnp.dot(s, w_ref[...], preferred_element_type=F32,
                         precision=lax.Precision.HIGHEST) + b_ref[...]


def _ada_mods(rows, ada_w, ada_b):
    depth, d, n = ada_w.shape
    tn = d // 2
    return pl.pallas_call(
        _ada_kernel,
        grid=(depth, n // tn),
        in_specs=[
            pl.BlockSpec((8, d), lambda l, j: (0, 0)),
            pl.BlockSpec((None, d, tn), lambda l, j: (l, 0, j)),
            pl.BlockSpec((None, 1, tn), lambda l, j: (l, 0, j)),
        ],
        out_specs=pl.BlockSpec((None, 8, tn), lambda l, j: (l, 0, j)),
        out_shape=jax.ShapeDtypeStruct((depth, 8, n), F32),
        compiler_params=_cparams("arbitrary", "arbitrary"),
        name="ada_mods",
    )(rows, ada_w, ada_b.reshape(depth, 1, n))


def _rope(y, cos, sa, sb):
    return y * cos + pltpu.roll(y, 96, 1) * sa + pltpu.roll(y, 32, 1) * sb


def _inproj_kernel(x_ref, sh_ref, sc_ref, g1_ref, w_ref, gain_ref, cos_ref, sa_ref, sb_ref, o_ref,
                   h_scr, p_scr, *, row, tn, n_j, n_steps):
    t = pl.program_id(0)
    j = t % n_j
    nchunk = tn // HEAD_DIM

    @pl.when(t == 0)
    def _():
        p_scr[...] = jnp.zeros_like(p_scr)

    @pl.when((j == 0) & (t < n_steps))
    def _():
        h = _norm_modulate(x_ref[...], g1_ref[...], sh_ref[row:row + 1, :], sc_ref[row:row + 1, :])
        h_scr[...] = h.astype(BF16)

    p_new = _dot(h_scr[...], w_ref[...])

    jp = (t + n_j - 1) % n_j
    tabs = None
    for c in range(nchunk):
        gc = jp * nchunk + c
        norm_on = gc < COL_V
        rope_on = norm_on | ((gc >= COL_RQ) & (gc < COL_RV))
        if c % N_KV_HEADS == 0:
            tabs = (jnp.where(rope_on, cos_ref[...], 1.0), jnp.where(rope_on, sa_ref[...], 0.0),
                    jnp.where(rope_on, sb_ref[...], 0.0))
        cols = slice(c * HEAD_DIM, (c + 1) * HEAD_DIM)
        pc = p_scr[:, cols]
        r = lax.rsqrt(jnp.mean(pc * pc, axis=-1, keepdims=True) + EPS)
        y = pc * jnp.where(norm_on, r, 1.0) * gain_ref[:, cols]
        o_ref[:, cols] = _rope(y, *tabs).astype(o_ref.dtype)
    p_scr[...] = p_new


def _in_projection(x, mods, layer, row, g1, w, gain, cos, sa, sb, tm):
    m, d = x.shape
    n = w.shape[1]
    tn = 2 * KV_W
    assert n == IN_COLS and ATT_W % tn == 0 and RET_W % tn == 0 and m % tm == 0
    n_i, n_j = m // tm, n // tn
    n_steps = n_i * n_j
    prev = lambda t: jnp.maximum(t - 1, 0)
    tab = pl.BlockSpec((tm, HEAD_DIM), lambda t: (prev(t) // n_j, 0))
    return pl.pallas_call(
        functools.partial(_inproj_kernel, row=row, tn=tn, n_j=n_j, n_steps=n_steps),
        grid=(n_steps + 1,),
        in_specs=[
            pl.BlockSpec((tm, d), lambda t: (jnp.minimum(t // n_j, n_i - 1), 0)),
            pl.BlockSpec((None, 8, d), lambda t: (layer, 0, 0)),
            pl.BlockSpec((None, 8, d), lambda t: (layer, 0, 1)),
            pl.BlockSpec((1, d), lambda t: (0, 0)),
            pl.BlockSpec((d, tn), lambda t: (0, t % n_j)),
            pl.BlockSpec((1, tn), lambda t: (0, prev(t) % n_j)),
            tab, tab, tab,
        ],
        out_specs=pl.BlockSpec((tm, tn), lambda t: (prev(t) // n_j, prev(t) % n_j)),
        out_shape=jax.ShapeDtypeStruct((m, n), BF16),
        scratch_shapes=[pltpu.VMEM((tm, d), BF16), pltpu.VMEM((tm, tn), F32)],
        compiler_params=_cparams("arbitrary"),
        name="in_projection",
    )(x, mods, mods, g1, w, gain, cos, sa, sb)


def _attn_kernel(sink_ref, q_ref, km_ref, vm_ref, kp_ref, vp_ref, kn_ref, vn_ref,
                 kc_ref, vc_ref, bias_ref, o_ref, kbuf, vbuf, vcbuf, *, tq):
    i = pl.program_id(0)
    last = pl.num_programs(0) - 1
    nsub = tq // WINDOW
    n_ctx_blk = kc_ref.shape[0] // HEAD_DIM
    kbuf[0:WINDOW] = kp_ref[...]
    kbuf[WINDOW:WINDOW + tq] = km_ref[...]
    kbuf[WINDOW + tq:] = kn_ref[...]
    for h in range(N_KV_HEADS):
        hs = slice(h * HEAD_DIM, (h + 1) * HEAD_DIM)
        va = slice(2 * h * HEAD_DIM, (2 * h + 1) * HEAD_DIM)
        vbuf[0:WINDOW, va] = vp_ref[:, hs]
        vbuf[WINDOW:WINDOW + tq, va] = vm_ref[:, hs]
        vbuf[WINDOW + tq:, va] = vn_ref[:, hs]

    @pl.when(i == 0)
    def _():
        for h in range(N_KV_HEADS):
            ones = slice((2 * h + 1) * HEAD_DIM, (2 * h + 2) * HEAD_DIM)
            vbuf[:, ones] = jnp.ones((vbuf.shape[0], HEAD_DIM), vbuf.dtype)
            vcbuf[:, ones] = jnp.ones((vcbuf.shape[0], HEAD_DIM), vcbuf.dtype)
            vcbuf[:, 2 * h * HEAD_DIM:(2 * h + 1) * HEAD_DIM] = vc_ref[:, h * HEAD_DIM:(h + 1) * HEAD_DIM]

    col = lax.broadcasted_iota(jnp.int32, (WINDOW, 3 * WINDOW), 1)
    for s in range(nsub):
        bias = bias_ref[...]
        if s == 0:
            bias = bias + jnp.where(col < WINDOW, jnp.where(i == 0, NEG_INF, 0.0), 0.0)
        if s == nsub - 1:
            bias = bias + jnp.where(col >= 2 * WINDOW, jnp.where(i == last, NEG_INF, 0.0), 0.0)
        bias4 = jnp.concatenate([bias] * GQA_GROUP, axis=0)
        for h in range(N_KV_HEADS):
            hs = slice(h * HEAD_DIM, (h + 1) * HEAD_DIM)
            q = jnp.concatenate(
                [q_ref[s * WINDOW:(s + 1) * WINDOW,
                       (h * GQA_GROUP + g) * HEAD_DIM:(h * GQA_GROUP + g + 1) * HEAD_DIM]
                 for g in range(GQA_GROUP)], axis=0)
            vs = slice(2 * h * HEAD_DIM, (2 * h + 2) * HEAD_DIM)
            kl = kbuf[s * WINDOW:(s + 3) * WINDOW, hs]
            vl = vbuf[s * WINDOW:(s + 3) * WINDOW, vs]
            s_loc = _dot_nt(q, kl) + bias4
            s_ctx = _dot_nt(q, kc_ref[:, hs])
            sink = jnp.concatenate(
                [jnp.full((WINDOW, 1), sink_ref[h * GQA_GROUP + g], F32) for g in range(GQA_GROUP)],
                axis=0)
            blocks = ([s_loc[:, b * HEAD_DIM:(b + 1) * HEAD_DIM] for b in range(3)]
                      + [s_ctx[:, b * HEAD_DIM:(b + 1) * HEAD_DIM] for b in range(n_ctx_blk)])
            m = jnp.maximum(jnp.max(functools.reduce(jnp.maximum, blocks), axis=-1, keepdims=True), sink)
            p_loc = jnp.exp(s_loc - m).astype(BF16)
            p_ctx = jnp.exp(s_ctx - m).astype(BF16)
            o_den = _dot(p_loc, vl) + _dot(p_ctx, vcbuf[:, vs])
            o = o_den[:, :HEAD_DIM] / (o_den[:, HEAD_DIM:] + jnp.exp(sink - m))
            for g in range(GQA_GROUP):
                o_ref[s * WINDOW:(s + 1) * WINDOW,
                      (h * GQA_GROUP + g) * HEAD_DIM:(h * GQA_GROUP + g + 1) * HEAD_DIM] = (
                    o[g * WINDOW:(g + 1) * WINDOW].astype(o_ref.dtype))


def _band_bias():
    r = jnp.arange(WINDOW)[:, None]
    c = jnp.arange(3 * WINDOW)[None, :]
    ok = (c - r >= 0) & (c - r <= 2 * WINDOW)
    return jnp.where(ok, 0.0, NEG_INF).astype(F32)


def _window_attention(p_lat, p_ctx, sink, tq):
    s = p_lat.shape[0]
    c = p_ctx.shape[0]
    r = tq // WINDOW
    nb = s // WINDOW
    kvb = KV_W // HEAD_DIM
    ck, cv = COL_K // kvb, COL_V // kvb
    return pl.pallas_call(
        functools.partial(_attn_kernel, tq=tq),
        grid=(s // tq,),
        in_specs=[
            pl.BlockSpec(memory_space=pltpu.SMEM),
            pl.BlockSpec((tq, ATT_W), lambda i: (i, 0)),
            pl.BlockSpec((tq, KV_W), lambda i: (i, ck)),
            pl.BlockSpec((tq, KV_W), lambda i: (i, cv)),
            pl.BlockSpec((WINDOW, KV_W), lambda i: (jnp.maximum(i * r - 1, 0), ck)),
            pl.BlockSpec((WINDOW, KV_W), lambda i: (jnp.maximum(i * r - 1, 0), cv)),
            pl.BlockSpec((WINDOW, KV_W), lambda i: (jnp.minimum((i + 1) * r, nb - 1), ck)),
            pl.BlockSpec((WINDOW, KV_W), lambda i: (jnp.minimum((i + 1) * r, nb - 1), cv)),
            pl.BlockSpec((c, KV_W), lambda i: (0, ck)),
            pl.BlockSpec((c, KV_W), lambda i: (0, cv)),
            pl.BlockSpec((WINDOW, 3 * WINDOW), lambda i: (0, 0)),
        ],
        out_specs=pl.BlockSpec((tq, ATT_W), lambda i: (i, 0)),
        out_shape=jax.ShapeDtypeStruct((s, ATT_W), BF16),
        scratch_shapes=[pltpu.VMEM((tq + 2 * WINDOW, KV_W), BF16),
                        pltpu.VMEM((tq + 2 * WINDOW, 2 * KV_W), BF16),
                        pltpu.VMEM((c, 2 * KV_W), BF16)],
        compiler_params=_cparams("arbitrary"),
        name="window_attention",
    )(sink, p_lat, p_lat, p_lat, p_lat, p_lat, p_lat, p_lat, p_ctx, p_ctx, _band_bias())


def _ctx_mixer_kernel(sink_ref, lg_ref, q_ref, k_ref, v_ref, rq_ref, rk_ref, rv_ref, g_ref, gn_ref,
                      att_ref, ret_ref, sf_ref, sb_ref):
    h = pl.program_id(0)
    c = q_ref.shape[0]
    lg_f = lg_ref[0, h]
    lg_b = lg_ref[1, h]
    s = _dot_nt(q_ref[...], k_ref[...])
    sink = jnp.full((c, 1), sink_ref[h], F32)
    m = jnp.maximum(jnp.max(s, axis=-1, keepdims=True), sink)
    p = jnp.exp(s - m)
    den = jnp.sum(p, axis=-1, keepdims=True) + jnp.exp(sink - m)
    att_ref[...] = (_dot(p.astype(BF16), v_ref[...]) / den).astype(att_ref.dtype)
    n_i = lax.broadcasted_iota(jnp.int32, (c, c), 0)
    m_i = lax.broadcasted_iota(jnp.int32, (c, c), 1)
    rel = (n_i - m_i).astype(F32)
    dec = (jnp.where(rel >= 0, jnp.exp(jnp.maximum(rel, 0.0) * lg_f), 0.0)
           + jnp.where(rel <= 0, jnp.exp(jnp.maximum(-rel, 0.0) * lg_b), 0.0))
    sc = _dot_nt(rq_ref[...], rk_ref[...]) * dec
    y = _dot(sc.astype(BF16), rv_ref[...])
    mu = jnp.mean(y, axis=-1, keepdims=True)
    var = jnp.mean(jnp.square(y - mu), axis=-1, keepdims=True)
    yn = (y - mu) * lax.rsqrt(var + EPS) * gn_ref[...]
    ret_ref[...] = (_silu(g_ref[...].astype(F32)) * yn).astype(ret_ref.dtype)
    pos = lax.broadcasted_iota(jnp.int32, (c, HEAD_DIM), 0).astype(F32)
    kf = rk_ref[...].astype(F32)
    k_f = (kf * jnp.exp((c - 1.0 - pos) * lg_f)).astype(BF16)
    k_b = (kf * jnp.exp(pos * lg_b)).astype(BF16)
    sf_ref[...] = _dot_tn(k_f, rv_ref[...])
    sb_ref[...] = _dot_tn(k_b, rv_ref[...])


def _ctx_mixer(p_ctx, sink, log_g, gn_g):
    c = p_ctx.shape[0]
    hb = lambda off: pl.BlockSpec((c, HEAD_DIM), lambda h: (0, off + h))
    return pl.pallas_call(
        _ctx_mixer_kernel,
        grid=(N_RET_HEADS,),
        in_specs=[
            pl.BlockSpec(memory_space=pltpu.SMEM),
            pl.BlockSpec(memory_space=pltpu.SMEM),
            hb(COL_Q),
            pl.BlockSpec((c, HEAD_DIM), lambda h: (0, COL_K + h // GQA_GROUP)),
            pl.BlockSpec((c, HEAD_DIM), lambda h: (0, COL_V + h // GQA_GROUP)),
            hb(COL_RQ), hb(COL_RK), hb(COL_RV), hb(COL_G),
            pl.BlockSpec((None, 1, HEAD_DIM), lambda h: (h, 0, 0)),
        ],
        out_specs=[
            pl.BlockSpec((c, HEAD_DIM), lambda h: (0, h)),
            pl.BlockSpec((c, HEAD_DIM), lambda h: (0, h)),
            pl.BlockSpec((None, HEAD_DIM, HEAD_DIM), lambda h: (h, 0, 0)),
            pl.BlockSpec((None, HEAD_DIM, HEAD_DIM), lambda h: (h, 0, 0)),
        ],
        out_shape=[
            jax.ShapeDtypeStruct((c, ATT_W), BF16),
            jax.ShapeDtypeStruct((c, RET_W), BF16),
            jax.ShapeDtypeStruct((N_RET_HEADS, HEAD_DIM, HEAD_DIM), F32),
            jax.ShapeDtypeStruct((N_RET_HEADS, HEAD_DIM, HEAD_DIM), F32),
        ],
        compiler_params=_cparams("arbitrary"),
        name="ctx_mixer",
    )(sink, log_g, p_ctx, p_ctx, p_ctx, p_ctx, p_ctx, p_ctx, p_ctx,
      gn_g.reshape(N_RET_HEADS, 1, HEAD_DIM))


RET_KERNEL_CHUNK = 256
RET_UNROLL = 4


def _ret_kernel(lg_ref, q_ref, k_ref, v_ref, g_ref, sf_ref, sb_ref, gn_ref, o_ref,
                y_scr, dmat, qdf, kdf, qdb, kdb, cdf, cdb):
    h = pl.program_id(0)
    ck = RET_KERNEL_CHUNK
    nit = q_ref.shape[0] // (ck * RET_UNROLL)
    lg_f = lg_ref[0, h]
    lg_b = lg_ref[1, h]
    ii = lax.broadcasted_iota(jnp.int32, (ck, ck), 0)
    jj = lax.broadcasted_iota(jnp.int32, (ck, ck), 1)
    rel = (ii - jj).astype(F32)
    dmat[...] = (jnp.where(rel >= 0, jnp.exp(jnp.maximum(rel, 0.0) * lg_f), 0.0)
                 + jnp.where(rel <= 0, jnp.exp(jnp.maximum(-rel, 0.0) * lg_b), 0.0))
    pos = lax.broadcasted_iota(jnp.int32, (ck, HEAD_DIM), 0).astype(F32)
    qdf[...] = jnp.exp((pos + 1.0) * lg_f)
    kdf[...] = jnp.exp((ck - 1.0 - pos) * lg_f)
    qdb[...] = jnp.exp((ck - pos) * lg_b)
    kdb[...] = jnp.exp(pos * lg_b)
    full = jnp.full((HEAD_DIM, HEAD_DIM), float(ck), F32)
    cdf[...] = jnp.exp(full * lg_f)
    cdb[...] = jnp.exp(full * lg_b)

    def rows_of(z):
        return pl.ds(pl.multiple_of(z * ck, ck), ck)

    def fwd(it, state):
        zs = [it * RET_UNROLL + u for u in range(RET_UNROLL)]
        qs = [q_ref[rows_of(z), :] for z in zs]
        ks = [k_ref[rows_of(z), :] for z in zs]
        vs = [v_ref[rows_of(z), :] for z in zs]
        inner = [_dot((_dot_nt(q, k) * dmat[...]).astype(BF16), v) for q, k, v in zip(qs, ks, vs)]
        kvs = [_dot_tn((k.astype(F32) * kdf[...]).astype(BF16), v) for k, v in zip(ks, vs)]
        for u, z in enumerate(zs):
            y_scr[rows_of(z), :] = inner[u] + _dot(qs[u], state.astype(BF16)) * qdf[...]
            state = cdf[...] * state + kvs[u]
        return state

    lax.fori_loop(0, nit, fwd, sf_ref[...])

    def bwd(it, state):
        zs = [nit * RET_UNROLL - 1 - (it * RET_UNROLL + u) for u in range(RET_UNROLL)]
        qs = [q_ref[rows_of(z), :] for z in zs]
        kvs = [_dot_tn((k_ref[rows_of(z), :].astype(F32) * kdb[...]).astype(BF16), v_ref[rows_of(z), :])
               for z in zs]
        for u, z in enumerate(zs):
            y = y_scr[rows_of(z), :] + _dot(qs[u], state.astype(BF16)) * qdb[...]
            state = cdb[...] * state + kvs[u]
            mu = jnp.mean(y, axis=-1, keepdims=True)
            var = jnp.mean(jnp.square(y - mu), axis=-1, keepdims=True)
            yn = (y - mu) * lax.rsqrt(var + EPS) * gn_ref[...]
            o_ref[rows_of(z), :] = (_silu(g_ref[rows_of(z), :].astype(F32)) * yn).astype(o_ref.dtype)
        return state

    lax.fori_loop(0, nit, bwd, sb_ref[...])


def _retention(p_lat, log_g, s_f, s_b, gn_g):
    s = p_lat.shape[0]
    ck = RET_KERNEL_CHUNK
    assert s % (ck * RET_UNROLL) == 0
    col = lambda off: pl.BlockSpec((s, HEAD_DIM), lambda h: (0, off + h))
    state_spec = pl.BlockSpec((None, HEAD_DIM, HEAD_DIM), lambda h: (h, 0, 0))
    vec = pltpu.VMEM((ck, HEAD_DIM), F32)
    sq = pltpu.VMEM((HEAD_DIM, HEAD_DIM), F32)
    return pl.pallas_call(
        _ret_kernel,
        grid=(N_RET_HEADS,),
        in_specs=[pl.BlockSpec(memory_space=pltpu.SMEM),
                  col(COL_RQ), col(COL_RK), col(COL_RV), col(COL_G), state_spec, state_spec,
                  pl.BlockSpec((None, 1, HEAD_DIM), lambda h: (h, 0, 0))],
        out_specs=col(0),
        out_shape=jax.ShapeDtypeStruct((s, RET_W), BF16),
        scratch_shapes=[pltpu.VMEM((s, HEAD_DIM), F32), pltpu.VMEM((ck, ck), F32),
                        vec, vec, vec, vec, sq, sq],
        compiler_params=_cparams("arbitrary"),
        name="retention",
    )(log_g, p_lat, p_lat, p_lat, p_lat, s_f, s_b, gn_g.reshape(N_RET_HEADS, 1, HEAD_DIM))


def _outproj_kernel(x_ref, a_ref, r_ref, wa_ref, wr_ref, gate_ref, o_ref, *, row):
    y = _dot(a_ref[...], wa_ref[...]) + _dot(r_ref[...], wr_ref[...])
    o_ref[...] = x_ref[...] + gate_ref[row:row + 1, :] * y


def _out_projection(x, att, ret, w_out, mods, layer, row, tm):
    m, d = x.shape
    return pl.pallas_call(
        functools.partial(_outproj_kernel, row=row),
        grid=(m // tm,),
        in_specs=[
            pl.BlockSpec((tm, d), lambda i: (i, 0)),
            pl.BlockSpec((tm, ATT_W), lambda i: (i, 0)),
            pl.BlockSpec((tm, RET_W), lambda i: (i, 0)),
            pl.BlockSpec((ATT_W, d), lambda i: (0, 0)),
            pl.BlockSpec((RET_W, d), lambda i: (ATT_W // RET_W, 0)),
            pl.BlockSpec((None, 8, d), lambda i: (layer, 0, 2)),
        ],
        out_specs=pl.BlockSpec((tm, d), lambda i: (i, 0)),
        out_shape=jax.ShapeDtypeStruct((m, d), F32),
        compiler_params=_cparams("arbitrary"),
        name="out_projection",
    )(x, att, ret, w_out, w_out, mods)


def _ffn_kernel(x_ref, sh_ref, sc_ref, gate_ref, g2_ref, w1_ref, w3_ref, w2_ref, o_ref,
                h_scr, acc, *, row):
    j = pl.program_id(1)

    @pl.when(j == 0)
    def _():
        h = _norm_modulate(x_ref[...], g2_ref[...], sh_ref[row:row + 1, :], sc_ref[row:row + 1, :])
        h_scr[...] = h.astype(BF16)
        acc[...] = jnp.zeros_like(acc)

    h = h_scr[...]
    u = (_silu(_dot(h, w1_ref[...])) * _dot(h, w3_ref[...])).astype(BF16)
    acc[...] += _dot(u, w2_ref[...])

    @pl.when(j == pl.num_programs(1) - 1)
    def _():
        o_ref[...] = x_ref[...] + gate_ref[row:row + 1, :] * acc[...]


def _dense_ffn(x, mods, layer, row, g2, w1, w3, w2, tm, tf):
    m, d = x.shape
    f = w1.shape[1]
    mod = lambda k: pl.BlockSpec((None, 8, d), lambda i, j: (layer, 0, k))
    return pl.pallas_call(
        functools.partial(_ffn_kernel, row=row),
        grid=(m // tm, f // tf),
        in_specs=[
            pl.BlockSpec((tm, d), lambda i, j: (i, 0)),
            mod(3), mod(4), mod(5),
            pl.BlockSpec((1, d), lambda i, j: (0, 0)),
            pl.BlockSpec((d, tf), lambda i, j: (0, j)),
            pl.BlockSpec((d, tf), lambda i, j: (0, j)),
            pl.BlockSpec((tf, d), lambda i, j: (j, 0)),
        ],
        out_specs=pl.BlockSpec((tm, d), lambda i, j: (i, 0)),
        out_shape=jax.ShapeDtypeStruct((m, d), F32),
        scratch_shapes=[pltpu.VMEM((tm, d), BF16), pltpu.VMEM((tm, d), F32)],
        compiler_params=_cparams("arbitrary", "arbitrary"),
        name="dense_ffn",
    )(x, mods, mods, mods, g2, w1, w3, w2)


def _router_kernel(x_ref, sh_ref, sc_ref, g2_ref, rw_ref, rb_ref, h_ref, info_ref, cnt_ref, carry, *, row):
    i = pl.program_id(0)
    tm = x_ref.shape[0]

    @pl.when(i == 0)
    def _():
        carry[...] = jnp.zeros_like(carry)

    h = _norm_modulate(x_ref[...], g2_ref[...], sh_ref[row:row + 1, :], sc_ref[row:row + 1, :])
    h_ref[...] = h
    logits = jnp.dot(h, rw_ref[...], preferred_element_type=F32,
                     precision=lax.Precision.HIGHEST) + rb_ref[...]
    lane = lax.broadcasted_iota(jnp.int32, (tm, LANES), 1).astype(F32)
    logits = jnp.where(lane < N_EXPERTS, logits, NEG_INF)
    v1 = jnp.max(logits, axis=-1, keepdims=True)
    e1 = jnp.min(jnp.where(logits == v1, lane, float(LANES)), axis=-1, keepdims=True)
    rest = jnp.where(lane == e1, NEG_INF, logits)
    v2 = jnp.max(rest, axis=-1, keepdims=True)
    e2 = jnp.min(jnp.where(rest == v2, lane, float(LANES)), axis=-1, keepdims=True)
    t = jnp.exp(v2 - v1)
    w1 = 1.0 / (1.0 + t)
    w2 = t / (1.0 + t)
    oh1 = jnp.where(lane == e1, 1.0, 0.0)
    oh2 = jnp.where(lane == e2, 1.0, 0.0)
    oh = oh1 + oh2
    r_i = lax.broadcasted_iota(jnp.int32, (tm, tm), 0)
    c_i = lax.broadcasted_iota(jnp.int32, (tm, tm), 1)
    tri = jnp.where(c_i < r_i, 1.0, 0.0).astype(BF16)
    before = _dot(tri, oh.astype(BF16)) + carry[0:1, :]
    rank1 = jnp.sum(before * oh1, axis=-1, keepdims=True)
    rank2 = jnp.sum(before * oh2, axis=-1, keepdims=True)
    carry[...] = carry[...] + jnp.sum(oh, axis=0, keepdims=True)
    info = jnp.where(lane == 0.0, e1, 0.0)
    info = jnp.where(lane == 1.0, e2, info)
    info = jnp.where(lane == 2.0, rank1, info)
    info = jnp.where(lane == 3.0, rank2, info)
    info = jnp.where(lane == 4.0, w1, info)
    info = jnp.where(lane == 5.0, w2, info)
    info_ref[...] = info
    cnt_ref[...] = carry[...]


def _router(x, mods, layer, row, g2, rw_pad, rb_pad, tm):
    m, d = x.shape
    mod = lambda k: pl.BlockSpec((None, 8, d), lambda i: (layer, 0, k))
    return pl.pallas_call(
        functools.partial(_router_kernel, row=row),
        grid=(m // tm,),
        in_specs=[
            pl.BlockSpec((tm, d), lambda i: (i, 0)),
            mod(3), mod(4),
            pl.BlockSpec((1, d), lambda i: (0, 0)),
            pl.BlockSpec((d, LANES), lambda i: (0, 0)),
            pl.BlockSpec((1, LANES), lambda i: (0, 0)),
        ],
        out_specs=[
            pl.BlockSpec((tm, d), lambda i: (i, 0)),
            pl.BlockSpec((tm, LANES), lambda i: (i, 0)),
            pl.BlockSpec((8, LANES), lambda i: (0, 0)),
        ],
        out_shape=[
            jax.ShapeDtypeStruct((m, d), F32),
            jax.ShapeDtypeStruct((m, LANES), F32),
            jax.ShapeDtypeStruct((8, LANES), F32),
        ],
        scratch_shapes=[pltpu.VMEM((8, LANES), F32)],
        compiler_params=_cparams("arbitrary"),
        name="moe_router",
    )(x, mods, mods, g2, rw_pad, rb_pad)


GATHER_UNROLL = 8


def _gather_rows(idx_ref, src_hbm, dst, sem, n):
    def body(kb, carry):
        for u in range(GATHER_UNROLL):
            k = kb * GATHER_UNROLL + u
            pltpu.make_async_copy(src_hbm.at[pl.ds(idx_ref[0, 0, k], 1)], dst.at[pl.ds(k, 1)], sem).start()
        return carry
    lax.fori_loop(0, n // GATHER_UNROLL, body, 0)


def _wait_rows(src_hbm, dst, sem, n):
    pltpu.make_async_copy(src_hbm.at[pl.ds(0, n)], dst, sem).wait()


def _issue_rows(idx_ref, src_hbm, dst, sem, start, count):
    for u in range(count):
        k = start + u
        pltpu.make_async_copy(src_hbm.at[pl.ds(idx_ref[0, 0, k], 1)], dst.at[pl.ds(k, 1)], sem).start()


def _moe_ffn_kernel(te_ref, nu_ref, nv_ref, cur_ref, nxt_ref, h_hbm, w1_ref, w3_ref, w2_ref, o_ref,
                    buf, sem, h_scr, *, tm, n_f):
    r = pl.program_id(0)
    j = pl.program_id(1)
    n_used = nu_ref[0]
    half = tm // 2
    per_step = tm // n_f
    head = tm - per_step * n_f

    @pl.when(r < n_used)
    def _():
        @pl.when(j == 0)
        def _():
            @pl.when(r == 0)
            def _():
                _gather_rows(cur_ref, h_hbm, buf, sem, tm)

            _wait_rows(h_hbm, buf, sem, tm)
            h_scr[...] = buf[...].astype(BF16)

            @pl.when(r + 1 < n_used)
            def _():
                _issue_rows(nxt_ref, h_hbm, buf, sem, 0, head)

            o_ref[...] = jnp.zeros_like(o_ref)

        def ffn_rows(lo):
            h = h_scr[lo:lo + half, :]
            a = _dot(h, w1_ref[...].astype(BF16))
            b = _dot(h, w3_ref[...].astype(BF16))
            u = (_silu(a) * b).astype(BF16)
            o_ref[lo:lo + half, :] += _dot(u, w2_ref[...].astype(BF16))

        @pl.when(r + 1 < n_used)
        def _():
            _issue_rows(nxt_ref, h_hbm, buf, sem, head + j * per_step, per_step)
            ffn_rows(0)

        @pl.when(r + 1 >= n_used)
        def _():
            ffn_rows(0)

        @pl.when(nv_ref[r] > half)
        def _():
            ffn_rows(half)

    @pl.when((r >= n_used) & (j == 0))
    def _():
        o_ref[...] = jnp.zeros_like(o_ref)


def _moe_ffn(h2, tile_expert, n_used, n_valid, src_rows, w1, w3, w2, tm, tf):
    d = h2.shape[1]
    f = w1.shape[2]
    n_tiles = src_rows.shape[0]
    n_f = f // tf
    clamp = lambda r, nu: jnp.maximum(jnp.minimum(r, nu[0] - 1), 0)
    fcol = lambda r, j, nu: jnp.where(r < nu[0], j, n_f - 1)
    grid_spec = pltpu.PrefetchScalarGridSpec(
        num_scalar_prefetch=3,
        grid=(n_tiles, n_f),
        in_specs=[
            pl.BlockSpec((1, 1, tm), lambda r, j, te, nu, nv: (clamp(r, nu), 0, 0), memory_space=pltpu.SMEM),
            pl.BlockSpec((1, 1, tm), lambda r, j, te, nu, nv: (clamp(r + 1, nu), 0, 0), memory_space=pltpu.SMEM),
            pl.BlockSpec(memory_space=pl.ANY),
            pl.BlockSpec((None, d, tf), lambda r, j, te, nu, nv: (te[clamp(r, nu)], 0, fcol(r, j, nu))),
            pl.BlockSpec((None, d, tf), lambda r, j, te, nu, nv: (te[clamp(r, nu)], 0, fcol(r, j, nu))),
            pl.BlockSpec((None, tf, d), lambda r, j, te, nu, nv: (te[clamp(r, nu)], fcol(r, j, nu), 0)),
        ],
        out_specs=pl.BlockSpec((tm, d), lambda r, j, te, nu, nv: (r, 0)),
        scratch_shapes=[
            pltpu.VMEM((tm, d), F32),
            pltpu.SemaphoreType.DMA(()),
            pltpu.VMEM((tm, d), BF16),
        ],
    )
    return pl.pallas_call(
        functools.partial(_moe_ffn_kernel, tm=tm, n_f=n_f),
        grid_spec=grid_spec,
        out_shape=jax.ShapeDtypeStruct((n_tiles * tm, d), F32),
        compiler_params=_cparams("arbitrary", "arbitrary"),
        name="moe_ffn",
    )(tile_expert, n_used, n_valid, src_rows, src_rows, h2, w1, w3, w2)


def _combine_kernel(cur_ref, nxt_ref, x_ref, info_ref, gate_ref, y_hbm, o_ref, buf, sems, *, tm, row):
    i = pl.program_id(0)
    slot = i % 2

    @pl.when(i == 0)
    def _():
        _gather_rows(cur_ref, y_hbm, buf.at[0], sems.at[0], 2 * tm)

    _wait_rows(y_hbm, buf.at[slot], sems.at[slot], 2 * tm)

    @pl.when(i + 1 < pl.num_programs(0))
    def _():
        _gather_rows(nxt_ref, y_hbm, buf.at[1 - slot], sems.at[1 - slot], 2 * tm)

    info = info_ref[...]
    w1 = info[:, 4:5]
    w2 = info[:, 5:6]
    y = w1 * buf[slot, 0:tm, :] + w2 * buf[slot, tm:2 * tm, :]
    o_ref[...] = x_ref[...] + gate_ref[row:row + 1, :] * y


def _moe_combine(x, info, mods, layer, row, y_rows, pos_tiles, tm):
    m, d = x.shape
    nt = m // tm
    return pl.pallas_call(
        functools.partial(_combine_kernel, tm=tm, row=row),
        grid=(nt,),
        in_specs=[
            pl.BlockSpec((1, 1, 2 * tm), lambda i: (i, 0, 0), memory_space=pltpu.SMEM),
            pl.BlockSpec((1, 1, 2 * tm), lambda i: (jnp.minimum(i + 1, nt - 1), 0, 0), memory_space=pltpu.SMEM),
            pl.BlockSpec((tm, d), lambda i: (i, 0)),
            pl.BlockSpec((tm, LANES), lambda i: (i, 0)),
            pl.BlockSpec((None, 8, d), lambda i: (layer, 0, 5)),
            pl.BlockSpec(memory_space=pl.ANY),
        ],
        out_specs=pl.BlockSpec((tm, d), lambda i: (i, 0)),
        out_shape=jax.ShapeDtypeStruct((m, d), F32),
        scratch_shapes=[pltpu.VMEM((2, 2 * tm, d), F32), pltpu.SemaphoreType.DMA((2,))],
        compiler_params=_cparams("arbitrary"),
        name="moe_combine",
    )(pos_tiles, pos_tiles, x, info, mods, y_rows)


def _moe_layer(x, mods, layer, row, g2, router_w, router_b, w1, w3, w2, tm_route, tm_exp, tf):
    m, d = x.shape
    rw_pad = jnp.pad(router_w, ((0, 0), (0, LANES - N_EXPERTS)))
    rb_pad = jnp.pad(router_b, (0, LANES - N_EXPERTS)).reshape(1, LANES)
    h2, info, cnt = _router(x, mods, layer, row, g2, rw_pad, rb_pad, tm_route)
    counts = cnt[0, :N_EXPERTS].astype(jnp.int32)
    tiles_per = (counts + tm_exp - 1) // tm_exp
    tile_end = jnp.cumsum(tiles_per)
    group_start = (tile_end - tiles_per) * tm_exp
    n_tiles = (2 * m) // tm_exp + N_EXPERTS
    n_used = tile_end[-1:].astype(jnp.int32)
    tile_ids = jnp.arange(n_tiles, dtype=jnp.int32)
    tile_expert = jnp.minimum(jnp.sum(tile_end[None, :] <= tile_ids[:, None], axis=1),
                              N_EXPERTS - 1).astype(jnp.int32)
    rows_before = (tile_ids - (tile_end - tiles_per)[tile_expert]) * tm_exp
    n_valid = jnp.clip(counts[tile_expert] - rows_before, 0, tm_exp).astype(jnp.int32)
    e12 = info[:, 0:2].astype(jnp.int32)
    rank12 = info[:, 2:4].astype(jnp.int32)
    pos = group_start[e12] + rank12
    tok = jnp.broadcast_to(jnp.arange(m, dtype=jnp.int32)[:, None], (m, 2))
    src_rows = jnp.zeros((n_tiles * tm_exp,), jnp.int32).at[pos.reshape(-1)].set(tok.reshape(-1))
    y_rows = _moe_ffn(h2, tile_expert, n_used, n_valid, src_rows.reshape(n_tiles, 1, tm_exp),
                      w1, w3, w2, tm_exp, tf)
    nt = m // tm_route
    pos_tiles = pos.reshape(nt, tm_route, 2).transpose(0, 2, 1).reshape(nt, 1, 2 * tm_route)
    return _moe_combine(x, info, mods, layer, row, y_rows, pos_tiles, tm_route)


def _rope_tables(n_tokens):
    n_rows = n_tokens // GRID_W
    inv = ROPE_THETA ** (-jnp.arange(ROPE_PAIRS, dtype=F32) / ROPE_PAIRS)
    ang_r = jnp.arange(n_rows).astype(F32)[:, None] * inv
    ang_c = jnp.arange(GRID_W).astype(F32)[:, None] * inv
    cos_r = jnp.repeat(jnp.cos(ang_r), GRID_W, axis=0)
    sin_r = jnp.repeat(jnp.sin(ang_r), GRID_W, axis=0)
    cos_c = jnp.tile(jnp.cos(ang_c), (n_rows, 1))
    sin_c = jnp.tile(jnp.sin(ang_c), (n_rows, 1))
    zero = jnp.zeros_like(sin_r)
    cos_t = jnp.concatenate([cos_r, cos_r, cos_c, cos_c], axis=-1)
    sa_t = jnp.concatenate([-sin_r, zero, -sin_c, zero], axis=-1)
    sb_t = jnp.concatenate([zero, sin_r, zero, sin_c], axis=-1)
    return cos_t, sa_t, sb_t


def kernel(x, c, ctx, c_ctx, ada_w, ada_b, norm1_g, norm2_g, w_in, q_norm_g, k_norm_g, attn_sink,
           ret_decay, ret_gn_g, w_out, ffn_w1, ffn_w3, ffn_w2, router_w, router_b, moe_w1, moe_w3, moe_w2):
    b, s, d = x.shape
    n_ctx = ctx.shape[1]
    depth = ada_w.shape[0]
    assert b == 1, "one latent sequence per call"
    x_lat = x[0]
    x_ctx = ctx[0]

    tm_lat = min(1024, s)
    tm_ffn = min(512, s)
    tq = min(512, s)
    tf = 512
    tm_moe = 1024
    tf_moe = 256

    rows = jnp.concatenate([c[0:1], c_ctx[None, :], jnp.zeros((6, d), F32)], axis=0)
    mods = _ada_mods(rows, ada_w, ada_b)
    cos_l, sa_l, sb_l = _rope_tables(s)
    cos_c = jnp.ones((n_ctx, HEAD_DIM), F32)
    zer_c = jnp.zeros((n_ctx, HEAD_DIM), F32)
    log_g_all = jax.nn.log_sigmoid(ret_decay.astype(F32))

    for i in range(depth):
        last = i == depth - 1
        w_in_i = w_in[i].astype(BF16)
        w_out_i = w_out[i].astype(BF16)
        g1 = norm1_g[i].reshape(1, d)
        g2 = norm2_g[i].reshape(1, d)
        log_g = log_g_all[i]
        gain = jnp.concatenate([
            jnp.tile(q_norm_g[i].astype(F32) * ATT_SCALE, N_ATT_HEADS),
            jnp.tile(k_norm_g[i].astype(F32), N_KV_HEADS),
            jnp.ones((KV_W + RET_W,), F32),
            jnp.full((RET_W,), ATT_SCALE, F32),
            jnp.ones((2 * RET_W,), F32)]).reshape(1, IN_COLS)

        p_lat = _in_projection(x_lat, mods, i, 0, g1, w_in_i, gain, cos_l, sa_l, sb_l, tm_lat)
        p_ctx = _in_projection(x_ctx, mods, i, 1, g1, w_in_i, gain, cos_c, zer_c, zer_c, n_ctx)

        att_c, ret_c, s_f, s_b = _ctx_mixer(p_ctx, attn_sink[i], log_g, ret_gn_g[i])
        att_l = _window_attention(p_lat, p_ctx, attn_sink[i], tq)
        ret_l = _retention(p_lat, log_g, s_f, s_b, ret_gn_g[i])
        x_lat = _out_projection(x_lat, att_l, ret_l, w_out_i, mods, i, 0, tm_ffn)

        j = i // 2
        if i % 2 == 0:
            w1 = ffn_w1[j].astype(BF16)
            w3 = ffn_w3[j].astype(BF16)
            w2 = ffn_w2[j].astype(BF16)
            x_lat = _dense_ffn(x_lat, mods, i, 0, g2, w1, w3, w2, tm_ffn, tf)
        else:
            x_lat = _moe_layer(x_lat, mods, i, 0, g2, router_w[j], router_b[j],
                               moe_w1[j], moe_w3[j], moe_w2[j], tm_ffn, min(tm_moe, s), tf_moe)

        if not last:
            x_ctx = _out_projection(x_ctx, att_c, ret_c, w_out_i, mods, i, 1, n_ctx)
            if i % 2 == 0:
                x_ctx = _dense_ffn(x_ctx, mods, i, 1, g2, w1, w3, w2, n_ctx, tf)
            else:
                x_ctx = _moe_layer(x_ctx, mods, i, 1, g2, router_w[j], router_b[j],
                                   moe_w1[j], moe_w3[j], moe_w2[j], n_ctx, n_ctx, tf_moe)
    return x_lat[None]
```

```python
import functools

import jax
import jax.numpy as jnp
from jax import lax
from jax.experimental import pallas as pl
from jax.experimental.pallas import tpu as pltpu

F32 = jnp.float32
BF16 = jnp.bfloat16

HEAD_DIM = 128
N_ATT_HEADS = 8
N_KV_HEADS = 2
N_RET_HEADS = 8
GQA_GROUP = N_ATT_HEADS // N_KV_HEADS
ATT_W = N_ATT_HEADS * HEAD_DIM
KV_W = N_KV_HEADS * HEAD_DIM
RET_W = N_RET_HEADS * HEAD_DIM
IN_COLS = ATT_W + 2 * KV_W + 4 * RET_W
WINDOW = 128
RET_CHUNK = 128
GRID_W = 64
ROPE_THETA = 10000.0
ROPE_PAIRS = HEAD_DIM // 4
N_EXPERTS = 8
EPS = 1e-6
ATT_SCALE = HEAD_DIM ** -0.5

COL_Q = 0
COL_K = ATT_W // HEAD_DIM
COL_V = COL_K + N_KV_HEADS
COL_RQ = COL_V + N_KV_HEADS
COL_RK = COL_RQ + N_RET_HEADS
COL_RV = COL_RK + N_RET_HEADS
COL_G = COL_RV + N_RET_HEADS

VMEM_LIMIT_BYTES = 56 * 1024 * 1024
LANES = 128

NEG_INF = float("-inf")


def _cparams(*sem):
    return pltpu.CompilerParams(dimension_semantics=sem, vmem_limit_bytes=VMEM_LIMIT_BYTES)


def _silu(x):
    return x * (1.0 / (1.0 + jnp.exp(-x)))


def _dot(a, b):
    return jnp.dot(a, b, preferred_element_type=F32)


def _dot_nt(a, b):
    return lax.dot_general(a, b, (((1,), (1,)), ((), ())), preferred_element_type=F32)


def _dot_tn(a, b):
    return lax.dot_general(a, b, (((0,), (0,)), ((), ())), preferred_element_type=F32)


def _norm_modulate(x, g, shift, scale):
    ms = jnp.mean(x * x, axis=-1, keepdims=True)
    return (x * lax.rsqrt(ms + EPS) * g) * (1.0 + scale) + shift


def _ada_kernel(r_ref, w_ref, b_ref, o_ref):
    s = _silu(r_ref[...])
    o_ref[...] = jnp.dot(s, w_ref[...], preferred_element_type=F32,
                         precision=lax.Precision.HIGHEST) + b_ref[...]


def _ada_mods(rows, ada_w, ada_b):
    depth, d, n = ada_w.shape
    tn = d // 2
    return pl.pallas_call(
        _ada_kernel,
        grid=(depth, n // tn),
        in_specs=[
            pl.BlockSpec((8, d), lambda l, j: (0, 0)),
            pl.BlockSpec((None, d, tn), lambda l, j: (l, 0, j)),
            pl.BlockSpec((None, 1, tn), lambda l, j: (l, 0, j)),
        ],
        out_specs=pl.BlockSpec((None, 8, tn), lambda l, j: (l, 0, j)),
        out_shape=jax.ShapeDtypeStruct((depth, 8, n), F32),
        compiler_params=_cparams("arbitrary", "arbitrary"),
        name="ada_mods",
    )(rows, ada_w, ada_b.reshape(depth, 1, n))


def _rope(y, cos, sa, sb):
    return y * cos + pltpu.roll(y, 96, 1) * sa + pltpu.roll(y, 32, 1) * sb


def _inproj_kernel(x_ref, sh_ref, sc_ref, g1_ref, w_ref, gain_ref, cos_ref, sa_ref, sb_ref, o_ref,
                   h_scr, p_scr, *, row, tn, n_j, n_steps):
    t = pl.program_id(0)
    j = t % n_j
    nchunk = tn // HEAD_DIM

    @pl.when(t == 0)
    def _():
        p_scr[...] = jnp.zeros_like(p_scr)

    @pl.when((j == 0) & (t < n_steps))
    def _():
        h = _norm_modulate(x_ref[...], g1_ref[...], sh_ref[row:row + 1, :], sc_ref[row:row + 1, :])
        h_scr[...] = h.astype(BF16)

    p_new = _dot(h_scr[...], w_ref[...])

    jp = (t + n_j - 1) % n_j
    tabs = None
    for c in range(nchunk):
        gc = jp * nchunk + c
        norm_on = gc < COL_V
        rope_on = norm_on | ((gc >= COL_RQ) & (gc < COL_RV))
        if c % N_KV_HEADS == 0:
            tabs = (jnp.where(rope_on, cos_ref[...], 1.0), jnp.where(rope_on, sa_ref[...], 0.0),
                    jnp.where(rope_on, sb_ref[...], 0.0))
        cols = slice(c * HEAD_DIM, (c + 1) * HEAD_DIM)
        pc = p_scr[:, cols]
        r = lax.rsqrt(jnp.mean(pc * pc, axis=-1, keepdims=True) + EPS)
        y = pc * jnp.where(norm_on, r, 1.0) * gain_ref[:, cols]
        o_ref[:, cols] = _rope(y, *tabs).astype(o_ref.dtype)
    p_scr[...] = p_new


def _in_projection(x, mods, layer, row, g1, w, gain, cos, sa, sb, tm):
    m, d = x.shape
    n = w.shape[1]
    tn = 2 * KV_W
    assert n == IN_COLS and ATT_W % tn == 0 and RET_W % tn == 0 and m % tm == 0
    n_i, n_j = m // tm, n // tn
    n_steps = n_i * n_j
    prev = lambda t: jnp.maximum(t - 1, 0)
    tab = pl.BlockSpec((tm, HEAD_DIM), lambda t: (prev(t) // n_j, 0))
    return pl.pallas_call(
        functools.partial(_inproj_kernel, row=row, tn=tn, n_j=n_j, n_steps=n_steps),
        grid=(n_steps + 1,),
        in_specs=[
            pl.BlockSpec((tm, d), lambda t: (jnp.minimum(t // n_j, n_i - 1), 0)),
            pl.BlockSpec((None, 8, d), lambda t: (layer, 0, 0)),
            pl.BlockSpec((None, 8, d), lambda t: (layer, 0, 1)),
            pl.BlockSpec((1, d), lambda t: (0, 0)),
            pl.BlockSpec((d, tn), lambda t: (0, t % n_j)),
            pl.BlockSpec((1, tn), lambda t: (0, prev(t) % n_j)),
            tab, tab, tab,
        ],
        out_specs=pl.BlockSpec((tm, tn), lambda t: (prev(t) // n_j, prev(t) % n_j)),
        out_shape=jax.ShapeDtypeStruct((m, n), BF16),
        scratch_shapes=[pltpu.VMEM((tm, d), BF16), pltpu.VMEM((tm, tn), F32)],
        compiler_params=_cparams("arbitrary"),
        name="in_projection",
    )(x, mods, mods, g1, w, gain, cos, sa, sb)


def _attn_kernel(sink_ref, q_ref, km_ref, vm_ref, kp_ref, vp_ref, kn_ref, vn_ref,
                 kc_ref, vc_ref, bias_ref, o_ref, kbuf, vbuf, vcbuf, *, tq):
    i = pl.program_id(0)
    last = pl.num_programs(0) - 1
    nsub = tq // WINDOW
    n_ctx_blk = kc_ref.shape[0] // HEAD_DIM
    kbuf[0:WINDOW] = kp_ref[...]
    kbuf[WINDOW:WINDOW + tq] = km_ref[...]
    kbuf[WINDOW + tq:] = kn_ref[...]
    for h in range(N_KV_HEADS):
        hs = slice(h * HEAD_DIM, (h + 1) * HEAD_DIM)
        va = slice(2 * h * HEAD_DIM, (2 * h + 1) * HEAD_DIM)
        vbuf[0:WINDOW, va] = vp_ref[:, hs]
        vbuf[WINDOW:WINDOW + tq, va] = vm_ref[:, hs]
        vbuf[WINDOW + tq:, va] = vn_ref[:, hs]

    @pl.when(i == 0)
    def _():
        for h in range(N_KV_HEADS):
            ones = slice((2 * h + 1) * HEAD_DIM, (2 * h + 2) * HEAD_DIM)
            vbuf[:, ones] = jnp.ones((vbuf.shape[0], HEAD_DIM), vbuf.dtype)
            vcbuf[:, ones] = jnp.ones((vcbuf.shape[0], HEAD_DIM), vcbuf.dtype)
            vcbuf[:, 2 * h * HEAD_DIM:(2 * h + 1) * HEAD_DIM] = vc_ref[:, h * HEAD_DIM:(h + 1) * HEAD_DIM]

    col = lax.broadcasted_iota(jnp.int32, (WINDOW, 3 * WINDOW), 1)
    for s in range(nsub):
        bias = bias_ref[...]
        if s == 0:
            bias = bias + jnp.where(col < WINDOW, jnp.where(i == 0, NEG_INF, 0.0), 0.0)
        if s == nsub - 1:
            bias = bias + jnp.where(col >= 2 * WINDOW, jnp.where(i == last, NEG_INF, 0.0), 0.0)
        bias4 = jnp.concatenate([bias] * GQA_GROUP, axis=0)
        for h in range(N_KV_HEADS):
            hs = slice(h * HEAD_DIM, (h + 1) * HEAD_DIM)
            q = jnp.concatenate(
                [q_ref[s * WINDOW:(s + 1) * WINDOW,
                       (h * GQA_GROUP + g) * HEAD_DIM:(h * GQA_GROUP + g + 1) * HEAD_DIM]
                 for g in range(GQA_GROUP)], axis=0)
            vs = slice(2 * h * HEAD_DIM, (2 * h + 2) * HEAD_DIM)
            kl = kbuf[s * WINDOW:(s + 3) * WINDOW, hs]
            vl = vbuf[s * WINDOW:(s + 3) * WINDOW, vs]
            s_loc = _dot_nt(q, kl) + bias4
            s_ctx = _dot_nt(q, kc_ref[:, hs])
            sink = jnp.concatenate(
                [jnp.full((WINDOW, 1), sink_ref[h * GQA_GROUP + g], F32) for g in range(GQA_GROUP)],
                axis=0)
            blocks = ([s_loc[:, b * HEAD_DIM:(b + 1) * HEAD_DIM] for b in range(3)]
                      + [s_ctx[:, b * HEAD_DIM:(b + 1) * HEAD_DIM] for b in range(n_ctx_blk)])
            m = jnp.maximum(jnp.max(functools.reduce(jnp.maximum, blocks), axis=-1, keepdims=True), sink)
            p_loc = jnp.exp(s_loc - m).astype(BF16)
            p_ctx = jnp.exp(s_ctx - m).astype(BF16)
            o_den = _dot(p_loc, vl) + _dot(p_ctx, vcbuf[:, vs])
            o = o_den[:, :HEAD_DIM] / (o_den[:, HEAD_DIM:] + jnp.exp(sink - m))
            for g in range(GQA_GROUP):
                o_ref[s * WINDOW:(s + 1) * WINDOW,
                      (h * GQA_GROUP + g) * HEAD_DIM:(h * GQA_GROUP + g + 1) * HEAD_DIM] = (
                    o[g * WINDOW:(g + 1) * WINDOW].astype(o_ref.dtype))


def _band_bias():
    r = jnp.arange(WINDOW)[:, None]
    c = jnp.arange(3 * WINDOW)[None, :]
    ok = (c - r >= 0) & (c - r <= 2 * WINDOW)
    return jnp.where(ok, 0.0, NEG_INF).astype(F32)


def _window_attention(p_lat, p_ctx, sink, tq):
    s = p_lat.shape[0]
    c = p_ctx.shape[0]
    r = tq // WINDOW
    nb = s // WINDOW
    kvb = KV_W // HEAD_DIM
    ck, cv = COL_K // kvb, COL_V // kvb
    return pl.pallas_call(
        functools.partial(_attn_kernel, tq=tq),
        grid=(s // tq,),
        in_specs=[
            pl.BlockSpec(memory_space=pltpu.SMEM),
            pl.BlockSpec((tq, ATT_W), lambda i: (i, 0)),
            pl.BlockSpec((tq, KV_W), lambda i: (i, ck)),
            pl.BlockSpec((tq, KV_W), lambda i: (i, cv)),
            pl.BlockSpec((WINDOW, KV_W), lambda i: (jnp.maximum(i * r - 1, 0), ck)),
            pl.BlockSpec((WINDOW, KV_W), lambda i: (jnp.maximum(i * r - 1, 0), cv)),
            pl.BlockSpec((WINDOW, KV_W), lambda i: (jnp.minimum((i + 1) * r, nb - 1), ck)),
            pl.BlockSpec((WINDOW, KV_W), lambda i: (jnp.minimum((i + 1) * r, nb - 1), cv)),
            pl.BlockSpec((c, KV_W), lambda i: (0, ck)),
            pl.BlockSpec((c, KV_W), lambda i: (0, cv)),
            pl.BlockSpec((WINDOW, 3 * WINDOW), lambda i: (0, 0)),
        ],
        out_specs=pl.BlockSpec((tq, ATT_W), lambda i: (i, 0)),
        out_shape=jax.ShapeDtypeStruct((s, ATT_W), BF16),
        scratch_shapes=[pltpu.VMEM((tq + 2 * WINDOW, KV_W), BF16),
                        pltpu.VMEM((tq + 2 * WINDOW, 2 * KV_W), BF16),
                        pltpu.VMEM((c, 2 * KV_W), BF16)],
        compiler_params=_cparams("arbitrary"),
        name="window_attention",
    )(sink, p_lat, p_lat, p_lat, p_lat, p_lat, p_lat, p_lat, p_ctx, p_ctx, _band_bias())


def _ctx_mixer_kernel(sink_ref, lg_ref, q_ref, k_ref, v_ref, rq_ref, rk_ref, rv_ref, g_ref, gn_ref,
                      att_ref, ret_ref, sf_ref, sb_ref):
    h = pl.program_id(0)
    c = q_ref.shape[0]
    lg_f = lg_ref[0, h]
    lg_b = lg_ref[1, h]
    s = _dot_nt(q_ref[...], k_ref[...])
    sink = jnp.full((c, 1), sink_ref[h], F32)
    m = jnp.maximum(jnp.max(s, axis=-1, keepdims=True), sink)
    p = jnp.exp(s - m)
    den = jnp.sum(p, axis=-1, keepdims=True) + jnp.exp(sink - m)
    att_ref[...] = (_dot(p.astype(BF16), v_ref[...]) / den).astype(att_ref.dtype)
    n_i = lax.broadcasted_iota(jnp.int32, (c, c), 0)
    m_i = lax.broadcasted_iota(jnp.int32, (c, c), 1)
    rel = (n_i - m_i).astype(F32)
    dec = (jnp.where(rel >= 0, jnp.exp(jnp.maximum(rel, 0.0) * lg_f), 0.0)
           + jnp.where(rel <= 0, jnp.exp(jnp.maximum(-rel, 0.0) * lg_b), 0.0))
    sc = _dot_nt(rq_ref[...], rk_ref[...]) * dec
    y = _dot(sc.astype(BF16), rv_ref[...])
    mu = jnp.mean(y, axis=-1, keepdims=True)
    var = jnp.mean(jnp.square(y - mu), axis=-1, keepdims=True)
    yn = (y - mu) * lax.rsqrt(var + EPS) * gn_ref[...]
    ret_ref[...] = (_silu(g_ref[...].astype(F32)) * yn).astype(ret_ref.dtype)
    pos = lax.broadcasted_iota(jnp.int32, (c, HEAD_DIM), 0).astype(F32)
    kf = rk_ref[...].astype(F32)
    k_f = (kf * jnp.exp((c - 1.0 - pos) * lg_f)).astype(BF16)
    k_b = (kf * jnp.exp(pos * lg_b)).astype(BF16)
    sf_ref[...] = _dot_tn(k_f, rv_ref[...])
    sb_ref[...] = _dot_tn(k_b, rv_ref[...])


def _ctx_mixer(p_ctx, sink, log_g, gn_g):
    c = p_ctx.shape[0]
    hb = lambda off: pl.BlockSpec((c, HEAD_DIM), lambda h: (0, off + h))
    return pl.pallas_call(
        _ctx_mixer_kernel,
        grid=(N_RET_HEADS,),
        in_specs=[
            pl.BlockSpec(memory_space=pltpu.SMEM),
            pl.BlockSpec(memory_space=pltpu.SMEM),
            hb(COL_Q),
            pl.BlockSpec((c, HEAD_DIM), lambda h: (0, COL_K + h // GQA_GROUP)),
            pl.BlockSpec((c, HEAD_DIM), lambda h: (0, COL_V + h // GQA_GROUP)),
            hb(COL_RQ), hb(COL_RK), hb(COL_RV), hb(COL_G),
            pl.BlockSpec((None, 1, HEAD_DIM), lambda h: (h, 0, 0)),
        ],
        out_specs=[
            pl.BlockSpec((c, HEAD_DIM), lambda h: (0, h)),
            pl.BlockSpec((c, HEAD_DIM), lambda h: (0, h)),
            pl.BlockSpec((None, HEAD_DIM, HEAD_DIM), lambda h: (h, 0, 0)),
            pl.BlockSpec((None, HEAD_DIM, HEAD_DIM), lambda h: (h, 0, 0)),
        ],
        out_shape=[
            jax.ShapeDtypeStruct((c, ATT_W), BF16),
            jax.ShapeDtypeStruct((c, RET_W), BF16),
            jax.ShapeDtypeStruct((N_RET_HEADS, HEAD_DIM, HEAD_DIM), F32),
            jax.ShapeDtypeStruct((N_RET_HEADS, HEAD_DIM, HEAD_DIM), F32),
        ],
        compiler_params=_cparams("arbitrary"),
        name="ctx_mixer",
    )(sink, log_g, p_ctx, p_ctx, p_ctx, p_ctx, p_ctx, p_ctx, p_ctx,
      gn_g.reshape(N_RET_HEADS, 1, HEAD_DIM))


RET_KERNEL_CHUNK = 256
RET_UNROLL = 4


def _ret_kernel(lg_ref, q_ref, k_ref, v_ref, g_ref, sf_ref, sb_ref, gn_ref, o_ref,
                y_scr, dmat, qdf, kdf, qdb, kdb, cdf, cdb):
    h = pl.program_id(0)
    ck = RET_KERNEL_CHUNK
    nit = q_ref.shape[0] // (ck * RET_UNROLL)
    lg_f = lg_ref[0, h]
    lg_b = lg_ref[1, h]
    ii = lax.broadcasted_iota(jnp.int32, (ck, ck), 0)
    jj = lax.broadcasted_iota(jnp.int32, (ck, ck), 1)
    rel = (ii - jj).astype(F32)
    dmat[...] = (jnp.where(rel >= 0, jnp.exp(jnp.maximum(rel, 0.0) * lg_f), 0.0)
                 + jnp.where(rel <= 0, jnp.exp(jnp.maximum(-rel, 0.0) * lg_b), 0.0))
    pos = lax.broadcasted_iota(jnp.int32, (ck, HEAD_DIM), 0).astype(F32)
    qdf[...] = jnp.exp((pos + 1.0) * lg_f)
    kdf[...] = jnp.exp((ck - 1.0 - pos) * lg_f)
    qdb[...] = jnp.exp((ck - pos) * lg_b)
    kdb[...] = jnp.exp(pos * lg_b)
    full = jnp.full((HEAD_DIM, HEAD_DIM), float(ck), F32)
    cdf[...] = jnp.exp(full * lg_f)
    cdb[...] = jnp.exp(full * lg_b)

    def rows_of(z):
        return pl.ds(pl.multiple_of(z * ck, ck), ck)

    def fwd(it, state):
        zs = [it * RET_UNROLL + u for u in range(RET_UNROLL)]
        qs = [q_ref[rows_of(z), :] for z in zs]
        ks = [k_ref[rows_of(z), :] for z in zs]
        vs = [v_ref[rows_of(z), :] for z in zs]
        inner = [_dot((_dot_nt(q, k) * dmat[...]).astype(BF16), v) for q, k, v in zip(qs, ks, vs)]
        kvs = [_dot_tn((k.astype(F32) * kdf[...]).astype(BF16), v) for k, v in zip(ks, vs)]
        for u, z in enumerate(zs):
            y_scr[rows_of(z), :] = inner[u] + _dot(qs[u], state.astype(BF16)) * qdf[...]
            state = cdf[...] * state + kvs[u]
        return state

    lax.fori_loop(0, nit, fwd, sf_ref[...])

    def bwd(it, state):
        zs = [nit * RET_UNROLL - 1 - (it * RET_UNROLL + u) for u in range(RET_UNROLL)]
        qs = [q_ref[rows_of(z), :] for z in zs]
        kvs = [_dot_tn((k_ref[rows_of(z), :].astype(F32) * kdb[...]).astype(BF16), v_ref[rows_of(z), :])
               for z in zs]
        for u, z in enumerate(zs):
            y = y_scr[rows_of(z), :] + _dot(qs[u], state.astype(BF16)) * qdb[...]
            state = cdb[...] * state + kvs[u]
            mu = jnp.mean(y, axis=-1, keepdims=True)
            var = jnp.mean(jnp.square(y - mu), axis=-1, keepdims=True)
            yn = (y - mu) * lax.rsqrt(var + EPS) * gn_ref[...]
            o_ref[rows_of(z), :] = (_silu(g_ref[rows_of(z), :].astype(F32)) * yn).astype(o_ref.dtype)
        return state

    lax.fori_loop(0, nit, bwd, sb_ref[...])


def _retention(p_lat, log_g, s_f, s_b, gn_g):
    s = p_lat.shape[0]
    ck = RET_KERNEL_CHUNK
    assert s % (ck * RET_UNROLL) == 0
    col = lambda off: pl.BlockSpec((s, HEAD_DIM), lambda h: (0, off + h))
    state_spec = pl.BlockSpec((None, HEAD_DIM, HEAD_DIM), lambda h: (h, 0, 0))
    vec = pltpu.VMEM((ck, HEAD_DIM), F32)
    sq = pltpu.VMEM((HEAD_DIM, HEAD_DIM), F32)
    return pl.pallas_call(
        _ret_kernel,
        grid=(N_RET_HEADS,),
        in_specs=[pl.BlockSpec(memory_space=pltpu.SMEM),
                  col(COL_RQ), col(COL_RK), col(COL_RV), col(COL_G), state_spec, state_spec,
                  pl.BlockSpec((None, 1, HEAD_DIM), lambda h: (h, 0, 0))],
        out_specs=col(0),
        out_shape=jax.ShapeDtypeStruct((s, RET_W), BF16),
        scratch_shapes=[pltpu.VMEM((s, HEAD_DIM), F32), pltpu.VMEM((ck, ck), F32),
                        vec, vec, vec, vec, sq, sq],
        compiler_params=_cparams("arbitrary"),
        name="retention",
    )(log_g, p_lat, p_lat, p_lat, p_lat, s_f, s_b, gn_g.reshape(N_RET_HEADS, 1, HEAD_DIM))


def _outproj_kernel(x_ref, a_ref, r_ref, wa_ref, wr_ref, gate_ref, o_ref, *, row):
    y = _dot(a_ref[...], wa_ref[...]) + _dot(r_ref[...], wr_ref[...])
    o_ref[...] = x_ref[...] + gate_ref[row:row + 1, :] * y


def _out_projection(x, att, ret, w_out, mods, layer, row, tm):
    m, d = x.shape
    return pl.pallas_call(
        functools.partial(_outproj_kernel, row=row),
        grid=(m // tm,),
        in_specs=[
            pl.BlockSpec((tm, d), lambda i: (i, 0)),
            pl.BlockSpec((tm, ATT_W), lambda i: (i, 0)),
            pl.BlockSpec((tm, RET_W), lambda i: (i, 0)),
            pl.BlockSpec((ATT_W, d), lambda i: (0, 0)),
            pl.BlockSpec((RET_W, d), lambda i: (ATT_W // RET_W, 0)),
            pl.BlockSpec((None, 8, d), lambda i: (layer, 0, 2)),
        ],
        out_specs=pl.BlockSpec((tm, d), lambda i: (i, 0)),
        out_shape=jax.ShapeDtypeStruct((m, d), F32),
        compiler_params=_cparams("arbitrary"),
        name="out_projection",
    )(x, att, ret, w_out, w_out, mods)


def _swiglu_accumulate(h_scr, w1_ref, w3_ref, w2_ref, o_ref, row_starts, n_rows):
    w1 = w1_ref[...].astype(BF16)
    w3 = w3_ref[...].astype(BF16)
    w2 = w2_ref[...].astype(BF16)
    us = []
    for lo in row_starts:
        h = h_scr[lo:lo + n_rows, :]
        us.append((_silu(_dot(h, w1)) * _dot(h, w3)).astype(BF16))
    for lo, u in zip(row_starts, us):
        o_ref[lo:lo + n_rows, :] += _dot(u, w2)


def _ffn_kernel(x_ref, sh_ref, sc_ref, gate_ref, g2_ref, w1_ref, w3_ref, w2_ref, o_ref, h_scr,
                *, row, n_parts):
    j = pl.program_id(1)
    part = x_ref.shape[0] // n_parts

    @pl.when(j == 0)
    def _():
        h = _norm_modulate(x_ref[...], g2_ref[...], sh_ref[row:row + 1, :], sc_ref[row:row + 1, :])
        h_scr[...] = h.astype(BF16)
        o_ref[...] = jnp.zeros_like(o_ref)

    _swiglu_accumulate(h_scr, w1_ref, w3_ref, w2_ref, o_ref, tuple(p * part for p in range(n_parts)), part)

    @pl.when(j == pl.num_programs(1) - 1)
    def _():
        o_ref[...] = x_ref[...] + gate_ref[row:row + 1, :] * o_ref[...]


def _dense_ffn(x, mods, layer, row, g2, w1, w3, w2, tm, tf):
    m, d = x.shape
    f = w1.shape[1]
    mod = lambda k: pl.BlockSpec((None, 8, d), lambda i, j: (layer, 0, k))
    return pl.pallas_call(
        functools.partial(_ffn_kernel, row=row, n_parts=2 if tm >= 1024 else 1),
        grid=(m // tm, f // tf),
        in_specs=[
            pl.BlockSpec((tm, d), lambda i, j: (i, 0), pipeline_mode=pl.Buffered(1)),
            mod(3), mod(4), mod(5),
            pl.BlockSpec((1, d), lambda i, j: (0, 0)),
            pl.BlockSpec((d, tf), lambda i, j: (0, j)),
            pl.BlockSpec((d, tf), lambda i, j: (0, j)),
            pl.BlockSpec((tf, d), lambda i, j: (j, 0)),
        ],
        out_specs=pl.BlockSpec((tm, d), lambda i, j: (i, 0)),
        out_shape=jax.ShapeDtypeStruct((m, d), F32),
        scratch_shapes=[pltpu.VMEM((tm, d), BF16)],
        compiler_params=_cparams("arbitrary", "arbitrary"),
        name="dense_ffn",
    )(x, mods, mods, mods, g2, w1, w3, w2)


def _router_kernel(x_ref, sh_ref, sc_ref, g2_ref, rw_ref, rb_ref, h_ref, info_ref, cnt_ref, carry, *, row):
    i = pl.program_id(0)
    tm = x_ref.shape[0]

    @pl.when(i == 0)
    def _():
        carry[...] = jnp.zeros_like(carry)

    h = _norm_modulate(x_ref[...], g2_ref[...], sh_ref[row:row + 1, :], sc_ref[row:row + 1, :])
    h_ref[...] = h
    logits = jnp.dot(h, rw_ref[...], preferred_element_type=F32,
                     precision=lax.Precision.HIGHEST) + rb_ref[...]
    lane = lax.broadcasted_iota(jnp.int32, (tm, LANES), 1).astype(F32)
    logits = jnp.where(lane < N_EXPERTS, logits, NEG_INF)
    v1 = jnp.max(logits, axis=-1, keepdims=True)
    e1 = jnp.min(jnp.where(logits == v1, lane, float(LANES)), axis=-1, keepdims=True)
    rest = jnp.where(lane == e1, NEG_INF, logits)
    v2 = jnp.max(rest, axis=-1, keepdims=True)
    e2 = jnp.min(jnp.where(rest == v2, lane, float(LANES)), axis=-1, keepdims=True)
    t = jnp.exp(v2 - v1)
    w1 = 1.0 / (1.0 + t)
    w2 = t / (1.0 + t)
    oh1 = jnp.where(lane == e1, 1.0, 0.0)
    oh2 = jnp.where(lane == e2, 1.0, 0.0)
    oh = oh1 + oh2
    r_i = lax.broadcasted_iota(jnp.int32, (tm, tm), 0)
    c_i = lax.broadcasted_iota(jnp.int32, (tm, tm), 1)
    tri = jnp.where(c_i < r_i, 1.0, 0.0).astype(BF16)
    before = _dot(tri, oh.astype(BF16)) + carry[0:1, :]
    rank1 = jnp.sum(before * oh1, axis=-1, keepdims=True)
    rank2 = jnp.sum(before * oh2, axis=-1, keepdims=True)
    carry[...] = carry[...] + jnp.sum(oh, axis=0, keepdims=True)
    info = jnp.where(lane == 0.0, e1, 0.0)
    info = jnp.where(lane == 1.0, e2, info)
    info = jnp.where(lane == 2.0, rank1, info)
    info = jnp.where(lane == 3.0, rank2, info)
    info = jnp.where(lane == 4.0, w1, info)
    info = jnp.where(lane == 5.0, w2, info)
    info_ref[...] = info
    cnt_ref[...] = carry[...]


def _router(x, mods, layer, row, g2, rw_pad, rb_pad, tm):
    m, d = x.shape
    mod = lambda k: pl.BlockSpec((None, 8, d), lambda i: (layer, 0, k))
    return pl.pallas_call(
        functools.partial(_router_kernel, row=row),
        grid=(m // tm,),
        in_specs=[
            pl.BlockSpec((tm, d), lambda i: (i, 0)),
            mod(3), mod(4),
            pl.BlockSpec((1, d), lambda i: (0, 0)),
            pl.BlockSpec((d, LANES), lambda i: (0, 0)),
            pl.BlockSpec((1, LANES), lambda i: (0, 0)),
        ],
        out_specs=[
            pl.BlockSpec((tm, d), lambda i: (i, 0)),
            pl.BlockSpec((tm, LANES), lambda i: (i, 0)),
            pl.BlockSpec((8, LANES), lambda i: (0, 0)),
        ],
        out_shape=[
            jax.ShapeDtypeStruct((m, d), F32),
            jax.ShapeDtypeStruct((m, LANES), F32),
            jax.ShapeDtypeStruct((8, LANES), F32),
        ],
        scratch_shapes=[pltpu.VMEM((8, LANES), F32)],
        compiler_params=_cparams("arbitrary"),
        name="moe_router",
    )(x, mods, mods, g2, rw_pad, rb_pad)


GATHER_UNROLL = 8


def _gather_rows(idx_ref, src_hbm, dst, sem, n):
    def body(kb, carry):
        for u in range(GATHER_UNROLL):
            k = kb * GATHER_UNROLL + u
            pltpu.make_async_copy(src_hbm.at[pl.ds(idx_ref[0, 0, k], 1)], dst.at[pl.ds(k, 1)], sem).start()
        return carry
    lax.fori_loop(0, n // GATHER_UNROLL, body, 0)


def _wait_rows(src_hbm, dst, sem, n):
    pltpu.make_async_copy(src_hbm.at[pl.ds(0, n)], dst, sem).wait()


def _issue_rows(idx_ref, src_hbm, dst, sem, start, count):
    for u in range(count):
        k = start + u
        pltpu.make_async_copy(src_hbm.at[pl.ds(idx_ref[0, 0, k], 1)], dst.at[pl.ds(k, 1)], sem).start()


def _moe_ffn_kernel(te_ref, nu_ref, nv_ref, cur_ref, nxt_ref, h_hbm, w1_ref, w3_ref, w2_ref, o_ref,
                    buf, sem, h_scr, *, tm, n_f):
    r = pl.program_id(0)
    j = pl.program_id(1)
    n_used = nu_ref[0]
    half = tm // 2
    per_step = tm // n_f
    head = tm - per_step * n_f

    @pl.when(r < n_used)
    def _():
        @pl.when(j == 0)
        def _():
            @pl.when(r == 0)
            def _():
                _gather_rows(cur_ref, h_hbm, buf, sem, tm)

            _wait_rows(h_hbm, buf, sem, tm)
            h_scr[...] = buf[...].astype(BF16)

            @pl.when(r + 1 < n_used)
            def _():
                _issue_rows(nxt_ref, h_hbm, buf, sem, 0, head)

            o_ref[...] = jnp.zeros_like(o_ref)

        more = r + 1 < n_used
        full = nv_ref[r] > half

        def case(fetch, los):
            if fetch:
                _issue_rows(nxt_ref, h_hbm, buf, sem, head + j * per_step, per_step)
            _swiglu_accumulate(h_scr, w1_ref, w3_ref, w2_ref, o_ref, los, half)

        pl.when(more & full)(lambda: case(True, (0, half)))
        pl.when(more & jnp.logical_not(full))(lambda: case(True, (0,)))
        pl.when(jnp.logical_not(more) & full)(lambda: case(False, (0, half)))
        pl.when(jnp.logical_not(more) & jnp.logical_not(full))(lambda: case(False, (0,)))

    @pl.when((r >= n_used) & (j == 0))
    def _():
        o_ref[...] = jnp.zeros_like(o_ref)


def _moe_ffn(h2, tile_expert, n_used, n_valid, src_rows, w1, w3, w2, tm, tf):
    d = h2.shape[1]
    f = w1.shape[2]
    n_tiles = src_rows.shape[0]
    n_f = f // tf
    clamp = lambda r, nu: jnp.maximum(jnp.minimum(r, nu[0] - 1), 0)
    fcol = lambda r, j, nu: jnp.where(r < nu[0], j, n_f - 1)
    grid_spec = pltpu.PrefetchScalarGridSpec(
        num_scalar_prefetch=3,
        grid=(n_tiles, n_f),
        in_specs=[
            pl.BlockSpec((1, 1, tm), lambda r, j, te, nu, nv: (clamp(r, nu), 0, 0), memory_space=pltpu.SMEM),
            pl.BlockSpec((1, 1, tm), lambda r, j, te, nu, nv: (clamp(r + 1, nu), 0, 0), memory_space=pltpu.SMEM),
            pl.BlockSpec(memory_space=pl.ANY),
            pl.BlockSpec((None, d, tf), lambda r, j, te, nu, nv: (te[clamp(r, nu)], 0, fcol(r, j, nu))),
            pl.BlockSpec((None, d, tf), lambda r, j, te, nu, nv: (te[clamp(r, nu)], 0, fcol(r, j, nu))),
            pl.BlockSpec((None, tf, d), lambda r, j, te, nu, nv: (te[clamp(r, nu)], fcol(r, j, nu), 0)),
        ],
        out_specs=pl.BlockSpec((tm, d), lambda r, j, te, nu, nv: (r, 0)),
        scratch_shapes=[
            pltpu.VMEM((tm, d), F32),
            pltpu.SemaphoreType.DMA(()),
            pltpu.VMEM((tm, d), BF16),
        ],
    )
    return pl.pallas_call(
        functools.partial(_moe_ffn_kernel, tm=tm, n_f=n_f),
        grid_spec=grid_spec,
        out_shape=jax.ShapeDtypeStruct((n_tiles * tm, d), F32),
        compiler_params=_cparams("arbitrary", "arbitrary"),
        name="moe_ffn",
    )(tile_expert, n_used, n_valid, src_rows, src_rows, h2, w1, w3, w2)


def _combine_kernel(cur_ref, nxt_ref, x_ref, info_ref, gate_ref, y_hbm, o_ref, buf, sems, *, tm, row):
    i = pl.program_id(0)
    slot = i % 2

    @pl.when(i == 0)
    def _():
        _gather_rows(cur_ref, y_hbm, buf.at[0], sems.at[0], 2 * tm)

    _wait_rows(y_hbm, buf.at[slot], sems.at[slot], 2 * tm)

    @pl.when(i + 1 < pl.num_programs(0))
    def _():
        _gather_rows(nxt_ref, y_hbm, buf.at[1 - slot], sems.at[1 - slot], 2 * tm)

    info = info_ref[...]
    w1 = info[:, 4:5]
    w2 = info[:, 5:6]
    y = w1 * buf[slot, 0:tm, :] + w2 * buf[slot, tm:2 * tm, :]
    o_ref[...] = x_ref[...] + gate_ref[row:row + 1, :] * y


def _moe_combine(x, info, mods, layer, row, y_rows, pos_tiles, tm):
    m, d = x.shape
    nt = m // tm
    return pl.pallas_call(
        functools.partial(_combine_kernel, tm=tm, row=row),
        grid=(nt,),
        in_specs=[
            pl.BlockSpec((1, 1, 2 * tm), lambda i: (i, 0, 0), memory_space=pltpu.SMEM),
            pl.BlockSpec((1, 1, 2 * tm), lambda i: (jnp.minimum(i + 1, nt - 1), 0, 0), memory_space=pltpu.SMEM),
            pl.BlockSpec((tm, d), lambda i: (i, 0)),
            pl.BlockSpec((tm, LANES), lambda i: (i, 0)),
            pl.BlockSpec((None, 8, d), lambda i: (layer, 0, 5)),
            pl.BlockSpec(memory_space=pl.ANY),
        ],
        out_specs=pl.BlockSpec((tm, d), lambda i: (i, 0)),
        out_shape=jax.ShapeDtypeStruct((m, d), F32),
        scratch_shapes=[pltpu.VMEM((2, 2 * tm, d), F32), pltpu.SemaphoreType.DMA((2,))],
        compiler_params=_cparams("arbitrary"),
        name="moe_combine",
    )(pos_tiles, pos_tiles, x, info, mods, y_rows)


def _moe_layer(x, mods, layer, row, g2, router_w, router_b, w1, w3, w2, tm_route, tm_exp, tf):
    m, d = x.shape
    rw_pad = jnp.pad(router_w, ((0, 0), (0, LANES - N_EXPERTS)))
    rb_pad = jnp.pad(router_b, (0, LANES - N_EXPERTS)).reshape(1, LANES)
    h2, info, cnt = _router(x, mods, layer, row, g2, rw_pad, rb_pad, tm_route)
    counts = cnt[0, :N_EXPERTS].astype(jnp.int32)
    tiles_per = (counts + tm_exp - 1) // tm_exp
    tile_end = jnp.cumsum(tiles_per)
    group_start = (tile_end - tiles_per) * tm_exp
    n_tiles = (2 * m) // tm_exp + N_EXPERTS
    n_used = tile_end[-1:].astype(jnp.int32)
    tile_ids = jnp.arange(n_tiles, dtype=jnp.int32)
    tile_expert = jnp.minimum(jnp.sum(tile_end[None, :] <= tile_ids[:, None], axis=1),
                              N_EXPERTS - 1).astype(jnp.int32)
    rows_before = (tile_ids - (tile_end - tiles_per)[tile_expert]) * tm_exp
    n_valid = jnp.clip(counts[tile_expert] - rows_before, 0, tm_exp).astype(jnp.int32)
    e12 = info[:, 0:2].astype(jnp.int32)
    rank12 = info[:, 2:4].astype(jnp.int32)
    pos = group_start[e12] + rank12
    tok = jnp.broadcast_to(jnp.arange(m, dtype=jnp.int32)[:, None], (m, 2))
    src_rows = jnp.zeros((n_tiles * tm_exp,), jnp.int32).at[pos.reshape(-1)].set(
        tok.reshape(-1), unique_indices=True, mode="promise_in_bounds")
    y_rows = _moe_ffn(h2, tile_expert, n_used, n_valid, src_rows.reshape(n_tiles, 1, tm_exp),
                      w1, w3, w2, tm_exp, tf)
    nt = m // tm_route
    pos_tiles = pos.reshape(nt, tm_route, 2).transpose(0, 2, 1).reshape(nt, 1, 2 * tm_route)
    return _moe_combine(x, info, mods, layer, row, y_rows, pos_tiles, tm_route)


def _rope_tables(n_tokens):
    n_rows = n_tokens // GRID_W
    inv = ROPE_THETA ** (-jnp.arange(ROPE_PAIRS, dtype=F32) / ROPE_PAIRS)
    ang_r = jnp.arange(n_rows).astype(F32)[:, None] * inv
    ang_c = jnp.arange(GRID_W).astype(F32)[:, None] * inv
    cos_r = jnp.repeat(jnp.cos(ang_r), GRID_W, axis=0)
    sin_r = jnp.repeat(jnp.sin(ang_r), GRID_W, axis=0)
    cos_c = jnp.tile(jnp.cos(ang_c), (n_rows, 1))
    sin_c = jnp.tile(jnp.sin(ang_c), (n_rows, 1))
    zero = jnp.zeros_like(sin_r)
    cos_t = jnp.concatenate([cos_r, cos_r, cos_c, cos_c], axis=-1)
    sa_t = jnp.concatenate([-sin_r, zero, -sin_c, zero], axis=-1)
    sb_t = jnp.concatenate([zero, sin_r, zero, sin_c], axis=-1)
    return cos_t, sa_t, sb_t


def kernel(x, c, ctx, c_ctx, ada_w, ada_b, norm1_g, norm2_g, w_in, q_norm_g, k_norm_g, attn_sink,
           ret_decay, ret_gn_g, w_out, ffn_w1, ffn_w3, ffn_w2, router_w, router_b, moe_w1, moe_w3, moe_w2):
    b, s, d = x.shape
    n_ctx = ctx.shape[1]
    depth = ada_w.shape[0]
    assert b == 1, "one latent sequence per call"
    x_lat = x[0]
    x_ctx = ctx[0]

    tm_lat = min(1024, s)
    tm_ffn = min(512, s)
    tq = min(512, s)
    tm_moe = 1024
    tf_moe = 256

    rows = jnp.concatenate([c[0:1], c_ctx[None, :], jnp.zeros((6, d), F32)], axis=0)
    mods = _ada_mods(rows, ada_w, ada_b)
    cos_l, sa_l, sb_l = _rope_tables(s)
    cos_c = jnp.ones((n_ctx, HEAD_DIM), F32)
    zer_c = jnp.zeros((n_ctx, HEAD_DIM), F32)
    log_g_all = jax.nn.log_sigmoid(ret_decay.astype(F32))

    for i in range(depth):
        last = i == depth - 1
        w_in_i = w_in[i].astype(BF16)
        w_out_i = w_out[i].astype(BF16)
        g1 = norm1_g[i].reshape(1, d)
        g2 = norm2_g[i].reshape(1, d)
        log_g = log_g_all[i]
        gain = jnp.concatenate([
            jnp.tile(q_norm_g[i].astype(F32) * ATT_SCALE, N_ATT_HEADS),
            jnp.tile(k_norm_g[i].astype(F32), N_KV_HEADS),
            jnp.ones((KV_W + RET_W,), F32),
            jnp.full((RET_W,), ATT_SCALE, F32),
            jnp.ones((2 * RET_W,), F32)]).reshape(1, IN_COLS)

        p_lat = _in_projection(x_lat, mods, i, 0, g1, w_in_i, gain, cos_l, sa_l, sb_l, tm_lat)
        p_ctx = _in_projection(x_ctx, mods, i, 1, g1, w_in_i, gain, cos_c, zer_c, zer_c, n_ctx)

        att_c, ret_c, s_f, s_b = _ctx_mixer(p_ctx, attn_sink[i], log_g, ret_gn_g[i])
        att_l = _window_attention(p_lat, p_ctx, attn_sink[i], tq)
        ret_l = _retention(p_lat, log_g, s_f, s_b, ret_gn_g[i])
        x_lat = _out_projection(x_lat, att_l, ret_l, w_out_i, mods, i, 0, tm_ffn)

        j = i // 2
        if i % 2 == 0:
            w1, w3, w2 = ffn_w1[j], ffn_w3[j], ffn_w2[j]
            x_lat = _dense_ffn(x_lat, mods, i, 0, g2, w1, w3, w2, min(tm_moe, s), tf_moe)
        else:
            x_lat = _moe_layer(x_lat, mods, i, 0, g2, router_w[j], router_b[j],
                               moe_w1[j], moe_w3[j], moe_w2[j], tm_ffn, min(tm_moe, s), tf_moe)

        if not last:
            x_ctx = _out_projection(x_ctx, att_c, ret_c, w_out_i, mods, i, 1, n_ctx)
            if i % 2 == 0:
                x_ctx = _dense_ffn(x_ctx, mods, i, 1, g2, w1, w3, w2, n_ctx, tf_moe)
            else:
                x_ctx = _moe_layer(x_ctx, mods, i, 1, g2, router_w[j], router_b[j],
                                   moe_w1[j], moe_w3[j], moe_w2[j], n_ctx, n_ctx, tf_moe)
    return x_lat[None]
```

```python
import functools

import jax
import jax.numpy as jnp
from jax import lax
from jax.experimental import pallas as pl
from jax.experimental.pallas import tpu as pltpu

F32 = jnp.float32
BF16 = jnp.bfloat16

HEAD_DIM = 128
N_ATT_HEADS = 8
N_KV_HEADS = 2
N_RET_HEADS = 8
GQA_GROUP = N_ATT_HEADS // N_KV_HEADS
ATT_W = N_ATT_HEADS * HEAD_DIM
KV_W = N_KV_HEADS * HEAD_DIM
RET_W = N_RET_HEADS * HEAD_DIM
IN_COLS = ATT_W + 2 * KV_W + 4 * RET_W
WINDOW = 128
RET_CHUNK = 128
GRID_W = 64
ROPE_THETA = 10000.0
ROPE_PAIRS = HEAD_DIM // 4
N_EXPERTS = 8
EPS = 1e-6
ATT_SCALE = HEAD_DIM ** -0.5

COL_Q = 0
COL_K = ATT_W // HEAD_DIM
COL_V = COL_K + N_KV_HEADS
COL_RQ = COL_V + N_KV_HEADS
COL_RK = COL_RQ + N_RET_HEADS
COL_RV = COL_RK + N_RET_HEADS
COL_G = COL_RV + N_RET_HEADS

VMEM_LIMIT_BYTES = 56 * 1024 * 1024
LANES = 128

NEG_INF = float("-inf")


def _cparams(*sem):
    return pltpu.CompilerParams(dimension_semantics=sem, vmem_limit_bytes=VMEM_LIMIT_BYTES)


def _silu(x):
    return x * (1.0 / (1.0 + jnp.exp(-x)))


def _dot(a, b):
    return jnp.dot(a, b, preferred_element_type=F32)


def _dot_nt(a, b):
    return lax.dot_general(a, b, (((1,), (1,)), ((), ())), preferred_element_type=F32)


def _dot_tn(a, b):
    return lax.dot_general(a, b, (((0,), (0,)), ((), ())), preferred_element_type=F32)


def _norm_modulate(x, g, shift, scale):
    ms = jnp.mean(x * x, axis=-1, keepdims=True)
    return (x * lax.rsqrt(ms + EPS) * g) * (1.0 + scale) + shift


def _ada_kernel(r_ref, w_ref, b_ref, o_ref):
    s = _silu(r_ref[...])
    o_ref[...] = jnp.dot(s, w_ref[...], preferred_element_type=F32,
                         precision=lax.Precision.HIGHEST) + b_ref[...]


def _ada_mods(rows, ada_w, ada_b):
    depth, d, n = ada_w.shape
    tn = d // 2
    return pl.pallas_call(
        _ada_kernel,
        grid=(depth, n // tn),
        in_specs=[
            pl.BlockSpec((8, d), lambda l, j: (0, 0)),
            pl.BlockSpec((None, d, tn), lambda l, j: (l, 0, j)),
            pl.BlockSpec((None, 1, tn), lambda l, j: (l, 0, j)),
        ],
        out_specs=pl.BlockSpec((None, 8, tn), lambda l, j: (l, 0, j)),
        out_shape=jax.ShapeDtypeStruct((depth, 8, n), F32),
        compiler_params=_cparams("arbitrary", "arbitrary"),
        name="ada_mods",
    )(rows, ada_w, ada_b.reshape(depth, 1, n))


def _rope(y, cos, sa, sb):
    return y * cos + pltpu.roll(y, 96, 1) * sa + pltpu.roll(y, 32, 1) * sb


def _inproj_kernel(x_ref, sh_ref, sc_ref, g1_ref, w_ref, gain_ref, cos_ref, sa_ref, sb_ref, o_ref,
                   h_scr, p_scr, *, row, tn, n_j, n_steps):
    t = pl.program_id(0)
    j = t % n_j
    nchunk = tn // HEAD_DIM

    @pl.when(t == 0)
    def _():
        p_scr[...] = jnp.zeros_like(p_scr)

    @pl.when((j == 0) & (t < n_steps))
    def _():
        h = _norm_modulate(x_ref[...], g1_ref[...], sh_ref[row:row + 1, :], sc_ref[row:row + 1, :])
        h_scr[...] = h.astype(BF16)

    p_new = _dot(h_scr[...], w_ref[...])

    jp = (t + n_j - 1) % n_j
    tabs = None
    for c in range(nchunk):
        gc = jp * nchunk + c
        norm_on = gc < COL_V
        rope_on = norm_on | ((gc >= COL_RQ) & (gc < COL_RV))
        if c % N_KV_HEADS == 0:
            tabs = (jnp.where(rope_on, cos_ref[...], 1.0), jnp.where(rope_on, sa_ref[...], 0.0),
                    jnp.where(rope_on, sb_ref[...], 0.0))
        cols = slice(c * HEAD_DIM, (c + 1) * HEAD_DIM)
        pc = p_scr[:, cols]
        r = lax.rsqrt(jnp.mean(pc * pc, axis=-1, keepdims=True) + EPS)
        y = pc * jnp.where(norm_on, r, 1.0) * gain_ref[:, cols]
        o_ref[:, cols] = _rope(y, *tabs).astype(o_ref.dtype)
    p_scr[...] = p_new


def _in_projection(x, mods, layer, row, g1, w, gain, cos, sa, sb, tm):
    m, d = x.shape
    n = w.shape[2]
    tn = 2 * KV_W
    assert n == IN_COLS and ATT_W % tn == 0 and RET_W % tn == 0 and m % tm == 0
    n_i, n_j = m // tm, n // tn
    n_steps = n_i * n_j
    prev = lambda t: jnp.maximum(t - 1, 0)
    tab = pl.BlockSpec((tm, HEAD_DIM), lambda t: (prev(t) // n_j, 0))
    return pl.pallas_call(
        functools.partial(_inproj_kernel, row=row, tn=tn, n_j=n_j, n_steps=n_steps),
        grid=(n_steps + 1,),
        in_specs=[
            pl.BlockSpec((tm, d), lambda t: (jnp.minimum(t // n_j, n_i - 1), 0)),
            pl.BlockSpec((None, 8, d), lambda t: (layer, 0, 0)),
            pl.BlockSpec((None, 8, d), lambda t: (layer, 0, 1)),
            pl.BlockSpec((1, d), lambda t: (0, 0)),
            pl.BlockSpec((None, d, tn), lambda t: (layer, 0, t % n_j)),
            pl.BlockSpec((1, tn), lambda t: (0, prev(t) % n_j)),
            tab, tab, tab,
        ],
        out_specs=pl.BlockSpec((tm, tn), lambda t: (prev(t) // n_j, prev(t) % n_j)),
        out_shape=jax.ShapeDtypeStruct((m, n), BF16),
        scratch_shapes=[pltpu.VMEM((tm, d), BF16), pltpu.VMEM((tm, tn), F32)],
        compiler_params=_cparams("arbitrary"),
        name="in_projection",
    )(x, mods, mods, g1, w, gain, cos, sa, sb)


def _attn_kernel(sink_ref, q_ref, km_ref, vm_ref, kp_ref, vp_ref, kn_ref, vn_ref,
                 kc_ref, vc_ref, bias_ref, o_ref, kbuf, vbuf, vcbuf, *, tq):
    i = pl.program_id(0)
    last = pl.num_programs(0) - 1
    nsub = tq // WINDOW
    n_ctx_blk = kc_ref.shape[0] // HEAD_DIM
    kbuf[0:WINDOW] = kp_ref[...]
    kbuf[WINDOW:WINDOW + tq] = km_ref[...]
    kbuf[WINDOW + tq:] = kn_ref[...]
    for h in range(N_KV_HEADS):
        hs = slice(h * HEAD_DIM, (h + 1) * HEAD_DIM)
        va = slice(2 * h * HEAD_DIM, (2 * h + 1) * HEAD_DIM)
        vbuf[0:WINDOW, va] = vp_ref[:, hs]
        vbuf[WINDOW:WINDOW + tq, va] = vm_ref[:, hs]
        vbuf[WINDOW + tq:, va] = vn_ref[:, hs]

    @pl.when(i == 0)
    def _():
        for h in range(N_KV_HEADS):
            ones = slice((2 * h + 1) * HEAD_DIM, (2 * h + 2) * HEAD_DIM)
            vbuf[:, ones] = jnp.ones((vbuf.shape[0], HEAD_DIM), vbuf.dtype)
            vcbuf[:, ones] = jnp.ones((vcbuf.shape[0], HEAD_DIM), vcbuf.dtype)
            vcbuf[:, 2 * h * HEAD_DIM:(2 * h + 1) * HEAD_DIM] = vc_ref[:, h * HEAD_DIM:(h + 1) * HEAD_DIM]

    col = lax.broadcasted_iota(jnp.int32, (WINDOW, 3 * WINDOW), 1)
    for s in range(nsub):
        bias = bias_ref[...]
        if s == 0:
            bias = bias + jnp.where(col < WINDOW, jnp.where(i == 0, NEG_INF, 0.0), 0.0)
        if s == nsub - 1:
            bias = bias + jnp.where(col >= 2 * WINDOW, jnp.where(i == last, NEG_INF, 0.0), 0.0)
        bias4 = jnp.concatenate([bias] * GQA_GROUP, axis=0)
        for h in range(N_KV_HEADS):
            hs = slice(h * HEAD_DIM, (h + 1) * HEAD_DIM)
            q = jnp.concatenate(
                [q_ref[s * WINDOW:(s + 1) * WINDOW,
                       (h * GQA_GROUP + g) * HEAD_DIM:(h * GQA_GROUP + g + 1) * HEAD_DIM]
                 for g in range(GQA_GROUP)], axis=0)
            vs = slice(2 * h * HEAD_DIM, (2 * h + 2) * HEAD_DIM)
            kl = kbuf[s * WINDOW:(s + 3) * WINDOW, hs]
            vl = vbuf[s * WINDOW:(s + 3) * WINDOW, vs]
            s_loc = _dot_nt(q, kl) + bias4
            s_ctx = _dot_nt(q, kc_ref[:, hs])
            sink = jnp.concatenate(
                [jnp.full((WINDOW, 1), sink_ref[h * GQA_GROUP + g], F32) for g in range(GQA_GROUP)],
                axis=0)
            blocks = ([s_loc[:, b * HEAD_DIM:(b + 1) * HEAD_DIM] for b in range(3)]
                      + [s_ctx[:, b * HEAD_DIM:(b + 1) * HEAD_DIM] for b in range(n_ctx_blk)])
            m = jnp.maximum(jnp.max(functools.reduce(jnp.maximum, blocks), axis=-1, keepdims=True), sink)
            p_loc = jnp.exp(s_loc - m).astype(BF16)
            p_ctx = jnp.exp(s_ctx - m).astype(BF16)
            o_den = _dot(p_loc, vl) + _dot(p_ctx, vcbuf[:, vs])
            o = o_den[:, :HEAD_DIM] / (o_den[:, HEAD_DIM:] + jnp.exp(sink - m))
            for g in range(GQA_GROUP):
                o_ref[s * WINDOW:(s + 1) * WINDOW,
                      (h * GQA_GROUP + g) * HEAD_DIM:(h * GQA_GROUP + g + 1) * HEAD_DIM] = (
                    o[g * WINDOW:(g + 1) * WINDOW].astype(o_ref.dtype))


def _band_bias():
    r = jnp.arange(WINDOW)[:, None]
    c = jnp.arange(3 * WINDOW)[None, :]
    ok = (c - r >= 0) & (c - r <= 2 * WINDOW)
    return jnp.where(ok, 0.0, NEG_INF).astype(F32)


def _window_attention(p_lat, p_ctx, sink, tq):
    s = p_lat.shape[0]
    c = p_ctx.shape[0]
    r = tq // WINDOW
    nb = s // WINDOW
    kvb = KV_W // HEAD_DIM
    ck, cv = COL_K // kvb, COL_V // kvb
    return pl.pallas_call(
        functools.partial(_attn_kernel, tq=tq),
        grid=(s // tq,),
        in_specs=[
            pl.BlockSpec(memory_space=pltpu.SMEM),
            pl.BlockSpec((tq, ATT_W), lambda i: (i, 0)),
            pl.BlockSpec((tq, KV_W), lambda i: (i, ck)),
            pl.BlockSpec((tq, KV_W), lambda i: (i, cv)),
            pl.BlockSpec((WINDOW, KV_W), lambda i: (jnp.maximum(i * r - 1, 0), ck)),
            pl.BlockSpec((WINDOW, KV_W), lambda i: (jnp.maximum(i * r - 1, 0), cv)),
            pl.BlockSpec((WINDOW, KV_W), lambda i: (jnp.minimum((i + 1) * r, nb - 1), ck)),
            pl.BlockSpec((WINDOW, KV_W), lambda i: (jnp.minimum((i + 1) * r, nb - 1), cv)),
            pl.BlockSpec((c, KV_W), lambda i: (0, ck)),
            pl.BlockSpec((c, KV_W), lambda i: (0, cv)),
            pl.BlockSpec((WINDOW, 3 * WINDOW), lambda i: (0, 0)),
        ],
        out_specs=pl.BlockSpec((tq, ATT_W), lambda i: (i, 0)),
        out_shape=jax.ShapeDtypeStruct((s, ATT_W), BF16),
        scratch_shapes=[pltpu.VMEM((tq + 2 * WINDOW, KV_W), BF16),
                        pltpu.VMEM((tq + 2 * WINDOW, 2 * KV_W), BF16),
                        pltpu.VMEM((c, 2 * KV_W), BF16)],
        compiler_params=_cparams("arbitrary"),
        name="window_attention",
    )(sink, p_lat, p_lat, p_lat, p_lat, p_lat, p_lat, p_lat, p_ctx, p_ctx, _band_bias())


def _ctx_mixer_kernel(sink_ref, lg_ref, q_ref, k_ref, v_ref, rq_ref, rk_ref, rv_ref, g_ref, gn_ref,
                      att_ref, ret_ref, sf_ref, sb_ref):
    h = pl.program_id(0)
    c = q_ref.shape[0]
    lg_f = lg_ref[0, h]
    lg_b = lg_ref[1, h]
    s = _dot_nt(q_ref[...], k_ref[...])
    sink = jnp.full((c, 1), sink_ref[h], F32)
    m = jnp.maximum(jnp.max(s, axis=-1, keepdims=True), sink)
    p = jnp.exp(s - m)
    den = jnp.sum(p, axis=-1, keepdims=True) + jnp.exp(sink - m)
    att_ref[...] = (_dot(p.astype(BF16), v_ref[...]) / den).astype(att_ref.dtype)
    n_i = lax.broadcasted_iota(jnp.int32, (c, c), 0)
    m_i = lax.broadcasted_iota(jnp.int32, (c, c), 1)
    rel = (n_i - m_i).astype(F32)
    dec = (jnp.where(rel >= 0, jnp.exp(jnp.maximum(rel, 0.0) * lg_f), 0.0)
           + jnp.where(rel <= 0, jnp.exp(jnp.maximum(-rel, 0.0) * lg_b), 0.0))
    sc = _dot_nt(rq_ref[...], rk_ref[...]) * dec
    y = _dot(sc.astype(BF16), rv_ref[...])
    mu = jnp.mean(y, axis=-1, keepdims=True)
    var = jnp.mean(jnp.square(y - mu), axis=-1, keepdims=True)
    yn = (y - mu) * lax.rsqrt(var + EPS) * gn_ref[...]
    ret_ref[...] = (_silu(g_ref[...].astype(F32)) * yn).astype(ret_ref.dtype)
    pos = lax.broadcasted_iota(jnp.int32, (c, HEAD_DIM), 0).astype(F32)
    kf = rk_ref[...].astype(F32)
    k_f = (kf * jnp.exp((c - 1.0 - pos) * lg_f)).astype(BF16)
    k_b = (kf * jnp.exp(pos * lg_b)).astype(BF16)
    sf_ref[...] = _dot_tn(k_f, rv_ref[...])
    sb_ref[...] = _dot_tn(k_b, rv_ref[...])


def _ctx_mixer(p_ctx, sink, log_g, gn_g):
    c = p_ctx.shape[0]
    hb = lambda off: pl.BlockSpec((c, HEAD_DIM), lambda h: (0, off + h))
    return pl.pallas_call(
        _ctx_mixer_kernel,
        grid=(N_RET_HEADS,),
        in_specs=[
            pl.BlockSpec(memory_space=pltpu.SMEM),
            pl.BlockSpec(memory_space=pltpu.SMEM),
            hb(COL_Q),
            pl.BlockSpec((c, HEAD_DIM), lambda h: (0, COL_K + h // GQA_GROUP)),
            pl.BlockSpec((c, HEAD_DIM), lambda h: (0, COL_V + h // GQA_GROUP)),
            hb(COL_RQ), hb(COL_RK), hb(COL_RV), hb(COL_G),
            pl.BlockSpec((None, 1, HEAD_DIM), lambda h: (h, 0, 0)),
        ],
        out_specs=[
            pl.BlockSpec((c, HEAD_DIM), lambda h: (0, h)),
            pl.BlockSpec((c, HEAD_DIM), lambda h: (0, h)),
            pl.BlockSpec((None, HEAD_DIM, HEAD_DIM), lambda h: (h, 0, 0)),
            pl.BlockSpec((None, HEAD_DIM, HEAD_DIM), lambda h: (h, 0, 0)),
        ],
        out_shape=[
            jax.ShapeDtypeStruct((c, ATT_W), BF16),
            jax.ShapeDtypeStruct((c, RET_W), BF16),
            jax.ShapeDtypeStruct((N_RET_HEADS, HEAD_DIM, HEAD_DIM), F32),
            jax.ShapeDtypeStruct((N_RET_HEADS, HEAD_DIM, HEAD_DIM), F32),
        ],
        compiler_params=_cparams("arbitrary"),
        name="ctx_mixer",
    )(sink, log_g, p_ctx, p_ctx, p_ctx, p_ctx, p_ctx, p_ctx, p_ctx,
      gn_g.reshape(N_RET_HEADS, 1, HEAD_DIM))


RET_KERNEL_CHUNK = 256
RET_UNROLL = 8
RET_UNROLL_BWD = 8


def _ret_kernel(lg_ref, q_ref, k_ref, v_ref, g_ref, sf_ref, sb_ref, gn_ref, o_ref,
                y_scr, dmat, qdf, kdf, qdb, kdb, cdf, cdb):
    h = pl.program_id(0)
    ck = RET_KERNEL_CHUNK
    nit = q_ref.shape[0] // (ck * RET_UNROLL)
    lg_f = lg_ref[0, h]
    lg_b = lg_ref[1, h]
    ii = lax.broadcasted_iota(jnp.int32, (ck, ck), 0)
    jj = lax.broadcasted_iota(jnp.int32, (ck, ck), 1)
    rel = (ii - jj).astype(F32)
    dmat[...] = (jnp.where(rel >= 0, jnp.exp(jnp.maximum(rel, 0.0) * lg_f), 0.0)
                 + jnp.where(rel <= 0, jnp.exp(jnp.maximum(-rel, 0.0) * lg_b), 0.0))
    pos = lax.broadcasted_iota(jnp.int32, (ck, HEAD_DIM), 0).astype(F32)
    qdf[...] = jnp.exp((pos + 1.0) * lg_f)
    kdf[...] = jnp.exp((ck - 1.0 - pos) * lg_f)
    qdb[...] = jnp.exp((ck - pos) * lg_b)
    kdb[...] = jnp.exp(pos * lg_b)
    full = jnp.full((HEAD_DIM, HEAD_DIM), float(ck), F32)
    cdf[...] = jnp.exp(full * lg_f)
    cdb[...] = jnp.exp(full * lg_b)

    def rows_of(z):
        return pl.ds(pl.multiple_of(z * ck, ck), ck)

    def fwd(it, state):
        zs = [it * RET_UNROLL + u for u in range(RET_UNROLL)]
        qs = [q_ref[rows_of(z), :] for z in zs]
        ks = [k_ref[rows_of(z), :] for z in zs]
        vs = [v_ref[rows_of(z), :] for z in zs]
        inner = [_dot((_dot_nt(q, k) * dmat[...]).astype(BF16), v) for q, k, v in zip(qs, ks, vs)]
        kvs = [_dot_tn((k.astype(F32) * kdf[...]).astype(BF16), v) for k, v in zip(ks, vs)]
        for u, z in enumerate(zs):
            y_scr[rows_of(z), :] = inner[u] + _dot(qs[u], state.astype(BF16)) * qdf[...]
            state = cdf[...] * state + kvs[u]
        return state

    lax.fori_loop(0, nit, fwd, sf_ref[...])

    nit_b = q_ref.shape[0] // (ck * RET_UNROLL_BWD)

    def bwd(it, state):
        zs = [nit_b * RET_UNROLL_BWD - 1 - (it * RET_UNROLL_BWD + u) for u in range(RET_UNROLL_BWD)]
        qs = [q_ref[rows_of(z), :] for z in zs]
        kvs = [_dot_tn((k_ref[rows_of(z), :].astype(F32) * kdb[...]).astype(BF16), v_ref[rows_of(z), :])
               for z in zs]
        for u, z in enumerate(zs):
            y = y_scr[rows_of(z), :] + _dot(qs[u], state.astype(BF16)) * qdb[...]
            state = cdb[...] * state + kvs[u]
            mu = jnp.mean(y, axis=-1, keepdims=True)
            var = jnp.mean(jnp.square(y - mu), axis=-1, keepdims=True)
            yn = (y - mu) * lax.rsqrt(var + EPS) * gn_ref[...]
            o_ref[rows_of(z), :] = (_silu(g_ref[rows_of(z), :].astype(F32)) * yn).astype(o_ref.dtype)
        return state

    lax.fori_loop(0, nit_b, bwd, sb_ref[...])


def _retention(p_lat, log_g, s_f, s_b, gn_g):
    s = p_lat.shape[0]
    ck = RET_KERNEL_CHUNK
    assert s % (ck * RET_UNROLL) == 0 and s % (ck * RET_UNROLL_BWD) == 0
    col = lambda off: pl.BlockSpec((s, HEAD_DIM), lambda h: (0, off + h))
    state_spec = pl.BlockSpec((None, HEAD_DIM, HEAD_DIM), lambda h: (h, 0, 0))
    vec = pltpu.VMEM((ck, HEAD_DIM), F32)
    sq = pltpu.VMEM((HEAD_DIM, HEAD_DIM), F32)
    return pl.pallas_call(
        _ret_kernel,
        grid=(N_RET_HEADS,),
        in_specs=[pl.BlockSpec(memory_space=pltpu.SMEM),
                  col(COL_RQ), col(COL_RK), col(COL_RV), col(COL_G), state_spec, state_spec,
                  pl.BlockSpec((None, 1, HEAD_DIM), lambda h: (h, 0, 0))],
        out_specs=col(0),
        out_shape=jax.ShapeDtypeStruct((s, RET_W), BF16),
        scratch_shapes=[pltpu.VMEM((s, HEAD_DIM), F32), pltpu.VMEM((ck, ck), F32),
                        vec, vec, vec, vec, sq, sq],
        compiler_params=_cparams("arbitrary"),
        name="retention",
    )(log_g, p_lat, p_lat, p_lat, p_lat, s_f, s_b, gn_g.reshape(N_RET_HEADS, 1, HEAD_DIM))


def _outproj_kernel(x_ref, a_ref, r_ref, wa_ref, wr_ref, gate_ref, o_ref, *, row):
    y = _dot(a_ref[...], wa_ref[...]) + _dot(r_ref[...], wr_ref[...])
    o_ref[...] = x_ref[...] + gate_ref[row:row + 1, :] * y


def _out_projection(x, att, ret, w_out, mods, layer, row, tm):
    m, d = x.shape
    return pl.pallas_call(
        functools.partial(_outproj_kernel, row=row),
        grid=(m // tm,),
        in_specs=[
            pl.BlockSpec((tm, d), lambda i: (i, 0)),
            pl.BlockSpec((tm, ATT_W), lambda i: (i, 0)),
            pl.BlockSpec((tm, RET_W), lambda i: (i, 0)),
            pl.BlockSpec((None, ATT_W, d), lambda i: (layer, 0, 0)),
            pl.BlockSpec((None, RET_W, d), lambda i: (layer, ATT_W // RET_W, 0)),
            pl.BlockSpec((None, 8, d), lambda i: (layer, 0, 2)),
        ],
        out_specs=pl.BlockSpec((tm, d), lambda i: (i, 0)),
        out_shape=jax.ShapeDtypeStruct((m, d), F32),
        compiler_params=_cparams("arbitrary"),
        name="out_projection",
    )(x, att, ret, w_out, w_out, mods)


def _swiglu_accumulate(h_scr, w1_ref, w3_ref, w2_ref, o_ref, groups):
    w1 = w1_ref[...].astype(BF16)
    w3 = w3_ref[...].astype(BF16)
    w2 = w2_ref[...].astype(BF16)
    us = []
    for lo, n in groups:
        h = h_scr[lo:lo + n, :]
        us.append((_silu(_dot(h, w1)) * _dot(h, w3)).astype(BF16))
    for (lo, n), u in zip(groups, us):
        o_ref[lo:lo + n, :] += _dot(u, w2)


def _ffn_kernel(x_ref, sh_ref, sc_ref, gate_ref, g2_ref, w1_ref, w3_ref, w2_ref, o_ref, h_scr,
                *, row, n_parts):
    j = pl.program_id(1)
    part = x_ref.shape[0] // n_parts

    @pl.when(j == 0)
    def _():
        h = _norm_modulate(x_ref[...], g2_ref[...], sh_ref[row:row + 1, :], sc_ref[row:row + 1, :])
        h_scr[...] = h.astype(BF16)
        o_ref[...] = jnp.zeros_like(o_ref)

    _swiglu_accumulate(h_scr, w1_ref, w3_ref, w2_ref, o_ref, tuple((p * part, part) for p in range(n_parts)))

    @pl.when(j == pl.num_programs(1) - 1)
    def _():
        o_ref[...] = x_ref[...] + gate_ref[row:row + 1, :] * o_ref[...]


def _dense_ffn(x, mods, layer, row, g2, w1, w3, w2, tm, tf):
    m, d = x.shape
    f = w1.shape[1]
    mod = lambda k: pl.BlockSpec((None, 8, d), lambda i, j: (layer, 0, k))
    return pl.pallas_call(
        functools.partial(_ffn_kernel, row=row, n_parts=2 if tm >= 1024 else 1),
        grid=(m // tm, f // tf),
        in_specs=[
            pl.BlockSpec((tm, d), lambda i, j: (i, 0)),
            mod(3), mod(4), mod(5),
            pl.BlockSpec((1, d), lambda i, j: (0, 0)),
            pl.BlockSpec((d, tf), lambda i, j: (0, j)),
            pl.BlockSpec((d, tf), lambda i, j: (0, j)),
            pl.BlockSpec((tf, d), lambda i, j: (j, 0)),
        ],
        out_specs=pl.BlockSpec((tm, d), lambda i, j: (i, 0)),
        out_shape=jax.ShapeDtypeStruct((m, d), F32),
        scratch_shapes=[pltpu.VMEM((tm, d), BF16)],
        compiler_params=_cparams("arbitrary", "arbitrary"),
        name="dense_ffn",
    )(x, mods, mods, mods, g2, w1, w3, w2)


def _router_kernel(x_ref, sh_ref, sc_ref, g2_ref, rw_ref, rb_ref, h_ref, info_ref, cnt_ref, carry, *, row):
    i = pl.program_id(0)
    tm = x_ref.shape[0]

    @pl.when(i == 0)
    def _():
        carry[...] = jnp.zeros_like(carry)

    h = _norm_modulate(x_ref[...], g2_ref[...], sh_ref[row:row + 1, :], sc_ref[row:row + 1, :])
    h_ref[...] = h
    logits = jnp.dot(h, rw_ref[...], preferred_element_type=F32,
                     precision=lax.Precision.HIGHEST) + rb_ref[...]
    lane = lax.broadcasted_iota(jnp.int32, (tm, LANES), 1).astype(F32)
    logits = jnp.where(lane < N_EXPERTS, logits, NEG_INF)
    v1 = jnp.max(logits, axis=-1, keepdims=True)
    e1 = jnp.min(jnp.where(logits == v1, lane, float(LANES)), axis=-1, keepdims=True)
    rest = jnp.where(lane == e1, NEG_INF, logits)
    v2 = jnp.max(rest, axis=-1, keepdims=True)
    e2 = jnp.min(jnp.where(rest == v2, lane, float(LANES)), axis=-1, keepdims=True)
    t = jnp.exp(v2 - v1)
    w1 = 1.0 / (1.0 + t)
    w2 = t / (1.0 + t)
    oh1 = jnp.where(lane == e1, 1.0, 0.0)
    oh2 = jnp.where(lane == e2, 1.0, 0.0)
    oh = oh1 + oh2
    r_i = lax.broadcasted_iota(jnp.int32, (tm, tm), 0)
    c_i = lax.broadcasted_iota(jnp.int32, (tm, tm), 1)
    tri = jnp.where(c_i < r_i, 1.0, 0.0).astype(BF16)
    before = _dot(tri, oh.astype(BF16)) + carry[0:1, :]
    rank1 = jnp.sum(before * oh1, axis=-1, keepdims=True)
    rank2 = jnp.sum(before * oh2, axis=-1, keepdims=True)
    carry[...] = carry[...] + jnp.sum(oh, axis=0, keepdims=True)
    info = jnp.where(lane == 0.0, e1, 0.0)
    info = jnp.where(lane == 1.0, e2, info)
    info = jnp.where(lane == 2.0, rank1, info)
    info = jnp.where(lane == 3.0, rank2, info)
    info = jnp.where(lane == 4.0, w1, info)
    info = jnp.where(lane == 5.0, w2, info)
    info_ref[...] = info
    cnt_ref[...] = carry[...]


def _router(x, mods, layer, row, g2, rw_pad, rb_pad, tm):
    m, d = x.shape
    mod = lambda k: pl.BlockSpec((None, 8, d), lambda i: (layer, 0, k))
    return pl.pallas_call(
        functools.partial(_router_kernel, row=row),
        grid=(m // tm,),
        in_specs=[
            pl.BlockSpec((tm, d), lambda i: (i, 0)),
            mod(3), mod(4),
            pl.BlockSpec((1, d), lambda i: (0, 0)),
            pl.BlockSpec((d, LANES), lambda i: (0, 0)),
            pl.BlockSpec((1, LANES), lambda i: (0, 0)),
        ],
        out_specs=[
            pl.BlockSpec((tm, d), lambda i: (i, 0)),
            pl.BlockSpec((tm, LANES), lambda i: (i, 0)),
            pl.BlockSpec((8, LANES), lambda i: (0, 0)),
        ],
        out_shape=[
            jax.ShapeDtypeStruct((m, d), F32),
            jax.ShapeDtypeStruct((m, LANES), F32),
            jax.ShapeDtypeStruct((8, LANES), F32),
        ],
        scratch_shapes=[pltpu.VMEM((8, LANES), F32)],
        compiler_params=_cparams("arbitrary"),
        name="moe_router",
    )(x, mods, mods, g2, rw_pad, rb_pad)


GATHER_UNROLL = 8


def _gather_rows(idx_ref, src_hbm, dst, sem, n):
    def body(kb, carry):
        for u in range(GATHER_UNROLL):
            k = kb * GATHER_UNROLL + u
            pltpu.make_async_copy(src_hbm.at[pl.ds(idx_ref[0, 0, k], 1)], dst.at[pl.ds(k, 1)], sem).start()
        return carry
    lax.fori_loop(0, n // GATHER_UNROLL, body, 0)


def _wait_rows(src_hbm, dst, sem, n):
    pltpu.make_async_copy(src_hbm.at[pl.ds(0, n)], dst, sem).wait()


def _issue_rows(idx_ref, src_hbm, dst, sem, start, count):
    for u in range(count):
        k = start + u
        pltpu.make_async_copy(src_hbm.at[pl.ds(idx_ref[0, 0, k], 1)], dst.at[pl.ds(k, 1)], sem).start()


def _moe_ffn_kernel(te_ref, nu_ref, nv_ref, cur_ref, nxt_ref, h_hbm, w1_ref, w3_ref, w2_ref, o_ref,
                    buf, sem, h_scr, *, tm, n_f):
    r = pl.program_id(0)
    j = pl.program_id(1)
    n_used = nu_ref[0]
    half = tm // 2
    quarter = tm // 4
    per_step = tm // n_f
    head = tm - per_step * n_f

    @pl.when(r < n_used)
    def _():
        @pl.when(j == 0)
        def _():
            @pl.when(r == 0)
            def _():
                _gather_rows(cur_ref, h_hbm, buf, sem, tm)

            _wait_rows(h_hbm, buf, sem, tm)
            h_scr[...] = buf[...].astype(BF16)

            @pl.when(r + 1 < n_used)
            def _():
                _issue_rows(nxt_ref, h_hbm, buf, sem, 0, head)

            o_ref[...] = jnp.zeros_like(o_ref)

        more = r + 1 < n_used
        quarters = (nv_ref[r] + quarter - 1) // quarter
        row_groups = {1: ((0, quarter),), 2: ((0, half),), 3: ((0, half), (half, quarter)),
                      4: ((0, half), (half, half))}

        def case(fetch, groups):
            if fetch:
                _issue_rows(nxt_ref, h_hbm, buf, sem, head + j * per_step, per_step)
            _swiglu_accumulate(h_scr, w1_ref, w3_ref, w2_ref, o_ref, groups)

        for nq, groups in row_groups.items():
            pl.when(more & (quarters == nq))(functools.partial(case, True, groups))
            pl.when(jnp.logical_not(more) & (quarters == nq))(functools.partial(case, False, groups))

    @pl.when((r >= n_used) & (j == 0))
    def _():
        o_ref[...] = jnp.zeros_like(o_ref)


def _moe_ffn(h2, tile_expert, n_used, n_valid, src_rows, w1, w3, w2, tm, tf):
    d = h2.shape[1]
    f = w1.shape[2]
    n_tiles = src_rows.shape[0]
    n_f = f // tf
    clamp = lambda r, nu: jnp.maximum(jnp.minimum(r, nu[0] - 1), 0)
    fcol = lambda r, j, nu: jnp.where(r < nu[0], j, n_f - 1)
    grid_spec = pltpu.PrefetchScalarGridSpec(
        num_scalar_prefetch=3,
        grid=(n_tiles, n_f),
        in_specs=[
            pl.BlockSpec((1, 1, tm), lambda r, j, te, nu, nv: (clamp(r, nu), 0, 0), memory_space=pltpu.SMEM),
            pl.BlockSpec((1, 1, tm), lambda r, j, te, nu, nv: (clamp(r + 1, nu), 0, 0), memory_space=pltpu.SMEM),
            pl.BlockSpec(memory_space=pl.ANY),
            pl.BlockSpec((None, d, tf), lambda r, j, te, nu, nv: (te[clamp(r, nu)], 0, fcol(r, j, nu))),
            pl.BlockSpec((None, d, tf), lambda r, j, te, nu, nv: (te[clamp(r, nu)], 0, fcol(r, j, nu))),
            pl.BlockSpec((None, tf, d), lambda r, j, te, nu, nv: (te[clamp(r, nu)], fcol(r, j, nu), 0)),
        ],
        out_specs=pl.BlockSpec((tm, d), lambda r, j, te, nu, nv: (r, 0)),
        scratch_shapes=[
            pltpu.VMEM((tm, d), F32),
            pltpu.SemaphoreType.DMA(()),
            pltpu.VMEM((tm, d), BF16),
        ],
    )
    return pl.pallas_call(
        functools.partial(_moe_ffn_kernel, tm=tm, n_f=n_f),
        grid_spec=grid_spec,
        out_shape=jax.ShapeDtypeStruct((n_tiles * tm, d), F32),
        compiler_params=_cparams("arbitrary", "arbitrary"),
        name="moe_ffn",
    )(tile_expert, n_used, n_valid, src_rows, src_rows, h2, w1, w3, w2)


def _combine_kernel(cur_ref, nxt_ref, x_ref, info_ref, gate_ref, y_hbm, o_ref, buf, sems, *, tm, row):
    i = pl.program_id(0)
    slot = i % 2

    @pl.when(i == 0)
    def _():
        _gather_rows(cur_ref, y_hbm, buf.at[0], sems.at[0], 2 * tm)

    _wait_rows(y_hbm, buf.at[slot], sems.at[slot], 2 * tm)

    @pl.when(i + 1 < pl.num_programs(0))
    def _():
        _gather_rows(nxt_ref, y_hbm, buf.at[1 - slot], sems.at[1 - slot], 2 * tm)

    info = info_ref[...]
    w1 = info[:, 4:5]
    w2 = info[:, 5:6]
    y = w1 * buf[slot, 0:tm, :] + w2 * buf[slot, tm:2 * tm, :]
    o_ref[...] = x_ref[...] + gate_ref[row:row + 1, :] * y


def _moe_combine(x, info, mods, layer, row, y_rows, pos_tiles, tm):
    m, d = x.shape
    nt = m // tm
    return pl.pallas_call(
        functools.partial(_combine_kernel, tm=tm, row=row),
        grid=(nt,),
        in_specs=[
            pl.BlockSpec((1, 1, 2 * tm), lambda i: (i, 0, 0), memory_space=pltpu.SMEM),
            pl.BlockSpec((1, 1, 2 * tm), lambda i: (jnp.minimum(i + 1, nt - 1), 0, 0), memory_space=pltpu.SMEM),
            pl.BlockSpec((tm, d), lambda i: (i, 0)),
            pl.BlockSpec((tm, LANES), lambda i: (i, 0)),
            pl.BlockSpec((None, 8, d), lambda i: (layer, 0, 5)),
            pl.BlockSpec(memory_space=pl.ANY),
        ],
        out_specs=pl.BlockSpec((tm, d), lambda i: (i, 0)),
        out_shape=jax.ShapeDtypeStruct((m, d), F32),
        scratch_shapes=[pltpu.VMEM((2, 2 * tm, d), F32), pltpu.SemaphoreType.DMA((2,))],
        compiler_params=_cparams("arbitrary"),
        name="moe_combine",
    )(pos_tiles, pos_tiles, x, info, mods, y_rows)


def _moe_layer(x, mods, layer, row, g2, router_w, router_b, w1, w3, w2, tm_route, tm_exp, tf):
    m, d = x.shape
    rw_pad = jnp.pad(router_w, ((0, 0), (0, LANES - N_EXPERTS)))
    rb_pad = jnp.pad(router_b, (0, LANES - N_EXPERTS)).reshape(1, LANES)
    h2, info, cnt = _router(x, mods, layer, row, g2, rw_pad, rb_pad, tm_route)
    counts = cnt[0, :N_EXPERTS].astype(jnp.int32)
    tiles_per = (counts + tm_exp - 1) // tm_exp
    tile_end = jnp.cumsum(tiles_per)
    group_start = (tile_end - tiles_per) * tm_exp
    n_tiles = (2 * m) // tm_exp + N_EXPERTS
    n_used = tile_end[-1:].astype(jnp.int32)
    tile_ids = jnp.arange(n_tiles, dtype=jnp.int32)
    tile_expert = jnp.minimum(jnp.sum(tile_end[None, :] <= tile_ids[:, None], axis=1),
                              N_EXPERTS - 1).astype(jnp.int32)
    rows_before = (tile_ids - (tile_end - tiles_per)[tile_expert]) * tm_exp
    n_valid = jnp.clip(counts[tile_expert] - rows_before, 0, tm_exp).astype(jnp.int32)
    e12 = info[:, 0:2].astype(jnp.int32)
    rank12 = info[:, 2:4].astype(jnp.int32)
    pos = group_start[e12] + rank12
    tok = jnp.broadcast_to(jnp.arange(m, dtype=jnp.int32)[:, None], (m, 2))
    src_rows = jnp.zeros((n_tiles * tm_exp,), jnp.int32).at[pos.reshape(-1)].set(
        tok.reshape(-1), unique_indices=True, mode="promise_in_bounds")
    y_rows = _moe_ffn(h2, tile_expert, n_used, n_valid, src_rows.reshape(n_tiles, 1, tm_exp),
                      w1, w3, w2, tm_exp, tf)
    nt = m // tm_route
    pos_tiles = pos.reshape(nt, tm_route, 2).transpose(0, 2, 1).reshape(nt, 1, 2 * tm_route)
    return _moe_combine(x, info, mods, layer, row, y_rows, pos_tiles, tm_route)


def _rope_tables(n_tokens):
    n_rows = n_tokens // GRID_W
    inv = ROPE_THETA ** (-jnp.arange(ROPE_PAIRS, dtype=F32) / ROPE_PAIRS)
    ang_r = jnp.arange(n_rows).astype(F32)[:, None] * inv
    ang_c = jnp.arange(GRID_W).astype(F32)[:, None] * inv
    cos_r = jnp.repeat(jnp.cos(ang_r), GRID_W, axis=0)
    sin_r = jnp.repeat(jnp.sin(ang_r), GRID_W, axis=0)
    cos_c = jnp.tile(jnp.cos(ang_c), (n_rows, 1))
    sin_c = jnp.tile(jnp.sin(ang_c), (n_rows, 1))
    zero = jnp.zeros_like(sin_r)
    cos_t = jnp.concatenate([cos_r, cos_r, cos_c, cos_c], axis=-1)
    sa_t = jnp.concatenate([-sin_r, zero, -sin_c, zero], axis=-1)
    sb_t = jnp.concatenate([zero, sin_r, zero, sin_c], axis=-1)
    return cos_t, sa_t, sb_t


def kernel(x, c, ctx, c_ctx, ada_w, ada_b, norm1_g, norm2_g, w_in, q_norm_g, k_norm_g, attn_sink,
           ret_decay, ret_gn_g, w_out, ffn_w1, ffn_w3, ffn_w2, router_w, router_b, moe_w1, moe_w3, moe_w2):
    b, s, d = x.shape
    n_ctx = ctx.shape[1]
    depth = ada_w.shape[0]
    assert b == 1, "one latent sequence per call"
    x_lat = x[0]
    x_ctx = ctx[0]

    tm_lat = min(1024, s)
    tm_ffn = min(512, s)
    tq = min(512, s)
    tf_ffn = 512
    tm_moe = 1024
    tf_moe = 256

    rows = jnp.concatenate([c[0:1], c_ctx[None, :], jnp.zeros((6, d), F32)], axis=0)
    mods = _ada_mods(rows, ada_w, ada_b)
    cos_l, sa_l, sb_l = _rope_tables(s)
    cos_c = jnp.ones((n_ctx, HEAD_DIM), F32)
    zer_c = jnp.zeros((n_ctx, HEAD_DIM), F32)
    log_g_all = jax.nn.log_sigmoid(ret_decay.astype(F32))

    w_in_i = w_in.astype(BF16)
    w_out_i = w_out.astype(BF16)

    for i in range(depth):
        last = i == depth - 1
        g1 = norm1_g[i].reshape(1, d)
        g2 = norm2_g[i].reshape(1, d)
        log_g = log_g_all[i]
        gain = jnp.concatenate([
            jnp.tile(q_norm_g[i].astype(F32) * ATT_SCALE, N_ATT_HEADS),
            jnp.tile(k_norm_g[i].astype(F32), N_KV_HEADS),
            jnp.ones((KV_W + RET_W,), F32),
            jnp.full((RET_W,), ATT_SCALE, F32),
            jnp.ones((2 * RET_W,), F32)]).reshape(1, IN_COLS)

        p_lat = _in_projection(x_lat, mods, i, 0, g1, w_in_i, gain, cos_l, sa_l, sb_l, tm_lat)
        p_ctx = _in_projection(x_ctx, mods, i, 1, g1, w_in_i, gain, cos_c, zer_c, zer_c, n_ctx)

        att_c, ret_c, s_f, s_b = _ctx_mixer(p_ctx, attn_sink[i], log_g, ret_gn_g[i])
        att_l = _window_attention(p_lat, p_ctx, attn_sink[i], tq)
        ret_l = _retention(p_lat, log_g, s_f, s_b, ret_gn_g[i])
        x_lat = _out_projection(x_lat, att_l, ret_l, w_out_i, mods, i, 0, tm_ffn)

        j = i // 2
        if i % 2 == 0:
            w1, w3, w2 = ffn_w1[j].astype(BF16), ffn_w3[j].astype(BF16), ffn_w2[j].astype(BF16)
            x_lat = _dense_ffn(x_lat, mods, i, 0, g2, w1, w3, w2, tm_ffn, tf_ffn)
        else:
            x_lat = _moe_layer(x_lat, mods, i, 0, g2, router_w[j], router_b[j],
                               moe_w1[j], moe_w3[j], moe_w2[j], tm_ffn, min(tm_moe, s), tf_moe)

        if not last:
            x_ctx = _out_projection(x_ctx, att_c, ret_c, w_out_i, mods, i, 1, n_ctx)
            if i % 2 == 0:
                x_ctx = _dense_ffn(x_ctx, mods, i, 1, g2, w1, w3, w2, n_ctx, tf_ffn)
            else:
                x_ctx = _moe_layer(x_ctx, mods, i, 1, g2, router_w[j], router_b[j],
                                   moe_w1[j], moe_w3[j], moe_w2[j], n_ctx, n_ctx, tf_moe)
    return x_lat[None]
```

```python
import functools

import jax
import jax.numpy as jnp
from jax import lax
from jax.experimental import pallas as pl
from jax.experimental.pallas import tpu as pltpu

F32 = jnp.float32
BF16 = jnp.bfloat16

HEAD_DIM = 128
N_ATT_HEADS = 8
N_KV_HEADS = 2
N_RET_HEADS = 8
GQA_GROUP = N_ATT_HEADS // N_KV_HEADS
ATT_W = N_ATT_HEADS * HEAD_DIM
KV_W = N_KV_HEADS * HEAD_DIM
RET_W = N_RET_HEADS * HEAD_DIM
IN_COLS = ATT_W + 2 * KV_W + 4 * RET_W
WINDOW = 128
RET_CHUNK = 128
GRID_W = 64
ROPE_THETA = 10000.0
ROPE_PAIRS = HEAD_DIM // 4
N_EXPERTS = 8
EPS = 1e-6
ATT_SCALE = HEAD_DIM ** -0.5

COL_Q = 0
COL_K = ATT_W // HEAD_DIM
COL_V = COL_K + N_KV_HEADS
COL_RQ = COL_V + N_KV_HEADS
COL_RK = COL_RQ + N_RET_HEADS
COL_RV = COL_RK + N_RET_HEADS
COL_G = COL_RV + N_RET_HEADS

VMEM_LIMIT_BYTES = 56 * 1024 * 1024
LANES = 128

NEG_INF = float("-inf")


def _cparams(*sem):
    return pltpu.CompilerParams(dimension_semantics=sem, vmem_limit_bytes=VMEM_LIMIT_BYTES)


def _silu(x):
    return x * (1.0 / (1.0 + jnp.exp(-x)))


def _dot(a, b):
    return jnp.dot(a, b, preferred_element_type=F32)


def _dot_nt(a, b):
    return lax.dot_general(a, b, (((1,), (1,)), ((), ())), preferred_element_type=F32)


def _dot_tn(a, b):
    return lax.dot_general(a, b, (((0,), (0,)), ((), ())), preferred_element_type=F32)


def _norm_modulate(x, g, shift, scale):
    ms = jnp.mean(x * x, axis=-1, keepdims=True)
    return (x * lax.rsqrt(ms + EPS) * g) * (1.0 + scale) + shift


N_STREAMS = 2


def _ada_kernel(c_ref, w_ref, b_ref, o_ref):
    tn = w_ref.shape[1]
    w = w_ref[...]
    rows = [jnp.sum(w * jnp.tile(_silu(c_ref[r]), (1, tn // LANES)), axis=0, keepdims=True)
            for r in range(N_STREAMS)]
    o_ref[...] = jnp.concatenate(rows + [jnp.zeros((8 - N_STREAMS, tn), F32)], axis=0) + b_ref[...]


def _ada_mods(cond, ada_w, ada_b):
    depth, d, n = ada_w.shape
    tn = d // 2
    cond_lanes = jnp.broadcast_to(cond[:, :, None], (N_STREAMS, d, LANES))
    return pl.pallas_call(
        _ada_kernel,
        grid=(depth, n // tn),
        in_specs=[
            pl.BlockSpec((N_STREAMS, d, LANES), lambda l, j: (0, 0, 0)),
            pl.BlockSpec((None, d, tn), lambda l, j: (l, 0, j)),
            pl.BlockSpec((None, 1, tn), lambda l, j: (l, 0, j)),
        ],
        out_specs=pl.BlockSpec((None, 8, tn), lambda l, j: (l, 0, j)),
        out_shape=jax.ShapeDtypeStruct((depth, 8, n), F32),
        compiler_params=_cparams("arbitrary", "arbitrary"),
        name="ada_mods",
    )(cond_lanes, ada_w, ada_b.reshape(depth, 1, n))


def _rope(y, cos, sa, sb):
    return y * cos + pltpu.roll(y, 96, 1) * sa + pltpu.roll(y, 32, 1) * sb


def _inproj_kernel(x_ref, sh_ref, sc_ref, g1_ref, w_ref, gain_ref, cos_ref, sa_ref, sb_ref, o_ref,
                   h_scr, p_scr, *, row, tn, n_j, n_steps):
    t = pl.program_id(0)
    j = t % n_j
    nchunk = tn // HEAD_DIM

    @pl.when(t == 0)
    def _():
        p_scr[...] = jnp.zeros_like(p_scr)

    @pl.when((j == 0) & (t < n_steps))
    def _():
        h = _norm_modulate(x_ref[...], g1_ref[...], sh_ref[row:row + 1, :], sc_ref[row:row + 1, :])
        h_scr[...] = h.astype(BF16)

    p_new = _dot(h_scr[...], w_ref[...])

    jp = (t + n_j - 1) % n_j
    tabs = None
    for c in range(nchunk):
        gc = jp * nchunk + c
        norm_on = gc < COL_V
        rope_on = norm_on | ((gc >= COL_RQ) & (gc < COL_RV))
        if c % N_KV_HEADS == 0:
            tabs = (jnp.where(rope_on, cos_ref[...], 1.0), jnp.where(rope_on, sa_ref[...], 0.0),
                    jnp.where(rope_on, sb_ref[...], 0.0))
        cols = slice(c * HEAD_DIM, (c + 1) * HEAD_DIM)
        pc = p_scr[:, cols]
        r = lax.rsqrt(jnp.mean(pc * pc, axis=-1, keepdims=True) + EPS)
        y = pc * jnp.where(norm_on, r, 1.0) * gain_ref[:, cols]
        o_ref[:, cols] = _rope(y, *tabs).astype(o_ref.dtype)
    p_scr[...] = p_new


def _in_projection(x, mods, layer, row, g1, w, gain, cos, sa, sb, tm):
    m, d = x.shape
    n = w.shape[2]
    tn = 2 * KV_W
    assert n == IN_COLS and ATT_W % tn == 0 and RET_W % tn == 0 and m % tm == 0
    n_i, n_j = m // tm, n // tn
    n_steps = n_i * n_j
    prev = lambda t: jnp.maximum(t - 1, 0)
    tab = pl.BlockSpec((tm, HEAD_DIM), lambda t: (prev(t) // n_j, 0))
    return pl.pallas_call(
        functools.partial(_inproj_kernel, row=row, tn=tn, n_j=n_j, n_steps=n_steps),
        grid=(n_steps + 1,),
        in_specs=[
            pl.BlockSpec((tm, d), lambda t: (jnp.minimum(t // n_j, n_i - 1), 0)),
            pl.BlockSpec((None, 8, d), lambda t: (layer, 0, 0)),
            pl.BlockSpec((None, 8, d), lambda t: (layer, 0, 1)),
            pl.BlockSpec((1, d), lambda t: (0, 0)),
            pl.BlockSpec((None, d, tn), lambda t: (layer, 0, t % n_j)),
            pl.BlockSpec((1, tn), lambda t: (0, prev(t) % n_j)),
            tab, tab, tab,
        ],
        out_specs=pl.BlockSpec((tm, tn), lambda t: (prev(t) // n_j, prev(t) % n_j)),
        out_shape=jax.ShapeDtypeStruct((m, n), BF16),
        scratch_shapes=[pltpu.VMEM((tm, d), BF16), pltpu.VMEM((tm, tn), F32)],
        compiler_params=_cparams("arbitrary"),
        name="in_projection",
    )(x, mods, mods, g1, w, gain, cos, sa, sb)


def _attn_kernel(sink_ref, q_ref, km_ref, vm_ref, kp_ref, vp_ref, kn_ref, vn_ref,
                 kc_ref, vc_ref, bias_ref, o_ref, kbuf, vbuf, vcbuf, *, tq):
    i = pl.program_id(0)
    last = pl.num_programs(0) - 1
    nsub = tq // WINDOW
    n_ctx_blk = kc_ref.shape[0] // HEAD_DIM
    kbuf[0:WINDOW] = kp_ref[...]
    kbuf[WINDOW:WINDOW + tq] = km_ref[...]
    kbuf[WINDOW + tq:] = kn_ref[...]
    for h in range(N_KV_HEADS):
        hs = slice(h * HEAD_DIM, (h + 1) * HEAD_DIM)
        va = slice(2 * h * HEAD_DIM, (2 * h + 1) * HEAD_DIM)
        vbuf[0:WINDOW, va] = vp_ref[:, hs]
        vbuf[WINDOW:WINDOW + tq, va] = vm_ref[:, hs]
        vbuf[WINDOW + tq:, va] = vn_ref[:, hs]

    @pl.when(i == 0)
    def _():
        for h in range(N_KV_HEADS):
            ones = slice((2 * h + 1) * HEAD_DIM, (2 * h + 2) * HEAD_DIM)
            vbuf[:, ones] = jnp.ones((vbuf.shape[0], HEAD_DIM), vbuf.dtype)
            vcbuf[:, ones] = jnp.ones((vcbuf.shape[0], HEAD_DIM), vcbuf.dtype)
            vcbuf[:, 2 * h * HEAD_DIM:(2 * h + 1) * HEAD_DIM] = vc_ref[:, h * HEAD_DIM:(h + 1) * HEAD_DIM]

    col = lax.broadcasted_iota(jnp.int32, (WINDOW, 3 * WINDOW), 1)
    for s in range(nsub):
        bias = bias_ref[...]
        if s == 0:
            bias = bias + jnp.where(col < WINDOW, jnp.where(i == 0, NEG_INF, 0.0), 0.0)
        if s == nsub - 1:
            bias = bias + jnp.where(col >= 2 * WINDOW, jnp.where(i == last, NEG_INF, 0.0), 0.0)
        bias4 = jnp.concatenate([bias] * GQA_GROUP, axis=0)
        for h in range(N_KV_HEADS):
            hs = slice(h * HEAD_DIM, (h + 1) * HEAD_DIM)
            q = jnp.concatenate(
                [q_ref[s * WINDOW:(s + 1) * WINDOW,
                       (h * GQA_GROUP + g) * HEAD_DIM:(h * GQA_GROUP + g + 1) * HEAD_DIM]
                 for g in range(GQA_GROUP)], axis=0)
            vs = slice(2 * h * HEAD_DIM, (2 * h + 2) * HEAD_DIM)
            kl = kbuf[s * WINDOW:(s + 3) * WINDOW, hs]
            vl = vbuf[s * WINDOW:(s + 3) * WINDOW, vs]
            s_loc = _dot_nt(q, kl) + bias4
            s_ctx = _dot_nt(q, kc_ref[:, hs])
            sink = jnp.concatenate(
                [jnp.full((WINDOW, 1), sink_ref[h * GQA_GROUP + g], F32) for g in range(GQA_GROUP)],
                axis=0)
            blocks = ([s_loc[:, b * HEAD_DIM:(b + 1) * HEAD_DIM] for b in range(3)]
                      + [s_ctx[:, b * HEAD_DIM:(b + 1) * HEAD_DIM] for b in range(n_ctx_blk)])
            m = jnp.maximum(jnp.max(functools.reduce(jnp.maximum, blocks), axis=-1, keepdims=True), sink)
            p_loc = jnp.exp(s_loc - m).astype(BF16)
            p_ctx = jnp.exp(s_ctx - m).astype(BF16)
            o_den = _dot(p_loc, vl) + _dot(p_ctx, vcbuf[:, vs])
            o = o_den[:, :HEAD_DIM] / (o_den[:, HEAD_DIM:] + jnp.exp(sink - m))
            for g in range(GQA_GROUP):
                o_ref[s * WINDOW:(s + 1) * WINDOW,
                      (h * GQA_GROUP + g) * HEAD_DIM:(h * GQA_GROUP + g + 1) * HEAD_DIM] = (
                    o[g * WINDOW:(g + 1) * WINDOW].astype(o_ref.dtype))


def _band_bias():
    r = jnp.arange(WINDOW)[:, None]
    c = jnp.arange(3 * WINDOW)[None, :]
    ok = (c - r >= 0) & (c - r <= 2 * WINDOW)
    return jnp.where(ok, 0.0, NEG_INF).astype(F32)


def _window_attention(p_lat, p_ctx, sink, tq):
    s = p_lat.shape[0]
    c = p_ctx.shape[0]
    r = tq // WINDOW
    nb = s // WINDOW
    kvb = KV_W // HEAD_DIM
    ck, cv = COL_K // kvb, COL_V // kvb
    return pl.pallas_call(
        functools.partial(_attn_kernel, tq=tq),
        grid=(s // tq,),
        in_specs=[
            pl.BlockSpec(memory_space=pltpu.SMEM),
            pl.BlockSpec((tq, ATT_W), lambda i: (i, 0)),
            pl.BlockSpec((tq, KV_W), lambda i: (i, ck)),
            pl.BlockSpec((tq, KV_W), lambda i: (i, cv)),
            pl.BlockSpec((WINDOW, KV_W), lambda i: (jnp.maximum(i * r - 1, 0), ck)),
            pl.BlockSpec((WINDOW, KV_W), lambda i: (jnp.maximum(i * r - 1, 0), cv)),
            pl.BlockSpec((WINDOW, KV_W), lambda i: (jnp.minimum((i + 1) * r, nb - 1), ck)),
            pl.BlockSpec((WINDOW, KV_W), lambda i: (jnp.minimum((i + 1) * r, nb - 1), cv)),
            pl.BlockSpec((c, KV_W), lambda i: (0, ck)),
            pl.BlockSpec((c, KV_W), lambda i: (0, cv)),
            pl.BlockSpec((WINDOW, 3 * WINDOW), lambda i: (0, 0)),
        ],
        out_specs=pl.BlockSpec((tq, ATT_W), lambda i: (i, 0)),
        out_shape=jax.ShapeDtypeStruct((s, ATT_W), BF16),
        scratch_shapes=[pltpu.VMEM((tq + 2 * WINDOW, KV_W), BF16),
                        pltpu.VMEM((tq + 2 * WINDOW, 2 * KV_W), BF16),
                        pltpu.VMEM((c, 2 * KV_W), BF16)],
        compiler_params=_cparams("arbitrary"),
        name="window_attention",
    )(sink, p_lat, p_lat, p_lat, p_lat, p_lat, p_lat, p_lat, p_ctx, p_ctx, _band_bias())


def _ctx_mixer_kernel(sink_ref, lg_ref, q_ref, k_ref, v_ref, rq_ref, rk_ref, rv_ref, g_ref, gn_ref,
                      att_ref, ret_ref, sf_ref, sb_ref):
    h = pl.program_id(0)
    c = q_ref.shape[0]
    lg_f = lg_ref[0, h]
    lg_b = lg_ref[1, h]
    s = _dot_nt(q_ref[...], k_ref[...])
    sink = jnp.full((c, 1), sink_ref[h], F32)
    m = jnp.maximum(jnp.max(s, axis=-1, keepdims=True), sink)
    p = jnp.exp(s - m)
    den = jnp.sum(p, axis=-1, keepdims=True) + jnp.exp(sink - m)
    att_ref[...] = (_dot(p.astype(BF16), v_ref[...]) / den).astype(att_ref.dtype)
    n_i = lax.broadcasted_iota(jnp.int32, (c, c), 0)
    m_i = lax.broadcasted_iota(jnp.int32, (c, c), 1)
    rel = (n_i - m_i).astype(F32)
    dec = (jnp.where(rel >= 0, jnp.exp(jnp.maximum(rel, 0.0) * lg_f), 0.0)
           + jnp.where(rel <= 0, jnp.exp(jnp.maximum(-rel, 0.0) * lg_b), 0.0))
    sc = _dot_nt(rq_ref[...], rk_ref[...]) * dec
    y = _dot(sc.astype(BF16), rv_ref[...])
    mu = jnp.mean(y, axis=-1, keepdims=True)
    var = jnp.mean(jnp.square(y - mu), axis=-1, keepdims=True)
    yn = (y - mu) * lax.rsqrt(var + EPS) * gn_ref[...]
    ret_ref[...] = (_silu(g_ref[...].astype(F32)) * yn).astype(ret_ref.dtype)
    pos = lax.broadcasted_iota(jnp.int32, (c, HEAD_DIM), 0).astype(F32)
    kf = rk_ref[...].astype(F32)
    k_f = (kf * jnp.exp((c - 1.0 - pos) * lg_f)).astype(BF16)
    k_b = (kf * jnp.exp(pos * lg_b)).astype(BF16)
    sf_ref[...] = _dot_tn(k_f, rv_ref[...])
    sb_ref[...] = _dot_tn(k_b, rv_ref[...])


def _ctx_mixer(p_ctx, sink, log_g, gn_g):
    c = p_ctx.shape[0]
    hb = lambda off: pl.BlockSpec((c, HEAD_DIM), lambda h: (0, off + h))
    return pl.pallas_call(
        _ctx_mixer_kernel,
        grid=(N_RET_HEADS,),
        in_specs=[
            pl.BlockSpec(memory_space=pltpu.SMEM),
            pl.BlockSpec(memory_space=pltpu.SMEM),
            hb(COL_Q),
            pl.BlockSpec((c, HEAD_DIM), lambda h: (0, COL_K + h // GQA_GROUP)),
            pl.BlockSpec((c, HEAD_DIM), lambda h: (0, COL_V + h // GQA_GROUP)),
            hb(COL_RQ), hb(COL_RK), hb(COL_RV), hb(COL_G),
            pl.BlockSpec((None, 1, HEAD_DIM), lambda h: (h, 0, 0)),
        ],
        out_specs=[
            pl.BlockSpec((c, HEAD_DIM), lambda h: (0, h)),
            pl.BlockSpec((c, HEAD_DIM), lambda h: (0, h)),
            pl.BlockSpec((None, HEAD_DIM, HEAD_DIM), lambda h: (h, 0, 0)),
            pl.BlockSpec((None, HEAD_DIM, HEAD_DIM), lambda h: (h, 0, 0)),
        ],
        out_shape=[
            jax.ShapeDtypeStruct((c, ATT_W), BF16),
            jax.ShapeDtypeStruct((c, RET_W), BF16),
            jax.ShapeDtypeStruct((N_RET_HEADS, HEAD_DIM, HEAD_DIM), F32),
            jax.ShapeDtypeStruct((N_RET_HEADS, HEAD_DIM, HEAD_DIM), F32),
        ],
        compiler_params=_cparams("arbitrary"),
        name="ctx_mixer",
    )(sink, log_g, p_ctx, p_ctx, p_ctx, p_ctx, p_ctx, p_ctx, p_ctx,
      gn_g.reshape(N_RET_HEADS, 1, HEAD_DIM))


RET_KERNEL_CHUNK = 256
RET_UNROLL = 8
RET_UNROLL_BWD = 8


def _ret_kernel(lg_ref, q_ref, k_ref, v_ref, g_ref, sf_ref, sb_ref, gn_ref, o_ref,
                y_scr, dmat, qdf, kdf, qdb, kdb, cdf, cdb):
    h = pl.program_id(0)
    ck = RET_KERNEL_CHUNK
    nit = q_ref.shape[0] // (ck * RET_UNROLL)
    lg_f = lg_ref[0, h]
    lg_b = lg_ref[1, h]
    ii = lax.broadcasted_iota(jnp.int32, (ck, ck), 0)
    jj = lax.broadcasted_iota(jnp.int32, (ck, ck), 1)
    rel = (ii - jj).astype(F32)
    dmat[...] = (jnp.where(rel >= 0, jnp.exp(jnp.maximum(rel, 0.0) * lg_f), 0.0)
                 + jnp.where(rel <= 0, jnp.exp(jnp.maximum(-rel, 0.0) * lg_b), 0.0))
    pos = lax.broadcasted_iota(jnp.int32, (ck, HEAD_DIM), 0).astype(F32)
    qdf[...] = jnp.exp((pos + 1.0) * lg_f)
    kdf[...] = jnp.exp((ck - 1.0 - pos) * lg_f)
    qdb[...] = jnp.exp((ck - pos) * lg_b)
    kdb[...] = jnp.exp(pos * lg_b)
    full = jnp.full((HEAD_DIM, HEAD_DIM), float(ck), F32)
    cdf[...] = jnp.exp(full * lg_f)
    cdb[...] = jnp.exp(full * lg_b)

    def rows_of(z):
        return pl.ds(pl.multiple_of(z * ck, ck), ck)

    def fwd(it, state):
        zs = [it * RET_UNROLL + u for u in range(RET_UNROLL)]
        qs = [q_ref[rows_of(z), :] for z in zs]
        ks = [k_ref[rows_of(z), :] for z in zs]
        vs = [v_ref[rows_of(z), :] for z in zs]
        inner = [_dot((_dot_nt(q, k) * dmat[...]).astype(BF16), v) for q, k, v in zip(qs, ks, vs)]
        kvs = [_dot_tn((k.astype(F32) * kdf[...]).astype(BF16), v) for k, v in zip(ks, vs)]
        for u, z in enumerate(zs):
            y_scr[rows_of(z), :] = inner[u] + _dot(qs[u], state.astype(BF16)) * qdf[...]
            state = cdf[...] * state + kvs[u]
        return state

    lax.fori_loop(0, nit, fwd, sf_ref[...])

    nit_b = q_ref.shape[0] // (ck * RET_UNROLL_BWD)

    def bwd(it, state):
        zs = [nit_b * RET_UNROLL_BWD - 1 - (it * RET_UNROLL_BWD + u) for u in range(RET_UNROLL_BWD)]
        qs = [q_ref[rows_of(z), :] for z in zs]
        kvs = [_dot_tn((k_ref[rows_of(z), :].astype(F32) * kdb[...]).astype(BF16), v_ref[rows_of(z), :])
               for z in zs]
        for u, z in enumerate(zs):
            y = y_scr[rows_of(z), :] + _dot(qs[u], state.astype(BF16)) * qdb[...]
            state = cdb[...] * state + kvs[u]
            mu = jnp.mean(y, axis=-1, keepdims=True)
            var = jnp.mean(jnp.square(y - mu), axis=-1, keepdims=True)
            yn = (y - mu) * lax.rsqrt(var + EPS) * gn_ref[...]
            o_ref[rows_of(z), :] = (_silu(g_ref[rows_of(z), :].astype(F32)) * yn).astype(o_ref.dtype)
        return state

    lax.fori_loop(0, nit_b, bwd, sb_ref[...])


def _retention(p_lat, log_g, s_f, s_b, gn_g):
    s = p_lat.shape[0]
    ck = RET_KERNEL_CHUNK
    assert s % (ck * RET_UNROLL) == 0 and s % (ck * RET_UNROLL_BWD) == 0
    col = lambda off: pl.BlockSpec((s, HEAD_DIM), lambda h: (0, off + h))
    state_spec = pl.BlockSpec((None, HEAD_DIM, HEAD_DIM), lambda h: (h, 0, 0))
    vec = pltpu.VMEM((ck, HEAD_DIM), F32)
    sq = pltpu.VMEM((HEAD_DIM, HEAD_DIM), F32)
    return pl.pallas_call(
        _ret_kernel,
        grid=(N_RET_HEADS,),
        in_specs=[pl.BlockSpec(memory_space=pltpu.SMEM),
                  col(COL_RQ), col(COL_RK), col(COL_RV), col(COL_G), state_spec, state_spec,
                  pl.BlockSpec((None, 1, HEAD_DIM), lambda h: (h, 0, 0))],
        out_specs=col(0),
        out_shape=jax.ShapeDtypeStruct((s, RET_W), BF16),
        scratch_shapes=[pltpu.VMEM((s, HEAD_DIM), F32), pltpu.VMEM((ck, ck), F32),
                        vec, vec, vec, vec, sq, sq],
        compiler_params=_cparams("arbitrary"),
        name="retention",
    )(log_g, p_lat, p_lat, p_lat, p_lat, s_f, s_b, gn_g.reshape(N_RET_HEADS, 1, HEAD_DIM))


def _outproj_kernel(x_ref, a_ref, r_ref, wa_ref, wr_ref, gate_ref, o_ref, *, row):
    y = _dot(a_ref[...], wa_ref[...]) + _dot(r_ref[...], wr_ref[...])
    o_ref[...] = x_ref[...] + gate_ref[row:row + 1, :] * y


def _out_projection(x, att, ret, w_out, mods, layer, row, tm):
    m, d = x.shape
    return pl.pallas_call(
        functools.partial(_outproj_kernel, row=row),
        grid=(m // tm,),
        in_specs=[
            pl.BlockSpec((tm, d), lambda i: (i, 0)),
            pl.BlockSpec((tm, ATT_W), lambda i: (i, 0)),
            pl.BlockSpec((tm, RET_W), lambda i: (i, 0)),
            pl.BlockSpec((None, ATT_W, d), lambda i: (layer, 0, 0)),
            pl.BlockSpec((None, RET_W, d), lambda i: (layer, ATT_W // RET_W, 0)),
            pl.BlockSpec((None, 8, d), lambda i: (layer, 0, 2)),
        ],
        out_specs=pl.BlockSpec((tm, d), lambda i: (i, 0)),
        out_shape=jax.ShapeDtypeStruct((m, d), F32),
        compiler_params=_cparams("arbitrary"),
        name="out_projection",
    )(x, att, ret, w_out, w_out, mods)


def _swiglu_accumulate(h_scr, w1_ref, w3_ref, w2_ref, o_ref, groups):
    w1 = w1_ref[...].astype(BF16)
    w3 = w3_ref[...].astype(BF16)
    w2 = w2_ref[...].astype(BF16)
    us = []
    for lo, n in groups:
        h = h_scr[lo:lo + n, :]
        us.append((_silu(_dot(h, w1)) * _dot(h, w3)).astype(BF16))
    for (lo, n), u in zip(groups, us):
        o_ref[lo:lo + n, :] += _dot(u, w2)


def _ffn_kernel(x_ref, sh_ref, sc_ref, gate_ref, g2_ref, w1_ref, w3_ref, w2_ref, o_ref, h_scr,
                *, row, n_parts):
    j = pl.program_id(1)
    part = x_ref.shape[0] // n_parts

    @pl.when(j == 0)
    def _():
        h = _norm_modulate(x_ref[...], g2_ref[...], sh_ref[row:row + 1, :], sc_ref[row:row + 1, :])
        h_scr[...] = h.astype(BF16)
        o_ref[...] = jnp.zeros_like(o_ref)

    _swiglu_accumulate(h_scr, w1_ref, w3_ref, w2_ref, o_ref, tuple((p * part, part) for p in range(n_parts)))

    @pl.when(j == pl.num_programs(1) - 1)
    def _():
        o_ref[...] = x_ref[...] + gate_ref[row:row + 1, :] * o_ref[...]


def _dense_ffn(x, mods, layer, row, g2, w1, w3, w2, tm, tf):
    m, d = x.shape
    f = w1.shape[1]
    mod = lambda k: pl.BlockSpec((None, 8, d), lambda i, j: (layer, 0, k))
    return pl.pallas_call(
        functools.partial(_ffn_kernel, row=row, n_parts=2 if tm >= 1024 else 1),
        grid=(m // tm, f // tf),
        in_specs=[
            pl.BlockSpec((tm, d), lambda i, j: (i, 0)),
            mod(3), mod(4), mod(5),
            pl.BlockSpec((1, d), lambda i, j: (0, 0)),
            pl.BlockSpec((d, tf), lambda i, j: (0, j)),
            pl.BlockSpec((d, tf), lambda i, j: (0, j)),
            pl.BlockSpec((tf, d), lambda i, j: (j, 0)),
        ],
        out_specs=pl.BlockSpec((tm, d), lambda i, j: (i, 0)),
        out_shape=jax.ShapeDtypeStruct((m, d), F32),
        scratch_shapes=[pltpu.VMEM((tm, d), BF16)],
        compiler_params=_cparams("arbitrary", "arbitrary"),
        name="dense_ffn",
    )(x, mods, mods, mods, g2, w1, w3, w2)


def _route_top2(logits, carry, live):
    tm = logits.shape[0]
    lane = lax.broadcasted_iota(jnp.int32, (tm, LANES), 1).astype(F32)
    v1 = jnp.max(logits, axis=-1, keepdims=True)
    e1 = jnp.min(jnp.where(logits == v1, lane, float(LANES)), axis=-1, keepdims=True)
    rest = jnp.where(lane == e1, NEG_INF, logits)
    v2 = jnp.max(rest, axis=-1, keepdims=True)
    e2 = jnp.min(jnp.where(rest == v2, lane, float(LANES)), axis=-1, keepdims=True)
    t = jnp.exp(v2 - v1)
    w1 = 1.0 / (1.0 + t)
    w2 = t / (1.0 + t)
    oh1 = jnp.where(lane == e1, 1.0, 0.0)
    oh2 = jnp.where(lane == e2, 1.0, 0.0)
    oh = oh1 + oh2
    r_i = lax.broadcasted_iota(jnp.int32, (tm, tm), 0)
    c_i = lax.broadcasted_iota(jnp.int32, (tm, tm), 1)
    tri = jnp.where(c_i < r_i, 1.0, 0.0).astype(BF16)
    before = _dot(tri, oh.astype(BF16)) + carry[0:1, :]
    rank1 = jnp.sum(before * oh1, axis=-1, keepdims=True)
    rank2 = jnp.sum(before * oh2, axis=-1, keepdims=True)
    carry[...] = carry[...] + live * jnp.sum(oh, axis=0, keepdims=True)
    info = jnp.where(lane == 0.0, e1, 0.0)
    info = jnp.where(lane == 1.0, e2, info)
    info = jnp.where(lane == 2.0, rank1, info)
    info = jnp.where(lane == 3.0, rank2, info)
    info = jnp.where(lane == 4.0, w1, info)
    info = jnp.where(lane == 5.0, w2, info)
    return info


def _outproj_router_kernel(x_ref, a_ref, r_ref, wa_ref, wr_ref, gate_ref, sh_ref, sc_ref, g2_ref, rw_ref, rb_ref,
                           o_ref, h_ref, info_ref, cnt_ref, xprev, carry, *, row):
    t = pl.program_id(0)
    tm, d = x_ref.shape

    @pl.when(t == 0)
    def _():
        xprev[...] = jnp.zeros_like(xprev)
        carry[...] = jnp.zeros_like(carry)

    y = _dot(a_ref[...], wa_ref[...]) + _dot(r_ref[...], wr_ref[...])
    x_new = x_ref[...] + gate_ref[row:row + 1, :] * y
    o_ref[...] = x_new

    h_ref[...] = _norm_modulate(xprev[...], g2_ref[...], sh_ref[row:row + 1, :], sc_ref[row:row + 1, :])
    lane = lax.broadcasted_iota(jnp.int32, (tm, LANES), 1).astype(F32)
    parts = [None] * N_EXPERTS
    for b in range(d // LANES):
        cols = slice(b * LANES, (b + 1) * LANES)
        hb = h_ref[:, cols]
        for e in range(N_EXPERTS):
            term = hb * rw_ref[e:e + 1, cols]
            parts[e] = term if parts[e] is None else parts[e] + term
    logits = jnp.full((tm, LANES), NEG_INF, F32)
    for e in range(N_EXPERTS):
        logit_e = jnp.sum(parts[e], axis=-1, keepdims=True) + rb_ref[0:1, e:e + 1]
        logits = jnp.where(lane == float(e), logit_e, logits)
    info_ref[...] = _route_top2(logits, carry, jnp.where(t > 0, 1.0, 0.0))
    cnt_ref[...] = carry[...]
    xprev[...] = x_new


def _out_projection_router(x, att, ret, w_out, mods, layer, row, g2, router_w, router_b, tm):
    m, d = x.shape
    n = m // tm
    rw_t = jnp.pad(router_w.astype(F32).T, ((0, 8 - N_EXPERTS), (0, 0)))
    rb_pad = jnp.pad(router_b.astype(F32), (0, LANES - N_EXPERTS)).reshape(1, LANES)
    cur = lambda t: (jnp.minimum(t, n - 1), 0)
    prv = lambda t: (jnp.maximum(t - 1, 0), 0)
    mod = lambda k: pl.BlockSpec((None, 8, d), lambda t: (layer, 0, k))
    once = pl.Buffered(1)
    return pl.pallas_call(
        functools.partial(_outproj_router_kernel, row=row),
        grid=(n + 1,),
        in_specs=[
            pl.BlockSpec((tm, d), cur),
            pl.BlockSpec((tm, ATT_W), cur),
            pl.BlockSpec((tm, RET_W), cur),
            pl.BlockSpec((None, ATT_W, d), lambda t: (layer, 0, 0), pipeline_mode=once),
            pl.BlockSpec((None, RET_W, d), lambda t: (layer, ATT_W // RET_W, 0), pipeline_mode=once),
            mod(2), mod(3), mod(4),
            pl.BlockSpec((1, d), lambda t: (0, 0)),
            pl.BlockSpec((8, d), lambda t: (0, 0)),
            pl.BlockSpec((1, LANES), lambda t: (0, 0)),
        ],
        out_specs=[
            pl.BlockSpec((tm, d), cur),
            pl.BlockSpec((tm, d), prv),
            pl.BlockSpec((tm, LANES), prv),
            pl.BlockSpec((8, LANES), lambda t: (0, 0)),
        ],
        out_shape=[
            jax.ShapeDtypeStruct((m, d), F32),
            jax.ShapeDtypeStruct((m, d), F32),
            jax.ShapeDtypeStruct((m, LANES), F32),
            jax.ShapeDtypeStruct((8, LANES), F32),
        ],
        scratch_shapes=[pltpu.VMEM((tm, d), F32), pltpu.VMEM((8, LANES), F32)],
        compiler_params=_cparams("arbitrary"),
        name="out_projection_router",
    )(x, att, ret, w_out, w_out, mods, mods, mods, g2, rw_t, rb_pad)


GATHER_UNROLL = 8


def _gather_rows(idx_ref, src_hbm, dst, sem, n):
    def body(kb, carry):
        for u in range(GATHER_UNROLL):
            k = kb * GATHER_UNROLL + u
            pltpu.make_async_copy(src_hbm.at[pl.ds(idx_ref[0, 0, k], 1)], dst.at[pl.ds(k, 1)], sem).start()
        return carry
    lax.fori_loop(0, n // GATHER_UNROLL, body, 0)


def _wait_rows(src_hbm, dst, sem, n):
    pltpu.make_async_copy(src_hbm.at[pl.ds(0, n)], dst, sem).wait()


def _issue_rows(idx_ref, src_hbm, dst, sem, start, count):
    for u in range(count):
        k = start + u
        pltpu.make_async_copy(src_hbm.at[pl.ds(idx_ref[0, 0, k], 1)], dst.at[pl.ds(k, 1)], sem).start()


def _moe_ffn_kernel(te_ref, nu_ref, nv_ref, cur_ref, nxt_ref, h_hbm, w1_ref, w3_ref, w2_ref, o_ref,
                    buf, sem, h_scr, *, tm, n_f):
    r = pl.program_id(0)
    j = pl.program_id(1)
    n_used = nu_ref[0]
    half = tm // 2
    quarter = tm // 4
    per_step = tm // n_f
    head = tm - per_step * n_f

    @pl.when(r < n_used)
    def _():
        @pl.when(j == 0)
        def _():
            @pl.when(r == 0)
            def _():
                _gather_rows(cur_ref, h_hbm, buf, sem, tm)

            _wait_rows(h_hbm, buf, sem, tm)
            h_scr[...] = buf[...].astype(BF16)

            @pl.when(r + 1 < n_used)
            def _():
                _issue_rows(nxt_ref, h_hbm, buf, sem, 0, head)

            o_ref[...] = jnp.zeros_like(o_ref)

        more = r + 1 < n_used
        quarters = (nv_ref[r] + quarter - 1) // quarter
        row_groups = {1: ((0, quarter),), 2: ((0, half),), 3: ((0, half), (half, quarter)),
                      4: ((0, half), (half, half))}

        def case(fetch, groups):
            if fetch:
                _issue_rows(nxt_ref, h_hbm, buf, sem, head + j * per_step, per_step)
            _swiglu_accumulate(h_scr, w1_ref, w3_ref, w2_ref, o_ref, groups)

        for nq, groups in row_groups.items():
            pl.when(more & (quarters == nq))(functools.partial(case, True, groups))
            pl.when(jnp.logical_not(more) & (quarters == nq))(functools.partial(case, False, groups))

    @pl.when((r >= n_used) & (j == 0))
    def _():
        o_ref[...] = jnp.zeros_like(o_ref)


def _moe_ffn(h2, tile_expert, n_used, n_valid, src_rows, w1, w3, w2, tm, tf):
    d = h2.shape[1]
    f = w1.shape[2]
    n_tiles = src_rows.shape[0]
    n_f = f // tf
    clamp = lambda r, nu: jnp.maximum(jnp.minimum(r, nu[0] - 1), 0)
    fcol = lambda r, j, nu: jnp.where(r < nu[0], j, n_f - 1)
    grid_spec = pltpu.PrefetchScalarGridSpec(
        num_scalar_prefetch=3,
        grid=(n_tiles, n_f),
        in_specs=[
            pl.BlockSpec((1, 1, tm), lambda r, j, te, nu, nv: (clamp(r, nu), 0, 0), memory_space=pltpu.SMEM),
            pl.BlockSpec((1, 1, tm), lambda r, j, te, nu, nv: (clamp(r + 1, nu), 0, 0), memory_space=pltpu.SMEM),
            pl.BlockSpec(memory_space=pl.ANY),
            pl.BlockSpec((None, d, tf), lambda r, j, te, nu, nv: (te[clamp(r, nu)], 0, fcol(r, j, nu))),
            pl.BlockSpec((None, d, tf), lambda r, j, te, nu, nv: (te[clamp(r, nu)], 0, fcol(r, j, nu))),
            pl.BlockSpec((None, tf, d), lambda r, j, te, nu, nv: (te[clamp(r, nu)], fcol(r, j, nu), 0)),
        ],
        out_specs=pl.BlockSpec((tm, d), lambda r, j, te, nu, nv: (r, 0)),
        scratch_shapes=[
            pltpu.VMEM((tm, d), F32),
            pltpu.SemaphoreType.DMA(()),
            pltpu.VMEM((tm, d), BF16),
        ],
    )
    return pl.pallas_call(
        functools.partial(_moe_ffn_kernel, tm=tm, n_f=n_f),
        grid_spec=grid_spec,
        out_shape=jax.ShapeDtypeStruct((n_tiles * tm, d), F32),
        compiler_params=_cparams("arbitrary", "arbitrary"),
        name="moe_ffn",
    )(tile_expert, n_used, n_valid, src_rows, src_rows, h2, w1, w3, w2)


def _combine_kernel(cur_ref, nxt_ref, x_ref, info_ref, gate_ref, y_hbm, o_ref, buf, sems, *, tm, row):
    i = pl.program_id(0)
    slot = i % 2

    @pl.when(i == 0)
    def _():
        _gather_rows(cur_ref, y_hbm, buf.at[0], sems.at[0], 2 * tm)

    _wait_rows(y_hbm, buf.at[slot], sems.at[slot], 2 * tm)

    @pl.when(i + 1 < pl.num_programs(0))
    def _():
        _gather_rows(nxt_ref, y_hbm, buf.at[1 - slot], sems.at[1 - slot], 2 * tm)

    info = info_ref[...]
    w1 = info[:, 4:5]
    w2 = info[:, 5:6]
    y = w1 * buf[slot, 0:tm, :] + w2 * buf[slot, tm:2 * tm, :]
    o_ref[...] = x_ref[...] + gate_ref[row:row + 1, :] * y


def _moe_combine(x, info, mods, layer, row, y_rows, pos_tiles, tm):
    m, d = x.shape
    nt = m // tm
    return pl.pallas_call(
        functools.partial(_combine_kernel, tm=tm, row=row),
        grid=(nt,),
        in_specs=[
            pl.BlockSpec((1, 1, 2 * tm), lambda i: (i, 0, 0), memory_space=pltpu.SMEM),
            pl.BlockSpec((1, 1, 2 * tm), lambda i: (jnp.minimum(i + 1, nt - 1), 0, 0), memory_space=pltpu.SMEM),
            pl.BlockSpec((tm, d), lambda i: (i, 0)),
            pl.BlockSpec((tm, LANES), lambda i: (i, 0)),
            pl.BlockSpec((None, 8, d), lambda i: (layer, 0, 5)),
            pl.BlockSpec(memory_space=pl.ANY),
        ],
        out_specs=pl.BlockSpec((tm, d), lambda i: (i, 0)),
        out_shape=jax.ShapeDtypeStruct((m, d), F32),
        scratch_shapes=[pltpu.VMEM((2, 2 * tm, d), F32), pltpu.SemaphoreType.DMA((2,))],
        compiler_params=_cparams("arbitrary"),
        name="moe_combine",
    )(pos_tiles, pos_tiles, x, info, mods, y_rows)


def _mix_and_moe(x_in, att, ret, w_out, mods, layer, row, g2, router_w, router_b, w1, w3, w2,
                 tm_route, tm_exp, tf):
    m, d = x_in.shape
    x, h2, info, cnt = _out_projection_router(x_in, att, ret, w_out, mods, layer, row, g2,
                                              router_w, router_b, tm_route)
    counts = cnt[0, :N_EXPERTS].astype(jnp.int32)
    tiles_per = (counts + tm_exp - 1) // tm_exp
    tile_end = jnp.cumsum(tiles_per)
    group_start = (tile_end - tiles_per) * tm_exp
    n_tiles = (2 * m) // tm_exp + N_EXPERTS
    n_used = tile_end[-1:].astype(jnp.int32)
    tile_ids = jnp.arange(n_tiles, dtype=jnp.int32)
    tile_expert = jnp.minimum(jnp.sum(tile_end[None, :] <= tile_ids[:, None], axis=1),
                              N_EXPERTS - 1).astype(jnp.int32)
    rows_before = (tile_ids - (tile_end - tiles_per)[tile_expert]) * tm_exp
    n_valid = jnp.clip(counts[tile_expert] - rows_before, 0, tm_exp).astype(jnp.int32)
    e12 = info[:, 0:2].astype(jnp.int32)
    rank12 = info[:, 2:4].astype(jnp.int32)
    pos = group_start[e12] + rank12
    tok = jnp.broadcast_to(jnp.arange(m, dtype=jnp.int32)[:, None], (m, 2))
    src_rows = jnp.zeros((n_tiles * tm_exp,), jnp.int32).at[pos.reshape(-1)].set(
        tok.reshape(-1), unique_indices=True, mode="promise_in_bounds")
    y_rows = _moe_ffn(h2, tile_expert, n_used, n_valid, src_rows.reshape(n_tiles, 1, tm_exp),
                      w1, w3, w2, tm_exp, tf)
    nt = m // tm_route
    pos_tiles = pos.reshape(nt, tm_route, 2).transpose(0, 2, 1).reshape(nt, 1, 2 * tm_route)
    return _moe_combine(x, info, mods, layer, row, y_rows, pos_tiles, tm_route)


def _rope_tables(n_tokens):
    n_rows = n_tokens // GRID_W
    inv = ROPE_THETA ** (-jnp.arange(ROPE_PAIRS, dtype=F32) / ROPE_PAIRS)
    ang_r = jnp.arange(n_rows).astype(F32)[:, None] * inv
    ang_c = jnp.arange(GRID_W).astype(F32)[:, None] * inv
    cos_r = jnp.repeat(jnp.cos(ang_r), GRID_W, axis=0)
    sin_r = jnp.repeat(jnp.sin(ang_r), GRID_W, axis=0)
    cos_c = jnp.tile(jnp.cos(ang_c), (n_rows, 1))
    sin_c = jnp.tile(jnp.sin(ang_c), (n_rows, 1))
    zero = jnp.zeros_like(sin_r)
    cos_t = jnp.concatenate([cos_r, cos_r, cos_c, cos_c], axis=-1)
    sa_t = jnp.concatenate([-sin_r, zero, -sin_c, zero], axis=-1)
    sb_t = jnp.concatenate([zero, sin_r, zero, sin_c], axis=-1)
    return cos_t, sa_t, sb_t


def kernel(x, c, ctx, c_ctx, ada_w, ada_b, norm1_g, norm2_g, w_in, q_norm_g, k_norm_g, attn_sink,
           ret_decay, ret_gn_g, w_out, ffn_w1, ffn_w3, ffn_w2, router_w, router_b, moe_w1, moe_w3, moe_w2):
    b, s, d = x.shape
    n_ctx = ctx.shape[1]
    depth = ada_w.shape[0]
    assert b == 1, "one latent sequence per call"
    x_lat = x[0]
    x_ctx = ctx[0]

    tm_lat = min(1024, s)
    tm_ffn = min(512, s)
    tq = min(512, s)
    tf_ffn = 512
    tm_moe = 1024
    tf_moe = 256

    cond = jnp.stack([c[0], c_ctx]).astype(F32)
    mods = _ada_mods(cond, ada_w, ada_b)
    cos_l, sa_l, sb_l = _rope_tables(s)
    cos_c = jnp.ones((n_ctx, HEAD_DIM), F32)
    zer_c = jnp.zeros((n_ctx, HEAD_DIM), F32)
    log_g_all = jax.nn.log_sigmoid(ret_decay.astype(F32))

    w_in_i = w_in.astype(BF16)
    w_out_i = w_out.astype(BF16)

    for i in range(depth):
        last = i == depth - 1
        g1 = norm1_g[i].reshape(1, d)
        g2 = norm2_g[i].reshape(1, d)
        log_g = log_g_all[i]
        gain = jnp.concatenate([
            jnp.tile(q_norm_g[i].astype(F32) * ATT_SCALE, N_ATT_HEADS),
            jnp.tile(k_norm_g[i].astype(F32), N_KV_HEADS),
            jnp.ones((KV_W + RET_W,), F32),
            jnp.full((RET_W,), ATT_SCALE, F32),
            jnp.ones((2 * RET_W,), F32)]).reshape(1, IN_COLS)

        p_lat = _in_projection(x_lat, mods, i, 0, g1, w_in_i, gain, cos_l, sa_l, sb_l, tm_lat)
        p_ctx = _in_projection(x_ctx, mods, i, 1, g1, w_in_i, gain, cos_c, zer_c, zer_c, n_ctx)

        att_c, ret_c, s_f, s_b = _ctx_mixer(p_ctx, attn_sink[i], log_g, ret_gn_g[i])
        att_l = _window_attention(p_lat, p_ctx, attn_sink[i], tq)
        ret_l = _retention(p_lat, log_g, s_f, s_b, ret_gn_g[i])
        j = i // 2
        if i % 2 == 0:
            w1, w3, w2 = ffn_w1[j].astype(BF16), ffn_w3[j].astype(BF16), ffn_w2[j].astype(BF16)
            x_lat = _out_projection(x_lat, att_l, ret_l, w_out_i, mods, i, 0, tm_ffn)
            x_lat = _dense_ffn(x_lat, mods, i, 0, g2, w1, w3, w2, tm_ffn, tf_ffn)
        else:
            x_lat = _mix_and_moe(x_lat, att_l, ret_l, w_out_i, mods, i, 0, g2, router_w[j], router_b[j],
                                 moe_w1[j], moe_w3[j], moe_w2[j], tm_ffn, min(tm_moe, s), tf_moe)

        if not last:
            if i % 2 == 0:
                x_ctx = _out_projection(x_ctx, att_c, ret_c, w_out_i, mods, i, 1, n_ctx)
                x_ctx = _dense_ffn(x_ctx, mods, i, 1, g2, w1, w3, w2, n_ctx, tf_ffn)
            else:
                x_ctx = _mix_and_moe(x_ctx, att_c, ret_c, w_out_i, mods, i, 1, g2, router_w[j], router_b[j],
                                     moe_w1[j], moe_w3[j], moe_w2[j], n_ctx, n_ctx, tf_moe)
    return x_lat[None]
```

```python
import functools

import jax
import jax.numpy as jnp
from jax import lax
from jax.experimental import pallas as pl
from jax.experimental.pallas import tpu as pltpu

F32 = jnp.float32
BF16 = jnp.bfloat16

HEAD_DIM = 128
N_ATT_HEADS = 8
N_KV_HEADS = 2
N_RET_HEADS = 8
GQA_GROUP = N_ATT_HEADS // N_KV_HEADS
ATT_W = N_ATT_HEADS * HEAD_DIM
KV_W = N_KV_HEADS * HEAD_DIM
RET_W = N_RET_HEADS * HEAD_DIM
IN_COLS = ATT_W + 2 * KV_W + 4 * RET_W
WINDOW = 128
RET_CHUNK = 128
GRID_W = 64
ROPE_THETA = 10000.0
ROPE_PAIRS = HEAD_DIM // 4
N_EXPERTS = 8
EPS = 1e-6
ATT_SCALE = HEAD_DIM ** -0.5

COL_Q = 0
COL_K = ATT_W // HEAD_DIM
COL_V = COL_K + N_KV_HEADS
COL_RQ = COL_V + N_KV_HEADS
COL_RK = COL_RQ + N_RET_HEADS
COL_RV = COL_RK + N_RET_HEADS
COL_G = COL_RV + N_RET_HEADS

VMEM_LIMIT_BYTES = 56 * 1024 * 1024
LANES = 128

NEG_INF = float("-inf")


def _cparams(*sem):
    return pltpu.CompilerParams(dimension_semantics=sem, vmem_limit_bytes=VMEM_LIMIT_BYTES)


def _silu(x):
    return x * (1.0 / (1.0 + jnp.exp(-x)))


def _dot(a, b):
    return jnp.dot(a, b, preferred_element_type=F32)


def _dot_nt(a, b):
    return lax.dot_general(a, b, (((1,), (1,)), ((), ())), preferred_element_type=F32)


def _dot_tn(a, b):
    return lax.dot_general(a, b, (((0,), (0,)), ((), ())), preferred_element_type=F32)


def _norm_modulate(x, g, shift, scale):
    ms = jnp.mean(x * x, axis=-1, keepdims=True)
    return (x * lax.rsqrt(ms + EPS) * g) * (1.0 + scale) + shift


N_STREAMS = 2


def _ada_kernel(c_ref, w_ref, b_ref, o_ref):
    tn = w_ref.shape[1]
    w = w_ref[...]
    rows = [jnp.sum(w * jnp.tile(_silu(c_ref[r]), (1, tn // LANES)), axis=0, keepdims=True)
            for r in range(N_STREAMS)]
    o_ref[...] = jnp.concatenate(rows + [jnp.zeros((8 - N_STREAMS, tn), F32)], axis=0) + b_ref[...]


def _ada_mods(cond, ada_w, ada_b):
    depth, d, n = ada_w.shape
    tn = d // 2
    cond_lanes = jnp.broadcast_to(cond[:, :, None], (N_STREAMS, d, LANES))
    return pl.pallas_call(
        _ada_kernel,
        grid=(depth, n // tn),
        in_specs=[
            pl.BlockSpec((N_STREAMS, d, LANES), lambda l, j: (0, 0, 0)),
            pl.BlockSpec((None, d, tn), lambda l, j: (l, 0, j)),
            pl.BlockSpec((None, 1, tn), lambda l, j: (l, 0, j)),
        ],
        out_specs=pl.BlockSpec((None, 8, tn), lambda l, j: (l, 0, j)),
        out_shape=jax.ShapeDtypeStruct((depth, 8, n), F32),
        compiler_params=_cparams("arbitrary", "arbitrary"),
        name="ada_mods",
    )(cond_lanes, ada_w, ada_b.reshape(depth, 1, n))


def _rope(y, cos, sa, sb):
    return y * cos + pltpu.roll(y, 96, 1) * sa + pltpu.roll(y, 32, 1) * sb


def _inproj_kernel(x_ref, sh_ref, sc_ref, g1_ref, w_ref, gain_ref, rowtab_ref, coltab_ref, o_ref,
                   h_scr, p_scr, cos_ref, sa_ref, sb_ref, *, row, tn, n_j, n_steps):
    t = pl.program_id(0)
    j = t % n_j
    nchunk = tn // HEAD_DIM

    @pl.when(t == 0)
    def _():
        p_scr[...] = jnp.zeros_like(p_scr)

    @pl.when((t == 0) | ((t - 1) % n_j == 0))
    def _():
        for k, tab in enumerate((cos_ref, sa_ref, sb_ref)):
            for gr in range(rowtab_ref.shape[1]):
                tab[gr * GRID_W:(gr + 1) * GRID_W, :] = rowtab_ref[k, gr:gr + 1, :] + coltab_ref[k]

    @pl.when((j == 0) & (t < n_steps))
    def _():
        h = _norm_modulate(x_ref[...], g1_ref[...], sh_ref[row:row + 1, :], sc_ref[row:row + 1, :])
        h_scr[...] = h.astype(BF16)

    p_new = _dot(h_scr[...], w_ref[...].astype(BF16))

    jp = (t + n_j - 1) % n_j
    tabs = None
    for c in range(nchunk):
        gc = jp * nchunk + c
        norm_on = gc < COL_V
        rope_on = norm_on | ((gc >= COL_RQ) & (gc < COL_RV))
        if c % N_KV_HEADS == 0:
            tabs = (jnp.where(rope_on, cos_ref[...], 1.0), jnp.where(rope_on, sa_ref[...], 0.0),
                    jnp.where(rope_on, sb_ref[...], 0.0))
        cols = slice(c * HEAD_DIM, (c + 1) * HEAD_DIM)
        pc = p_scr[:, cols]
        r = lax.rsqrt(jnp.mean(pc * pc, axis=-1, keepdims=True) + EPS)
        y = pc * jnp.where(norm_on, r, 1.0) * gain_ref[:, cols]
        o_ref[:, cols] = _rope(y, *tabs).astype(o_ref.dtype)
    p_scr[...] = p_new


def _in_projection(x, mods, layer, row, g1, w, gain, rowtab, coltab, tm):
    m, d = x.shape
    n = w.shape[2]
    tn = 2 * KV_W
    assert n == IN_COLS and ATT_W % tn == 0 and RET_W % tn == 0 and m % tm == 0 and tm % GRID_W == 0
    n_i, n_j = m // tm, n // tn
    n_steps = n_i * n_j
    prev = lambda t: jnp.maximum(t - 1, 0)
    tab = pltpu.VMEM((tm, HEAD_DIM), F32)
    return pl.pallas_call(
        functools.partial(_inproj_kernel, row=row, tn=tn, n_j=n_j, n_steps=n_steps),
        grid=(n_steps + 1,),
        in_specs=[
            pl.BlockSpec((tm, d), lambda t: (jnp.minimum(t // n_j, n_i - 1), 0)),
            pl.BlockSpec((None, 8, d), lambda t: (layer, 0, 0)),
            pl.BlockSpec((None, 8, d), lambda t: (layer, 0, 1)),
            pl.BlockSpec((1, d), lambda t: (0, 0)),
            pl.BlockSpec((None, d, tn), lambda t: (layer, 0, t % n_j)),
            pl.BlockSpec((1, tn), lambda t: (0, prev(t) % n_j)),
            pl.BlockSpec((3, tm // GRID_W, HEAD_DIM), lambda t: (0, prev(t) // n_j, 0)),
            pl.BlockSpec((3, GRID_W, HEAD_DIM), lambda t: (0, 0, 0)),
        ],
        out_specs=pl.BlockSpec((tm, tn), lambda t: (prev(t) // n_j, prev(t) % n_j)),
        out_shape=jax.ShapeDtypeStruct((m, n), BF16),
        scratch_shapes=[pltpu.VMEM((tm, d), BF16), pltpu.VMEM((tm, tn), F32), tab, tab, tab],
        compiler_params=_cparams("arbitrary"),
        name="in_projection",
    )(x, mods, mods, g1, w, gain, rowtab, coltab)


def _attn_kernel(sink_ref, q_ref, km_ref, vm_ref, kp_ref, vp_ref, kn_ref, vn_ref,
                 kc_ref, vc_ref, bias_ref, o_ref, kbuf, vbuf, vcbuf, *, tq):
    i = pl.program_id(0)
    last = pl.num_programs(0) - 1
    nsub = tq // WINDOW
    n_ctx_blk = kc_ref.shape[0] // HEAD_DIM
    kbuf[0:WINDOW] = kp_ref[...]
    kbuf[WINDOW:WINDOW + tq] = km_ref[...]
    kbuf[WINDOW + tq:] = kn_ref[...]
    for h in range(N_KV_HEADS):
        hs = slice(h * HEAD_DIM, (h + 1) * HEAD_DIM)
        va = slice(2 * h * HEAD_DIM, (2 * h + 1) * HEAD_DIM)
        vbuf[0:WINDOW, va] = vp_ref[:, hs]
        vbuf[WINDOW:WINDOW + tq, va] = vm_ref[:, hs]
        vbuf[WINDOW + tq:, va] = vn_ref[:, hs]

    @pl.when(i == 0)
    def _():
        for h in range(N_KV_HEADS):
            ones = slice((2 * h + 1) * HEAD_DIM, (2 * h + 2) * HEAD_DIM)
            vbuf[:, ones] = jnp.ones((vbuf.shape[0], HEAD_DIM), vbuf.dtype)
            vcbuf[:, ones] = jnp.ones((vcbuf.shape[0], HEAD_DIM), vcbuf.dtype)
            vcbuf[:, 2 * h * HEAD_DIM:(2 * h + 1) * HEAD_DIM] = vc_ref[:, h * HEAD_DIM:(h + 1) * HEAD_DIM]

    col = lax.broadcasted_iota(jnp.int32, (WINDOW, 3 * WINDOW), 1)
    for s in range(nsub):
        bias = bias_ref[...]
        if s == 0:
            bias = bias + jnp.where(col < WINDOW, jnp.where(i == 0, NEG_INF, 0.0), 0.0)
        if s == nsub - 1:
            bias = bias + jnp.where(col >= 2 * WINDOW, jnp.where(i == last, NEG_INF, 0.0), 0.0)
        bias4 = jnp.concatenate([bias] * GQA_GROUP, axis=0)
        for h in range(N_KV_HEADS):
            hs = slice(h * HEAD_DIM, (h + 1) * HEAD_DIM)
            q = jnp.concatenate(
                [q_ref[s * WINDOW:(s + 1) * WINDOW,
                       (h * GQA_GROUP + g) * HEAD_DIM:(h * GQA_GROUP + g + 1) * HEAD_DIM]
                 for g in range(GQA_GROUP)], axis=0)
            vs = slice(2 * h * HEAD_DIM, (2 * h + 2) * HEAD_DIM)
            kl = kbuf[s * WINDOW:(s + 3) * WINDOW, hs]
            vl = vbuf[s * WINDOW:(s + 3) * WINDOW, vs]
            s_loc = _dot_nt(q, kl) + bias4
            s_ctx = _dot_nt(q, kc_ref[:, hs])
            sink = jnp.concatenate(
                [jnp.full((WINDOW, 1), sink_ref[h * GQA_GROUP + g], F32) for g in range(GQA_GROUP)],
                axis=0)
            blocks = ([s_loc[:, b * HEAD_DIM:(b + 1) * HEAD_DIM] for b in range(3)]
                      + [s_ctx[:, b * HEAD_DIM:(b + 1) * HEAD_DIM] for b in range(n_ctx_blk)])
            m = jnp.maximum(jnp.max(functools.reduce(jnp.maximum, blocks), axis=-1, keepdims=True), sink)
            p_loc = jnp.exp(s_loc - m).astype(BF16)
            p_ctx = jnp.exp(s_ctx - m).astype(BF16)
            o_den = _dot(p_loc, vl) + _dot(p_ctx, vcbuf[:, vs])
            o = o_den[:, :HEAD_DIM] / (o_den[:, HEAD_DIM:] + jnp.exp(sink - m))
            for g in range(GQA_GROUP):
                o_ref[s * WINDOW:(s + 1) * WINDOW,
                      (h * GQA_GROUP + g) * HEAD_DIM:(h * GQA_GROUP + g + 1) * HEAD_DIM] = (
                    o[g * WINDOW:(g + 1) * WINDOW].astype(o_ref.dtype))


def _band_bias():
    r = jnp.arange(WINDOW)[:, None]
    c = jnp.arange(3 * WINDOW)[None, :]
    ok = (c - r >= 0) & (c - r <= 2 * WINDOW)
    return jnp.where(ok, 0.0, NEG_INF).astype(F32)


def _window_attention(p_lat, p_ctx, sink, tq):
    s = p_lat.shape[0]
    c = p_ctx.shape[0]
    r = tq // WINDOW
    nb = s // WINDOW
    kvb = KV_W // HEAD_DIM
    ck, cv = COL_K // kvb, COL_V // kvb
    return pl.pallas_call(
        functools.partial(_attn_kernel, tq=tq),
        grid=(s // tq,),
        in_specs=[
            pl.BlockSpec(memory_space=pltpu.SMEM),
            pl.BlockSpec((tq, ATT_W), lambda i: (i, 0)),
            pl.BlockSpec((tq, KV_W), lambda i: (i, ck)),
            pl.BlockSpec((tq, KV_W), lambda i: (i, cv)),
            pl.BlockSpec((WINDOW, KV_W), lambda i: (jnp.maximum(i * r - 1, 0), ck)),
            pl.BlockSpec((WINDOW, KV_W), lambda i: (jnp.maximum(i * r - 1, 0), cv)),
            pl.BlockSpec((WINDOW, KV_W), lambda i: (jnp.minimum((i + 1) * r, nb - 1), ck)),
            pl.BlockSpec((WINDOW, KV_W), lambda i: (jnp.minimum((i + 1) * r, nb - 1), cv)),
            pl.BlockSpec((c, KV_W), lambda i: (0, ck)),
            pl.BlockSpec((c, KV_W), lambda i: (0, cv)),
            pl.BlockSpec((WINDOW, 3 * WINDOW), lambda i: (0, 0)),
        ],
        out_specs=pl.BlockSpec((tq, ATT_W), lambda i: (i, 0)),
        out_shape=jax.ShapeDtypeStruct((s, ATT_W), BF16),
        scratch_shapes=[pltpu.VMEM((tq + 2 * WINDOW, KV_W), BF16),
                        pltpu.VMEM((tq + 2 * WINDOW, 2 * KV_W), BF16),
                        pltpu.VMEM((c, 2 * KV_W), BF16)],
        compiler_params=_cparams("arbitrary"),
        name="window_attention",
    )(sink, p_lat, p_lat, p_lat, p_lat, p_lat, p_lat, p_lat, p_ctx, p_ctx, _band_bias())


def _ctx_mixer_kernel(sink_ref, lg_ref, q_ref, k_ref, v_ref, rq_ref, rk_ref, rv_ref, g_ref, gn_ref,
                      att_ref, ret_ref, sf_ref, sb_ref):
    h = pl.program_id(0)
    c = q_ref.shape[0]
    lg_f = lg_ref[0, h]
    lg_b = lg_ref[1, h]
    s = _dot_nt(q_ref[...], k_ref[...])
    sink = jnp.full((c, 1), sink_ref[h], F32)
    m = jnp.maximum(jnp.max(s, axis=-1, keepdims=True), sink)
    p = jnp.exp(s - m)
    den = jnp.sum(p, axis=-1, keepdims=True) + jnp.exp(sink - m)
    att_ref[...] = (_dot(p.astype(BF16), v_ref[...]) / den).astype(att_ref.dtype)
    n_i = lax.broadcasted_iota(jnp.int32, (c, c), 0)
    m_i = lax.broadcasted_iota(jnp.int32, (c, c), 1)
    rel = (n_i - m_i).astype(F32)
    dec = (jnp.where(rel >= 0, jnp.exp(jnp.maximum(rel, 0.0) * lg_f), 0.0)
           + jnp.where(rel <= 0, jnp.exp(jnp.maximum(-rel, 0.0) * lg_b), 0.0))
    sc = _dot_nt(rq_ref[...], rk_ref[...]) * dec
    y = _dot(sc.astype(BF16), rv_ref[...])
    mu = jnp.mean(y, axis=-1, keepdims=True)
    var = jnp.mean(jnp.square(y - mu), axis=-1, keepdims=True)
    yn = (y - mu) * lax.rsqrt(var + EPS) * gn_ref[...]
    ret_ref[...] = (_silu(g_ref[...].astype(F32)) * yn).astype(ret_ref.dtype)
    pos = lax.broadcasted_iota(jnp.int32, (c, HEAD_DIM), 0).astype(F32)
    kf = rk_ref[...].astype(F32)
    k_f = (kf * jnp.exp((c - 1.0 - pos) * lg_f)).astype(BF16)
    k_b = (kf * jnp.exp(pos * lg_b)).astype(BF16)
    sf_ref[...] = _dot_tn(k_f, rv_ref[...])
    sb_ref[...] = _dot_tn(k_b, rv_ref[...])


def _ctx_mixer(p_ctx, sink, log_g, gn_g):
    c = p_ctx.shape[0]
    hb = lambda off: pl.BlockSpec((c, HEAD_DIM), lambda h: (0, off + h))
    return pl.pallas_call(
        _ctx_mixer_kernel,
        grid=(N_RET_HEADS,),
        in_specs=[
            pl.BlockSpec(memory_space=pltpu.SMEM),
            pl.BlockSpec(memory_space=pltpu.SMEM),
            hb(COL_Q),
            pl.BlockSpec((c, HEAD_DIM), lambda h: (0, COL_K + h // GQA_GROUP)),
            pl.BlockSpec((c, HEAD_DIM), lambda h: (0, COL_V + h // GQA_GROUP)),
            hb(COL_RQ), hb(COL_RK), hb(COL_RV), hb(COL_G),
            pl.BlockSpec((None, 1, HEAD_DIM), lambda h: (h, 0, 0)),
        ],
        out_specs=[
            pl.BlockSpec((c, HEAD_DIM), lambda h: (0, h)),
            pl.BlockSpec((c, HEAD_DIM), lambda h: (0, h)),
            pl.BlockSpec((None, HEAD_DIM, HEAD_DIM), lambda h: (h, 0, 0)),
            pl.BlockSpec((None, HEAD_DIM, HEAD_DIM), lambda h: (h, 0, 0)),
        ],
        out_shape=[
            jax.ShapeDtypeStruct((c, ATT_W), BF16),
            jax.ShapeDtypeStruct((c, RET_W), BF16),
            jax.ShapeDtypeStruct((N_RET_HEADS, HEAD_DIM, HEAD_DIM), F32),
            jax.ShapeDtypeStruct((N_RET_HEADS, HEAD_DIM, HEAD_DIM), F32),
        ],
        compiler_params=_cparams("arbitrary"),
        name="ctx_mixer",
    )(sink, log_g, p_ctx, p_ctx, p_ctx, p_ctx, p_ctx, p_ctx, p_ctx,
      gn_g.reshape(N_RET_HEADS, 1, HEAD_DIM))


RET_KERNEL_CHUNK = 256
RET_UNROLL = 8
RET_UNROLL_BWD = 8


def _ret_kernel(lg_ref, q_ref, k_ref, v_ref, g_ref, sf_ref, sb_ref, gn_ref, o_ref,
                y_scr, dmat, qdf, kdf, qdb, kdb, cdf, cdb):
    h = pl.program_id(0)
    ck = RET_KERNEL_CHUNK
    nit = q_ref.shape[0] // (ck * RET_UNROLL)
    lg_f = lg_ref[0, h]
    lg_b = lg_ref[1, h]
    ii = lax.broadcasted_iota(jnp.int32, (ck, ck), 0)
    jj = lax.broadcasted_iota(jnp.int32, (ck, ck), 1)
    rel = (ii - jj).astype(F32)
    dmat[...] = (jnp.where(rel >= 0, jnp.exp(jnp.maximum(rel, 0.0) * lg_f), 0.0)
                 + jnp.where(rel <= 0, jnp.exp(jnp.maximum(-rel, 0.0) * lg_b), 0.0))
    pos = lax.broadcasted_iota(jnp.int32, (ck, HEAD_DIM), 0).astype(F32)
    qdf[...] = jnp.exp((pos + 1.0) * lg_f)
    kdf[...] = jnp.exp((ck - 1.0 - pos) * lg_f)
    qdb[...] = jnp.exp((ck - pos) * lg_b)
    kdb[...] = jnp.exp(pos * lg_b)
    full = jnp.full((HEAD_DIM, HEAD_DIM), float(ck), F32)
    cdf[...] = jnp.exp(full * lg_f)
    cdb[...] = jnp.exp(full * lg_b)

    def rows_of(z):
        return pl.ds(pl.multiple_of(z * ck, ck), ck)

    def fwd(it, state):
        zs = [it * RET_UNROLL + u for u in range(RET_UNROLL)]
        qs = [q_ref[rows_of(z), :] for z in zs]
        ks = [k_ref[rows_of(z), :] for z in zs]
        vs = [v_ref[rows_of(z), :] for z in zs]
        inner = [_dot((_dot_nt(q, k) * dmat[...]).astype(BF16), v) for q, k, v in zip(qs, ks, vs)]
        kvs = [_dot_tn((k.astype(F32) * kdf[...]).astype(BF16), v) for k, v in zip(ks, vs)]
        for u, z in enumerate(zs):
            y_scr[rows_of(z), :] = inner[u] + _dot(qs[u], state.astype(BF16)) * qdf[...]
            state = cdf[...] * state + kvs[u]
        return state

    lax.fori_loop(0, nit, fwd, sf_ref[...])

    nit_b = q_ref.shape[0] // (ck * RET_UNROLL_BWD)

    def bwd(it, state):
        zs = [nit_b * RET_UNROLL_BWD - 1 - (it * RET_UNROLL_BWD + u) for u in range(RET_UNROLL_BWD)]
        qs = [q_ref[rows_of(z), :] for z in zs]
        kvs = [_dot_tn((k_ref[rows_of(z), :].astype(F32) * kdb[...]).astype(BF16), v_ref[rows_of(z), :])
               for z in zs]
        for u, z in enumerate(zs):
            y = y_scr[rows_of(z), :] + _dot(qs[u], state.astype(BF16)) * qdb[...]
            state = cdb[...] * state + kvs[u]
            mu = jnp.mean(y, axis=-1, keepdims=True)
            var = jnp.mean(jnp.square(y - mu), axis=-1, keepdims=True)
            yn = (y - mu) * lax.rsqrt(var + EPS) * gn_ref[...]
            o_ref[rows_of(z), :] = (_silu(g_ref[rows_of(z), :].astype(F32)) * yn).astype(o_ref.dtype)
        return state

    lax.fori_loop(0, nit_b, bwd, sb_ref[...])


def _retention(p_lat, log_g, s_f, s_b, gn_g):
    s = p_lat.shape[0]
    ck = RET_KERNEL_CHUNK
    assert s % (ck * RET_UNROLL) == 0 and s % (ck * RET_UNROLL_BWD) == 0
    col = lambda off: pl.BlockSpec((s, HEAD_DIM), lambda h: (0, off + h))
    state_spec = pl.BlockSpec((None, HEAD_DIM, HEAD_DIM), lambda h: (h, 0, 0))
    vec = pltpu.VMEM((ck, HEAD_DIM), F32)
    sq = pltpu.VMEM((HEAD_DIM, HEAD_DIM), F32)
    return pl.pallas_call(
        _ret_kernel,
        grid=(N_RET_HEADS,),
        in_specs=[pl.BlockSpec(memory_space=pltpu.SMEM),
                  col(COL_RQ), col(COL_RK), col(COL_RV), col(COL_G), state_spec, state_spec,
                  pl.BlockSpec((None, 1, HEAD_DIM), lambda h: (h, 0, 0))],
        out_specs=col(0),
        out_shape=jax.ShapeDtypeStruct((s, RET_W), BF16),
        scratch_shapes=[pltpu.VMEM((s, HEAD_DIM), F32), pltpu.VMEM((ck, ck), F32),
                        vec, vec, vec, vec, sq, sq],
        compiler_params=_cparams("arbitrary"),
        name="retention",
    )(log_g, p_lat, p_lat, p_lat, p_lat, s_f, s_b, gn_g.reshape(N_RET_HEADS, 1, HEAD_DIM))


def _outproj_kernel(x_ref, a_ref, r_ref, wa_ref, wr_ref, gate_ref, *rest, row, n_cast):
    cast_in, o_ref, cast_out = rest[:n_cast], rest[n_cast], rest[n_cast + 1:]
    y = _dot(a_ref[...], wa_ref[...]) + _dot(r_ref[...], wr_ref[...])
    o_ref[...] = x_ref[...] + gate_ref[row:row + 1, :] * y
    for src, dst in zip(cast_in, cast_out):
        dst[...] = src[...].astype(dst.dtype)


def _out_projection(x, att, ret, w_out, mods, layer, row, tm, cast=()):
    m, d = x.shape
    n = m // tm
    slab = lambda w: pl.BlockSpec((w.shape[0] // n, w.shape[1]), lambda i: (i, 0))
    outs = pl.pallas_call(
        functools.partial(_outproj_kernel, row=row, n_cast=len(cast)),
        grid=(n,),
        in_specs=[
            pl.BlockSpec((tm, d), lambda i: (i, 0)),
            pl.BlockSpec((tm, ATT_W), lambda i: (i, 0)),
            pl.BlockSpec((tm, RET_W), lambda i: (i, 0)),
            pl.BlockSpec((None, ATT_W, d), lambda i: (layer, 0, 0)),
            pl.BlockSpec((None, RET_W, d), lambda i: (layer, ATT_W // RET_W, 0)),
            pl.BlockSpec((None, 8, d), lambda i: (layer, 0, 2)),
        ] + [slab(w) for w in cast],
        out_specs=[pl.BlockSpec((tm, d), lambda i: (i, 0))] + [slab(w) for w in cast],
        out_shape=[jax.ShapeDtypeStruct((m, d), F32)] + [jax.ShapeDtypeStruct(w.shape, BF16) for w in cast],
        compiler_params=_cparams("arbitrary"),
        name="out_projection",
    )(x, att, ret, w_out, w_out, mods, *cast)
    return outs[0] if not cast else outs


def _swiglu_accumulate(h_scr, w1_ref, w3_ref, w2_ref, o_ref, groups):
    w1 = w1_ref[...].astype(BF16)
    w3 = w3_ref[...].astype(BF16)
    w2 = w2_ref[...].astype(BF16)
    us = []
    for lo, n in groups:
        h = h_scr[lo:lo + n, :]
        us.append((_silu(_dot(h, w1)) * _dot(h, w3)).astype(BF16))
    for (lo, n), u in zip(groups, us):
        o_ref[lo:lo + n, :] += _dot(u, w2)


def _ffn_kernel(x_ref, sh_ref, sc_ref, gate_ref, g2_ref, w1_ref, w3_ref, w2_ref, o_ref, h_scr,
                *, row, n_parts):
    j = pl.program_id(1)
    part = x_ref.shape[0] // n_parts

    @pl.when(j == 0)
    def _():
        h = _norm_modulate(x_ref[...], g2_ref[...], sh_ref[row:row + 1, :], sc_ref[row:row + 1, :])
        h_scr[...] = h.astype(BF16)
        o_ref[...] = jnp.zeros_like(o_ref)

    _swiglu_accumulate(h_scr, w1_ref, w3_ref, w2_ref, o_ref, tuple((p * part, part) for p in range(n_parts)))

    @pl.when(j == pl.num_programs(1) - 1)
    def _():
        o_ref[...] = x_ref[...] + gate_ref[row:row + 1, :] * o_ref[...]


def _dense_ffn(x, mods, layer, row, g2, w1, w3, w2, tm, tf):
    m, d = x.shape
    f = w1.shape[1]
    mod = lambda k: pl.BlockSpec((None, 8, d), lambda i, j: (layer, 0, k))
    return pl.pallas_call(
        functools.partial(_ffn_kernel, row=row, n_parts=2 if tm >= 1024 else 1),
        grid=(m // tm, f // tf),
        in_specs=[
            pl.BlockSpec((tm, d), lambda i, j: (i, 0)),
            mod(3), mod(4), mod(5),
            pl.BlockSpec((1, d), lambda i, j: (0, 0)),
            pl.BlockSpec((d, tf), lambda i, j: (0, j)),
            pl.BlockSpec((d, tf), lambda i, j: (0, j)),
            pl.BlockSpec((tf, d), lambda i, j: (j, 0)),
        ],
        out_specs=pl.BlockSpec((tm, d), lambda i, j: (i, 0)),
        out_shape=jax.ShapeDtypeStruct((m, d), F32),
        scratch_shapes=[pltpu.VMEM((tm, d), BF16)],
        compiler_params=_cparams("arbitrary", "arbitrary"),
        name="dense_ffn",
    )(x, mods, mods, mods, g2, w1, w3, w2)


def _route_top2(logits, carry, live):
    tm = logits.shape[0]
    lane = lax.broadcasted_iota(jnp.int32, (tm, LANES), 1).astype(F32)
    v1 = jnp.max(logits, axis=-1, keepdims=True)
    e1 = jnp.min(jnp.where(logits == v1, lane, float(LANES)), axis=-1, keepdims=True)
    rest = jnp.where(lane == e1, NEG_INF, logits)
    v2 = jnp.max(rest, axis=-1, keepdims=True)
    e2 = jnp.min(jnp.where(rest == v2, lane, float(LANES)), axis=-1, keepdims=True)
    t = jnp.exp(v2 - v1)
    w1 = 1.0 / (1.0 + t)
    w2 = t / (1.0 + t)
    oh1 = jnp.where(lane == e1, 1.0, 0.0)
    oh2 = jnp.where(lane == e2, 1.0, 0.0)
    oh = oh1 + oh2
    r_i = lax.broadcasted_iota(jnp.int32, (tm, tm), 0)
    c_i = lax.broadcasted_iota(jnp.int32, (tm, tm), 1)
    tri = jnp.where(c_i < r_i, 1.0, 0.0).astype(BF16)
    before = _dot(tri, oh.astype(BF16)) + carry[0:1, :]
    rank1 = jnp.sum(before * oh1, axis=-1, keepdims=True)
    rank2 = jnp.sum(before * oh2, axis=-1, keepdims=True)
    carry[...] = carry[...] + live * jnp.sum(oh, axis=0, keepdims=True)
    info = jnp.where(lane == 0.0, e1, 0.0)
    info = jnp.where(lane == 1.0, e2, info)
    info = jnp.where(lane == 2.0, rank1, info)
    info = jnp.where(lane == 3.0, rank2, info)
    info = jnp.where(lane == 4.0, w1, info)
    info = jnp.where(lane == 5.0, w2, info)
    return info


def _outproj_router_kernel(x_ref, a_ref, r_ref, wa_ref, wr_ref, gate_ref, sh_ref, sc_ref, g2_ref, rw_ref, rb_ref,
                           o_ref, h_ref, info_ref, cnt_ref, xprev, carry, *, row):
    t = pl.program_id(0)
    tm, d = x_ref.shape

    @pl.when(t == 0)
    def _():
        xprev[...] = jnp.zeros_like(xprev)
        carry[...] = jnp.zeros_like(carry)

    y = _dot(a_ref[...], wa_ref[...]) + _dot(r_ref[...], wr_ref[...])
    x_new = x_ref[...] + gate_ref[row:row + 1, :] * y
    o_ref[...] = x_new

    h_ref[...] = _norm_modulate(xprev[...], g2_ref[...], sh_ref[row:row + 1, :], sc_ref[row:row + 1, :])
    lane = lax.broadcasted_iota(jnp.int32, (tm, LANES), 1).astype(F32)
    parts = [None] * N_EXPERTS
    for b in range(d // LANES):
        cols = slice(b * LANES, (b + 1) * LANES)
        hb = h_ref[:, cols]
        for e in range(N_EXPERTS):
            term = hb * rw_ref[e:e + 1, cols]
            parts[e] = term if parts[e] is None else parts[e] + term
    logits = jnp.full((tm, LANES), NEG_INF, F32)
    for e in range(N_EXPERTS):
        logit_e = jnp.sum(parts[e], axis=-1, keepdims=True) + rb_ref[0:1, e:e + 1]
        logits = jnp.where(lane == float(e), logit_e, logits)
    info_ref[...] = _route_top2(logits, carry, jnp.where(t > 0, 1.0, 0.0))
    cnt_ref[...] = carry[...]
    xprev[...] = x_new


def _out_projection_router(x, att, ret, w_out, mods, layer, row, g2, router_w, router_b, tm):
    m, d = x.shape
    n = m // tm
    rw_t = jnp.pad(router_w.astype(F32).T, ((0, 8 - N_EXPERTS), (0, 0)))
    rb_pad = jnp.pad(router_b.astype(F32), (0, LANES - N_EXPERTS)).reshape(1, LANES)
    cur = lambda t: (jnp.minimum(t, n - 1), 0)
    prv = lambda t: (jnp.maximum(t - 1, 0), 0)
    mod = lambda k: pl.BlockSpec((None, 8, d), lambda t: (layer, 0, k))
    once = pl.Buffered(1)
    return pl.pallas_call(
        functools.partial(_outproj_router_kernel, row=row),
        grid=(n + 1,),
        in_specs=[
            pl.BlockSpec((tm, d), cur),
            pl.BlockSpec((tm, ATT_W), cur),
            pl.BlockSpec((tm, RET_W), cur),
            pl.BlockSpec((None, ATT_W, d), lambda t: (layer, 0, 0), pipeline_mode=once),
            pl.BlockSpec((None, RET_W, d), lambda t: (layer, ATT_W // RET_W, 0), pipeline_mode=once),
            mod(2), mod(3), mod(4),
            pl.BlockSpec((1, d), lambda t: (0, 0)),
            pl.BlockSpec((8, d), lambda t: (0, 0)),
            pl.BlockSpec((1, LANES), lambda t: (0, 0)),
        ],
        out_specs=[
            pl.BlockSpec((tm, d), cur),
            pl.BlockSpec((tm, d), prv),
            pl.BlockSpec((tm, LANES), prv),
            pl.BlockSpec((8, LANES), lambda t: (0, 0)),
        ],
        out_shape=[
            jax.ShapeDtypeStruct((m, d), F32),
            jax.ShapeDtypeStruct((m, d), F32),
            jax.ShapeDtypeStruct((m, LANES), F32),
            jax.ShapeDtypeStruct((8, LANES), F32),
        ],
        scratch_shapes=[pltpu.VMEM((tm, d), F32), pltpu.VMEM((8, LANES), F32)],
        compiler_params=_cparams("arbitrary"),
        name="out_projection_router",
    )(x, att, ret, w_out, w_out, mods, mods, mods, g2, rw_t, rb_pad)


GATHER_UNROLL = 8


def _gather_rows(idx_ref, src_hbm, dst, sem, n):
    def body(kb, carry):
        for u in range(GATHER_UNROLL):
            k = kb * GATHER_UNROLL + u
            pltpu.make_async_copy(src_hbm.at[pl.ds(idx_ref[0, 0, k], 1)], dst.at[pl.ds(k, 1)], sem).start()
        return carry
    lax.fori_loop(0, n // GATHER_UNROLL, body, 0)


def _wait_rows(src_hbm, dst, sem, n):
    pltpu.make_async_copy(src_hbm.at[pl.ds(0, n)], dst, sem).wait()


def _issue_rows(idx_ref, src_hbm, dst, sem, start, count):
    for u in range(count):
        k = start + u
        pltpu.make_async_copy(src_hbm.at[pl.ds(idx_ref[0, 0, k], 1)], dst.at[pl.ds(k, 1)], sem).start()


def _moe_ffn_kernel(te_ref, nu_ref, nv_ref, cur_ref, nxt_ref, h_hbm, w1_ref, w3_ref, w2_ref, o_ref,
                    buf, sem, h_scr, *, tm, n_f):
    r = pl.program_id(0)
    j = pl.program_id(1)
    n_used = nu_ref[0]
    half = tm // 2
    quarter = tm // 4
    per_step = tm // n_f
    head = tm - per_step * n_f

    @pl.when(r < n_used)
    def _():
        @pl.when(j == 0)
        def _():
            @pl.when(r == 0)
            def _():
                _gather_rows(cur_ref, h_hbm, buf, sem, tm)

            _wait_rows(h_hbm, buf, sem, tm)
            h_scr[...] = buf[...].astype(BF16)

            @pl.when(r + 1 < n_used)
            def _():
                _issue_rows(nxt_ref, h_hbm, buf, sem, 0, head)

            o_ref[...] = jnp.zeros_like(o_ref)

        more = r + 1 < n_used
        quarters = (nv_ref[r] + quarter - 1) // quarter
        row_groups = {1: ((0, quarter),), 2: ((0, half),), 3: ((0, half), (half, quarter)),
                      4: ((0, half), (half, half))}

        def case(fetch, groups):
            if fetch:
                _issue_rows(nxt_ref, h_hbm, buf, sem, head + j * per_step, per_step)
            _swiglu_accumulate(h_scr, w1_ref, w3_ref, w2_ref, o_ref, groups)

        for nq, groups in row_groups.items():
            pl.when(more & (quarters == nq))(functools.partial(case, True, groups))
            pl.when(jnp.logical_not(more) & (quarters == nq))(functools.partial(case, False, groups))

    @pl.when((r >= n_used) & (j == 0))
    def _():
        o_ref[...] = jnp.zeros_like(o_ref)


def _moe_ffn(h2, tile_expert, n_used, n_valid, src_rows, w1, w3, w2, tm, tf):
    d = h2.shape[1]
    f = w1.shape[2]
    n_tiles = src_rows.shape[0]
    n_f = f // tf
    clamp = lambda r, nu: jnp.maximum(jnp.minimum(r, nu[0] - 1), 0)
    fcol = lambda r, j, nu: jnp.where(r < nu[0], j, n_f - 1)
    grid_spec = pltpu.PrefetchScalarGridSpec(
        num_scalar_prefetch=3,
        grid=(n_tiles, n_f),
        in_specs=[
            pl.BlockSpec((1, 1, tm), lambda r, j, te, nu, nv: (clamp(r, nu), 0, 0), memory_space=pltpu.SMEM),
            pl.BlockSpec((1, 1, tm), lambda r, j, te, nu, nv: (clamp(r + 1, nu), 0, 0), memory_space=pltpu.SMEM),
            pl.BlockSpec(memory_space=pl.ANY),
            pl.BlockSpec((None, d, tf), lambda r, j, te, nu, nv: (te[clamp(r, nu)], 0, fcol(r, j, nu))),
            pl.BlockSpec((None, d, tf), lambda r, j, te, nu, nv: (te[clamp(r, nu)], 0, fcol(r, j, nu))),
            pl.BlockSpec((None, tf, d), lambda r, j, te, nu, nv: (te[clamp(r, nu)], fcol(r, j, nu), 0)),
        ],
        out_specs=pl.BlockSpec((tm, d), lambda r, j, te, nu, nv: (r, 0)),
        scratch_shapes=[
            pltpu.VMEM((tm, d), F32),
            pltpu.SemaphoreType.DMA(()),
            pltpu.VMEM((tm, d), BF16),
        ],
    )
    return pl.pallas_call(
        functools.partial(_moe_ffn_kernel, tm=tm, n_f=n_f),
        grid_spec=grid_spec,
        out_shape=jax.ShapeDtypeStruct((n_tiles * tm, d), F32),
        compiler_params=_cparams("arbitrary", "arbitrary"),
        name="moe_ffn",
    )(tile_expert, n_used, n_valid, src_rows, src_rows, h2, w1, w3, w2)


def _combine_kernel(cur_ref, nxt_ref, x_ref, info_ref, gate_ref, y_hbm, o_ref, buf_a, buf_b, sems, *, tm, row):
    i = pl.program_id(0)

    @pl.when(i == 0)
    def _():
        _gather_rows(cur_ref, y_hbm, buf_a, sems.at[0], 2 * tm)

    def step(cur, cur_sem, nxt, nxt_sem):
        _wait_rows(y_hbm, cur, cur_sem, 2 * tm)

        @pl.when(i + 1 < pl.num_programs(0))
        def _():
            _issue_rows(nxt_ref, y_hbm, nxt, nxt_sem, 0, 2 * tm)

        info = info_ref[...]
        y = info[:, 4:5] * cur[0:tm, :] + info[:, 5:6] * cur[tm:2 * tm, :]
        o_ref[...] = x_ref[...] + gate_ref[row:row + 1, :] * y

    pl.when(i % 2 == 0)(functools.partial(step, buf_a, sems.at[0], buf_b, sems.at[1]))
    pl.when(i % 2 == 1)(functools.partial(step, buf_b, sems.at[1], buf_a, sems.at[0]))


def _moe_combine(x, info, mods, layer, row, y_rows, pos_tiles, tm):
    m, d = x.shape
    nt = m // tm
    return pl.pallas_call(
        functools.partial(_combine_kernel, tm=tm, row=row),
        grid=(nt,),
        in_specs=[
            pl.BlockSpec((1, 1, 2 * tm), lambda i: (i, 0, 0), memory_space=pltpu.SMEM),
            pl.BlockSpec((1, 1, 2 * tm), lambda i: (jnp.minimum(i + 1, nt - 1), 0, 0), memory_space=pltpu.SMEM),
            pl.BlockSpec((tm, d), lambda i: (i, 0)),
            pl.BlockSpec((tm, LANES), lambda i: (i, 0)),
            pl.BlockSpec((None, 8, d), lambda i: (layer, 0, 5)),
            pl.BlockSpec(memory_space=pl.ANY),
        ],
        out_specs=pl.BlockSpec((tm, d), lambda i: (i, 0)),
        out_shape=jax.ShapeDtypeStruct((m, d), F32),
        scratch_shapes=[pltpu.VMEM((2 * tm, d), F32), pltpu.VMEM((2 * tm, d), F32),
                        pltpu.SemaphoreType.DMA((2,))],
        compiler_params=_cparams("arbitrary"),
        name="moe_combine",
    )(pos_tiles, pos_tiles, x, info, mods, y_rows)


def _mix_and_moe(x_in, att, ret, w_out, mods, layer, row, g2, router_w, router_b, w1, w3, w2,
                 tm_route, tm_exp, tf):
    m, d = x_in.shape
    x, h2, info, cnt = _out_projection_router(x_in, att, ret, w_out, mods, layer, row, g2,
                                              router_w, router_b, tm_route)
    counts = cnt[0, :N_EXPERTS].astype(jnp.int32)
    tiles_per = (counts + tm_exp - 1) // tm_exp
    tile_end = jnp.cumsum(tiles_per)
    group_start = (tile_end - tiles_per) * tm_exp
    n_tiles = (2 * m) // tm_exp + N_EXPERTS
    n_used = tile_end[-1:].astype(jnp.int32)
    tile_ids = jnp.arange(n_tiles, dtype=jnp.int32)
    tile_expert = jnp.minimum(jnp.sum(tile_end[None, :] <= tile_ids[:, None], axis=1),
                              N_EXPERTS - 1).astype(jnp.int32)
    rows_before = (tile_ids - (tile_end - tiles_per)[tile_expert]) * tm_exp
    n_valid = jnp.clip(counts[tile_expert] - rows_before, 0, tm_exp).astype(jnp.int32)
    e12 = info[:, 0:2].astype(jnp.int32)
    rank12 = info[:, 2:4].astype(jnp.int32)
    pos = group_start[e12] + rank12
    tok = jnp.broadcast_to(jnp.arange(m, dtype=jnp.int32)[:, None], (m, 2))
    src_rows = jnp.zeros((n_tiles * tm_exp,), jnp.int32).at[pos.reshape(-1)].set(
        tok.reshape(-1), unique_indices=True, mode="promise_in_bounds")
    y_rows = _moe_ffn(h2, tile_expert, n_used, n_valid, src_rows.reshape(n_tiles, 1, tm_exp),
                      w1, w3, w2, tm_exp, tf)
    nt = m // tm_route
    pos_tiles = pos.reshape(nt, tm_route, 2).transpose(0, 2, 1).reshape(nt, 1, 2 * tm_route)
    return _moe_combine(x, info, mods, layer, row, y_rows, pos_tiles, tm_route)


def _rope_tables(n_tokens):
    n_rows = n_tokens // GRID_W
    inv = ROPE_THETA ** (-jnp.arange(ROPE_PAIRS, dtype=F32) / ROPE_PAIRS)
    ang_r = jnp.arange(n_rows).astype(F32)[:, None] * inv
    ang_c = jnp.arange(GRID_W).astype(F32)[:, None] * inv
    cr, sr, cc, sc = jnp.cos(ang_r), jnp.sin(ang_r), jnp.cos(ang_c), jnp.sin(ang_c)
    zr, zc = jnp.zeros_like(cr), jnp.zeros_like(cc)
    rowtab = jnp.stack([jnp.concatenate(p, axis=-1) for p in
                        ((cr, cr, zr, zr), (-sr, zr, zr, zr), (zr, sr, zr, zr))])
    coltab = jnp.stack([jnp.concatenate(p, axis=-1) for p in
                        ((zc, zc, cc, cc), (zc, zc, -sc, zc), (zc, zc, zc, sc))])
    return rowtab, coltab


def _identity_rope_tables(n_tokens):
    n_rows = n_tokens // GRID_W
    rowtab = jnp.stack([jnp.ones((n_rows, HEAD_DIM), F32), jnp.zeros((n_rows, HEAD_DIM), F32),
                        jnp.zeros((n_rows, HEAD_DIM), F32)])
    return rowtab, jnp.zeros((3, GRID_W, HEAD_DIM), F32)


def kernel(x, c, ctx, c_ctx, ada_w, ada_b, norm1_g, norm2_g, w_in, q_norm_g, k_norm_g, attn_sink,
           ret_decay, ret_gn_g, w_out, ffn_w1, ffn_w3, ffn_w2, router_w, router_b, moe_w1, moe_w3, moe_w2):
    b, s, d = x.shape
    n_ctx = ctx.shape[1]
    depth = ada_w.shape[0]
    assert b == 1, "one latent sequence per call"
    x_lat = x[0]
    x_ctx = ctx[0]

    tm_lat = min(1024, s)
    tm_ffn = min(512, s)
    tq = min(512, s)
    tf_ffn = 512
    tm_moe = 1024
    tf_moe = 256

    cond = jnp.stack([c[0], c_ctx]).astype(F32)
    mods = _ada_mods(cond, ada_w, ada_b)
    rope_l = _rope_tables(s)
    rope_c = _identity_rope_tables(n_ctx)
    log_g_all = jax.nn.log_sigmoid(ret_decay.astype(F32))

    w_in_i = w_in
    w_out_i = w_out.astype(BF16)

    for i in range(depth):
        last = i == depth - 1
        g1 = norm1_g[i].reshape(1, d)
        g2 = norm2_g[i].reshape(1, d)
        log_g = log_g_all[i]
        gain = jnp.concatenate([
            jnp.tile(q_norm_g[i].astype(F32) * ATT_SCALE, N_ATT_HEADS),
            jnp.tile(k_norm_g[i].astype(F32), N_KV_HEADS),
            jnp.ones((KV_W + RET_W,), F32),
            jnp.full((RET_W,), ATT_SCALE, F32),
            jnp.ones((2 * RET_W,), F32)]).reshape(1, IN_COLS)

        p_lat = _in_projection(x_lat, mods, i, 0, g1, w_in_i, gain, *rope_l, tm_lat)
        p_ctx = _in_projection(x_ctx, mods, i, 1, g1, w_in_i, gain, *rope_c, n_ctx)

        att_c, ret_c, s_f, s_b = _ctx_mixer(p_ctx, attn_sink[i], log_g, ret_gn_g[i])
        att_l = _window_attention(p_lat, p_ctx, attn_sink[i], tq)
        ret_l = _retention(p_lat, log_g, s_f, s_b, ret_gn_g[i])
        j = i // 2
        if i % 2 == 0:
            ffn_w = (ffn_w1[j], ffn_w3[j], ffn_w2[j])
            n_slabs = s // tm_ffn
            if all(w.shape[0] % (16 * n_slabs) == 0 for w in ffn_w):
                x_lat, w1, w3, w2 = _out_projection(x_lat, att_l, ret_l, w_out_i, mods, i, 0, tm_ffn, cast=ffn_w)
            else:
                x_lat = _out_projection(x_lat, att_l, ret_l, w_out_i, mods, i, 0, tm_ffn)
                w1, w3, w2 = (w.astype(BF16) for w in ffn_w)
            x_lat = _dense_ffn(x_lat, mods, i, 0, g2, w1, w3, w2, tm_ffn, tf_ffn)
        else:
            x_lat = _mix_and_moe(x_lat, att_l, ret_l, w_out_i, mods, i, 0, g2, router_w[j], router_b[j],
                                 moe_w1[j], moe_w3[j], moe_w2[j], tm_ffn, min(tm_moe, s), tf_moe)

        if not last:
            if i % 2 == 0:
                x_ctx = _out_projection(x_ctx, att_c, ret_c, w_out_i, mods, i, 1, n_ctx)
                x_ctx = _dense_ffn(x_ctx, mods, i, 1, g2, w1, w3, w2, n_ctx, tf_ffn)
            else:
                x_ctx = _mix_and_moe(x_ctx, att_c, ret_c, w_out_i, mods, i, 1, g2, router_w[j], router_b[j],
                                     moe_w1[j], moe_w3[j], moe_w2[j], n_ctx, n_ctx, tf_moe)
    return x_lat[None]
```

```python
import functools

import jax
import jax.numpy as jnp
from jax import lax
from jax.experimental import pallas as pl
from jax.experimental.pallas import tpu as pltpu

F32 = jnp.float32
BF16 = jnp.bfloat16

HEAD_DIM = 128
N_ATT_HEADS = 8
N_KV_HEADS = 2
N_RET_HEADS = 8
GQA_GROUP = N_ATT_HEADS // N_KV_HEADS
ATT_W = N_ATT_HEADS * HEAD_DIM
KV_W = N_KV_HEADS * HEAD_DIM
RET_W = N_RET_HEADS * HEAD_DIM
IN_COLS = ATT_W + 2 * KV_W + 4 * RET_W
WINDOW = 128
RET_CHUNK = 128
GRID_W = 64
ROPE_THETA = 10000.0
ROPE_PAIRS = HEAD_DIM // 4
N_EXPERTS = 8
EPS = 1e-6
ATT_SCALE = HEAD_DIM ** -0.5

COL_Q = 0
COL_K = ATT_W // HEAD_DIM
COL_V = COL_K + N_KV_HEADS
COL_RQ = COL_V + N_KV_HEADS
COL_RK = COL_RQ + N_RET_HEADS
COL_RV = COL_RK + N_RET_HEADS
COL_G = COL_RV + N_RET_HEADS

VMEM_LIMIT_BYTES = 56 * 1024 * 1024
LANES = 128

NEG_INF = float("-inf")


def _cparams(*sem):
    return pltpu.CompilerParams(dimension_semantics=sem, vmem_limit_bytes=VMEM_LIMIT_BYTES)


def _silu(x):
    return x * (1.0 / (1.0 + jnp.exp(-x)))


def _dot(a, b):
    return jnp.dot(a, b, preferred_element_type=F32)


def _dot_nt(a, b):
    return lax.dot_general(a, b, (((1,), (1,)), ((), ())), preferred_element_type=F32)


def _dot_tn(a, b):
    return lax.dot_general(a, b, (((0,), (0,)), ((), ())), preferred_element_type=F32)


def _norm_modulate(x, g, shift, scale):
    ms = jnp.mean(x * x, axis=-1, keepdims=True)
    return (x * lax.rsqrt(ms + EPS) * g) * (1.0 + scale) + shift


N_STREAMS = 2


def _ada_kernel(c_ref, w_ref, b_ref, o_ref):
    tn = w_ref.shape[1]
    w = w_ref[...]
    rows = [jnp.sum(w * jnp.tile(_silu(c_ref[r]), (1, tn // LANES)), axis=0, keepdims=True)
            for r in range(N_STREAMS)]
    o_ref[...] = jnp.concatenate(rows + [jnp.zeros((8 - N_STREAMS, tn), F32)], axis=0) + b_ref[...]


def _ada_mods(cond, ada_w, ada_b):
    depth, d, n = ada_w.shape
    tn = d // 2
    cond_lanes = jnp.broadcast_to(cond[:, :, None], (N_STREAMS, d, LANES))
    return pl.pallas_call(
        _ada_kernel,
        grid=(depth, n // tn),
        in_specs=[
            pl.BlockSpec((N_STREAMS, d, LANES), lambda l, j: (0, 0, 0)),
            pl.BlockSpec((None, d, tn), lambda l, j: (l, 0, j)),
            pl.BlockSpec((None, 1, tn), lambda l, j: (l, 0, j)),
        ],
        out_specs=pl.BlockSpec((None, 8, tn), lambda l, j: (l, 0, j)),
        out_shape=jax.ShapeDtypeStruct((depth, 8, n), F32),
        compiler_params=_cparams("arbitrary", "arbitrary"),
        name="ada_mods",
    )(cond_lanes, ada_w, ada_b.reshape(depth, 1, n))


def _rope(y, cos, sa, sb):
    return y * cos + pltpu.roll(y, 96, 1) * sa + pltpu.roll(y, 32, 1) * sb


def _inproj_kernel(x_ref, sh_ref, sc_ref, g1_ref, w_ref, gain_ref, rowtab_ref, coltab_ref, o_ref,
                   h_scr, p_scr, cos_ref, sa_ref, sb_ref, *, row, tn, n_j, n_steps):
    t = pl.program_id(0)
    j = t % n_j
    nchunk = tn // HEAD_DIM

    @pl.when(t == 0)
    def _():
        p_scr[...] = jnp.zeros_like(p_scr)

    @pl.when((t == 0) | ((t - 1) % n_j == 0))
    def _():
        for k, tab in enumerate((cos_ref, sa_ref, sb_ref)):
            for gr in range(rowtab_ref.shape[1]):
                tab[gr * GRID_W:(gr + 1) * GRID_W, :] = rowtab_ref[k, gr:gr + 1, :] + coltab_ref[k]

    @pl.when((j == 0) & (t < n_steps))
    def _():
        h = _norm_modulate(x_ref[...], g1_ref[...], sh_ref[row:row + 1, :], sc_ref[row:row + 1, :])
        h_scr[...] = h.astype(BF16)

    p_new = _dot(h_scr[...], w_ref[...])

    jp = (t + n_j - 1) % n_j
    tabs = None
    for c in range(nchunk):
        gc = jp * nchunk + c
        norm_on = gc < COL_V
        rope_on = norm_on | ((gc >= COL_RQ) & (gc < COL_RV))
        if c % N_KV_HEADS == 0:
            tabs = (jnp.where(rope_on, cos_ref[...], 1.0), jnp.where(rope_on, sa_ref[...], 0.0),
                    jnp.where(rope_on, sb_ref[...], 0.0))
        cols = slice(c * HEAD_DIM, (c + 1) * HEAD_DIM)
        pc = p_scr[:, cols]
        r = lax.rsqrt(jnp.mean(pc * pc, axis=-1, keepdims=True) + EPS)
        y = pc * jnp.where(norm_on, r, 1.0) * gain_ref[:, cols]
        o_ref[:, cols] = _rope(y, *tabs).astype(o_ref.dtype)
    p_scr[...] = p_new


def _in_projection(x, mods, layer, row, g1, w, gain, rowtab, coltab, tm):
    m, d = x.shape
    n = w.shape[2]
    tn = 2 * KV_W
    assert n == IN_COLS and ATT_W % tn == 0 and RET_W % tn == 0 and m % tm == 0 and tm % GRID_W == 0
    n_i, n_j = m // tm, n // tn
    n_steps = n_i * n_j
    prev = lambda t: jnp.maximum(t - 1, 0)
    tab = pltpu.VMEM((tm, HEAD_DIM), F32)
    return pl.pallas_call(
        functools.partial(_inproj_kernel, row=row, tn=tn, n_j=n_j, n_steps=n_steps),
        grid=(n_steps + 1,),
        in_specs=[
            pl.BlockSpec((tm, d), lambda t: (jnp.minimum(t // n_j, n_i - 1), 0)),
            pl.BlockSpec((None, 8, d), lambda t: (layer, 0, 0)),
            pl.BlockSpec((None, 8, d), lambda t: (layer, 0, 1)),
            pl.BlockSpec((1, d), lambda t: (0, 0)),
            pl.BlockSpec((None, d, tn), lambda t: (layer, 0, t % n_j)),
            pl.BlockSpec((1, tn), lambda t: (0, prev(t) % n_j)),
            pl.BlockSpec((3, tm // GRID_W, HEAD_DIM), lambda t: (0, prev(t) // n_j, 0)),
            pl.BlockSpec((3, GRID_W, HEAD_DIM), lambda t: (0, 0, 0)),
        ],
        out_specs=pl.BlockSpec((tm, tn), lambda t: (prev(t) // n_j, prev(t) % n_j)),
        out_shape=jax.ShapeDtypeStruct((m, n), BF16),
        scratch_shapes=[pltpu.VMEM((tm, d), BF16), pltpu.VMEM((tm, tn), F32), tab, tab, tab],
        compiler_params=_cparams("arbitrary"),
        name="in_projection",
    )(x, mods, mods, g1, w, gain, rowtab, coltab)


ATTN_GROUP_BLOCKS = 4


def _attn_kernel(sink_ref, q_ref, km_ref, vm_ref, kp_ref, vp_ref, kn_ref, vn_ref,
                 kc_ref, vc_ref, bias_ref, o_ref, kbuf, vbuf, vcbuf, *, tq):
    i = pl.program_id(0)
    last = pl.num_programs(0) - 1
    nsub = tq // WINDOW
    n_ctx = kc_ref.shape[0]
    kbuf[0:WINDOW] = kp_ref[...]
    kbuf[WINDOW:WINDOW + tq] = km_ref[...]
    kbuf[WINDOW + tq:] = kn_ref[...]
    for h in range(N_KV_HEADS):
        hs = slice(h * HEAD_DIM, (h + 1) * HEAD_DIM)
        va = slice(2 * h * HEAD_DIM, (2 * h + 1) * HEAD_DIM)
        vbuf[0:WINDOW, va] = vp_ref[:, hs]
        vbuf[WINDOW:WINDOW + tq, va] = vm_ref[:, hs]
        vbuf[WINDOW + tq:, va] = vn_ref[:, hs]

    @pl.when(i == 0)
    def _():
        for h in range(N_KV_HEADS):
            ones = slice((2 * h + 1) * HEAD_DIM, (2 * h + 2) * HEAD_DIM)
            vbuf[:, ones] = jnp.ones((vbuf.shape[0], HEAD_DIM), vbuf.dtype)
            vcbuf[:, ones] = jnp.ones((vcbuf.shape[0], HEAD_DIM), vcbuf.dtype)
            vcbuf[:, 2 * h * HEAD_DIM:(2 * h + 1) * HEAD_DIM] = vc_ref[:, h * HEAD_DIM:(h + 1) * HEAD_DIM]

    col = lax.broadcasted_iota(jnp.int32, (WINDOW, 3 * WINDOW), 1)
    def block_bias(s):
        bias = bias_ref[...]
        if s == 0:
            bias = bias + jnp.where(col < WINDOW, jnp.where(i == 0, NEG_INF, 0.0), 0.0)
        if s == nsub - 1:
            bias = bias + jnp.where(col >= 2 * WINDOW, jnp.where(i == last, NEG_INF, 0.0), 0.0)
        return jnp.concatenate([bias] * GQA_GROUP, axis=0)

    def head_cols(h, g):
        return slice((h * GQA_GROUP + g) * HEAD_DIM, (h * GQA_GROUP + g + 1) * HEAD_DIM)

    for s0 in range(0, nsub, ATTN_GROUP_BLOCKS):
        units = [(s, h) for s in range(s0, min(s0 + ATTN_GROUP_BLOCKS, nsub)) for h in range(N_KV_HEADS)]
        scores = []
        for s, h in units:
            hs = slice(h * HEAD_DIM, (h + 1) * HEAD_DIM)
            rows = slice(s * WINDOW, (s + 1) * WINDOW)
            q = jnp.concatenate([q_ref[rows, head_cols(h, g)] for g in range(GQA_GROUP)], axis=0)
            keys = jnp.concatenate([kc_ref[:, hs], kbuf[s * WINDOW:(s + 3) * WINDOW, hs]], axis=0)
            s_all = _dot_nt(q, keys)
            scores.append((s_all[:, :n_ctx], s_all[:, n_ctx:] + block_bias(s)))
        probs = []
        for (s, h), (s_ctx, s_loc) in zip(units, scores):
            sink = jnp.concatenate(
                [jnp.full((WINDOW, 1), sink_ref[h * GQA_GROUP + g], F32) for g in range(GQA_GROUP)],
                axis=0)
            blocks = ([s_loc[:, b * HEAD_DIM:(b + 1) * HEAD_DIM] for b in range(3)]
                      + [s_ctx[:, b * HEAD_DIM:(b + 1) * HEAD_DIM] for b in range(n_ctx // HEAD_DIM)])
            m = jnp.maximum(jnp.max(functools.reduce(jnp.maximum, blocks), axis=-1, keepdims=True), sink)
            p = jnp.concatenate([jnp.exp(s_ctx - m), jnp.exp(s_loc - m)], axis=1).astype(BF16)
            probs.append((p, jnp.exp(sink - m)))
        for (s, h), (p, p_sink) in zip(units, probs):
            vs = slice(2 * h * HEAD_DIM, (2 * h + 2) * HEAD_DIM)
            vals = jnp.concatenate([vcbuf[:, vs], vbuf[s * WINDOW:(s + 3) * WINDOW, vs]], axis=0)
            o_den = _dot(p, vals)
            o = o_den[:, :HEAD_DIM] / (o_den[:, HEAD_DIM:] + p_sink)
            for g in range(GQA_GROUP):
                o_ref[s * WINDOW:(s + 1) * WINDOW, head_cols(h, g)] = (
                    o[g * WINDOW:(g + 1) * WINDOW].astype(o_ref.dtype))


def _band_bias():
    r = jnp.arange(WINDOW)[:, None]
    c = jnp.arange(3 * WINDOW)[None, :]
    ok = (c - r >= 0) & (c - r <= 2 * WINDOW)
    return jnp.where(ok, 0.0, NEG_INF).astype(F32)


def _window_attention(p_lat, p_ctx, sink, tq):
    s = p_lat.shape[0]
    c = p_ctx.shape[0]
    r = tq // WINDOW
    nb = s // WINDOW
    kvb = KV_W // HEAD_DIM
    ck, cv = COL_K // kvb, COL_V // kvb
    return pl.pallas_call(
        functools.partial(_attn_kernel, tq=tq),
        grid=(s // tq,),
        in_specs=[
            pl.BlockSpec(memory_space=pltpu.SMEM),
            pl.BlockSpec((tq, ATT_W), lambda i: (i, 0)),
            pl.BlockSpec((tq, KV_W), lambda i: (i, ck)),
            pl.BlockSpec((tq, KV_W), lambda i: (i, cv)),
            pl.BlockSpec((WINDOW, KV_W), lambda i: (jnp.maximum(i * r - 1, 0), ck)),
            pl.BlockSpec((WINDOW, KV_W), lambda i: (jnp.maximum(i * r - 1, 0), cv)),
            pl.BlockSpec((WINDOW, KV_W), lambda i: (jnp.minimum((i + 1) * r, nb - 1), ck)),
            pl.BlockSpec((WINDOW, KV_W), lambda i: (jnp.minimum((i + 1) * r, nb - 1), cv)),
            pl.BlockSpec((c, KV_W), lambda i: (0, ck)),
            pl.BlockSpec((c, KV_W), lambda i: (0, cv)),
            pl.BlockSpec((WINDOW, 3 * WINDOW), lambda i: (0, 0)),
        ],
        out_specs=pl.BlockSpec((tq, ATT_W), lambda i: (i, 0)),
        out_shape=jax.ShapeDtypeStruct((s, ATT_W), BF16),
        scratch_shapes=[pltpu.VMEM((tq + 2 * WINDOW, KV_W), BF16),
                        pltpu.VMEM((tq + 2 * WINDOW, 2 * KV_W), BF16),
                        pltpu.VMEM((c, 2 * KV_W), BF16)],
        compiler_params=_cparams("arbitrary"),
        name="window_attention",
    )(sink, p_lat, p_lat, p_lat, p_lat, p_lat, p_lat, p_lat, p_ctx, p_ctx, _band_bias())


def _ctx_mixer_kernel(sink_ref, lg_ref, q_ref, k_ref, v_ref, rq_ref, rk_ref, rv_ref, g_ref, gn_ref,
                      att_ref, ret_ref, sf_ref, sb_ref):
    h = pl.program_id(0)
    c = q_ref.shape[0]
    lg_f = lg_ref[0, h]
    lg_b = lg_ref[1, h]
    s = _dot_nt(q_ref[...], k_ref[...])
    sink = jnp.full((c, 1), sink_ref[h], F32)
    m = jnp.maximum(jnp.max(s, axis=-1, keepdims=True), sink)
    p = jnp.exp(s - m)
    den = jnp.sum(p, axis=-1, keepdims=True) + jnp.exp(sink - m)
    att_ref[...] = (_dot(p.astype(BF16), v_ref[...]) / den).astype(att_ref.dtype)
    n_i = lax.broadcasted_iota(jnp.int32, (c, c), 0)
    m_i = lax.broadcasted_iota(jnp.int32, (c, c), 1)
    rel = (n_i - m_i).astype(F32)
    dec = (jnp.where(rel >= 0, jnp.exp(jnp.maximum(rel, 0.0) * lg_f), 0.0)
           + jnp.where(rel <= 0, jnp.exp(jnp.maximum(-rel, 0.0) * lg_b), 0.0))
    sc = _dot_nt(rq_ref[...], rk_ref[...]) * dec
    y = _dot(sc.astype(BF16), rv_ref[...])
    mu = jnp.mean(y, axis=-1, keepdims=True)
    var = jnp.mean(jnp.square(y - mu), axis=-1, keepdims=True)
    yn = (y - mu) * lax.rsqrt(var + EPS) * gn_ref[...]
    ret_ref[...] = (_silu(g_ref[...].astype(F32)) * yn).astype(ret_ref.dtype)
    pos = lax.broadcasted_iota(jnp.int32, (c, HEAD_DIM), 0).astype(F32)
    kf = rk_ref[...].astype(F32)
    k_f = (kf * jnp.exp((c - 1.0 - pos) * lg_f)).astype(BF16)
    k_b = (kf * jnp.exp(pos * lg_b)).astype(BF16)
    sf_ref[...] = _dot_tn(k_f, rv_ref[...])
    sb_ref[...] = _dot_tn(k_b, rv_ref[...])


def _ctx_mixer(p_ctx, sink, log_g, gn_g):
    c = p_ctx.shape[0]
    hb = lambda off: pl.BlockSpec((c, HEAD_DIM), lambda h: (0, off + h))
    return pl.pallas_call(
        _ctx_mixer_kernel,
        grid=(N_RET_HEADS,),
        in_specs=[
            pl.BlockSpec(memory_space=pltpu.SMEM),
            pl.BlockSpec(memory_space=pltpu.SMEM),
            hb(COL_Q),
            pl.BlockSpec((c, HEAD_DIM), lambda h: (0, COL_K + h // GQA_GROUP)),
            pl.BlockSpec((c, HEAD_DIM), lambda h: (0, COL_V + h // GQA_GROUP)),
            hb(COL_RQ), hb(COL_RK), hb(COL_RV), hb(COL_G),
            pl.BlockSpec((None, 1, HEAD_DIM), lambda h: (h, 0, 0)),
        ],
        out_specs=[
            pl.BlockSpec((c, HEAD_DIM), lambda h: (0, h)),
            pl.BlockSpec((c, HEAD_DIM), lambda h: (0, h)),
            pl.BlockSpec((None, HEAD_DIM, HEAD_DIM), lambda h: (h, 0, 0)),
            pl.BlockSpec((None, HEAD_DIM, HEAD_DIM), lambda h: (h, 0, 0)),
        ],
        out_shape=[
            jax.ShapeDtypeStruct((c, ATT_W), BF16),
            jax.ShapeDtypeStruct((c, RET_W), BF16),
            jax.ShapeDtypeStruct((N_RET_HEADS, HEAD_DIM, HEAD_DIM), F32),
            jax.ShapeDtypeStruct((N_RET_HEADS, HEAD_DIM, HEAD_DIM), F32),
        ],
        compiler_params=_cparams("arbitrary"),
        name="ctx_mixer",
    )(sink, log_g, p_ctx, p_ctx, p_ctx, p_ctx, p_ctx, p_ctx, p_ctx,
      gn_g.reshape(N_RET_HEADS, 1, HEAD_DIM))


RET_KERNEL_CHUNK = 256
RET_UNROLL = 8
RET_UNROLL_BWD = 8


def _ret_kernel(lg_ref, q_ref, k_ref, v_ref, g_ref, sf_ref, sb_ref, gn_ref, o_ref,
                y_scr, dmat, qdf, kdf, qdb, kdb, cdf, cdb):
    h = pl.program_id(0)
    ck = RET_KERNEL_CHUNK
    nit = q_ref.shape[0] // (ck * RET_UNROLL)
    lg_f = lg_ref[0, h]
    lg_b = lg_ref[1, h]
    ii = lax.broadcasted_iota(jnp.int32, (ck, ck), 0)
    jj = lax.broadcasted_iota(jnp.int32, (ck, ck), 1)
    rel = (ii - jj).astype(F32)
    dmat[...] = (jnp.where(rel >= 0, jnp.exp(jnp.maximum(rel, 0.0) * lg_f), 0.0)
                 + jnp.where(rel <= 0, jnp.exp(jnp.maximum(-rel, 0.0) * lg_b), 0.0))
    pos = lax.broadcasted_iota(jnp.int32, (ck, HEAD_DIM), 0).astype(F32)
    qdf[...] = jnp.exp((pos + 1.0) * lg_f)
    kdf[...] = jnp.exp((ck - 1.0 - pos) * lg_f)
    qdb[...] = jnp.exp((ck - pos) * lg_b)
    kdb[...] = jnp.exp(pos * lg_b)
    full = jnp.full((HEAD_DIM, HEAD_DIM), float(ck), F32)
    cdf[...] = jnp.exp(full * lg_f)
    cdb[...] = jnp.exp(full * lg_b)

    def rows_of(z):
        return pl.ds(pl.multiple_of(z * ck, ck), ck)

    def fwd(it, state):
        zs = [it * RET_UNROLL + u for u in range(RET_UNROLL)]
        qs = [q_ref[rows_of(z), :] for z in zs]
        ks = [k_ref[rows_of(z), :] for z in zs]
        vs = [v_ref[rows_of(z), :] for z in zs]
        inner = [_dot((_dot_nt(q, k) * dmat[...]).astype(BF16), v) for q, k, v in zip(qs, ks, vs)]
        kvs = [_dot_tn((k.astype(F32) * kdf[...]).astype(BF16), v) for k, v in zip(ks, vs)]
        for u, z in enumerate(zs):
            y_scr[rows_of(z), :] = inner[u] + _dot(qs[u], state.astype(BF16)) * qdf[...]
            state = cdf[...] * state + kvs[u]
        return state

    lax.fori_loop(0, nit, fwd, sf_ref[...])

    nit_b = q_ref.shape[0] // (ck * RET_UNROLL_BWD)

    def bwd(it, state):
        zs = [nit_b * RET_UNROLL_BWD - 1 - (it * RET_UNROLL_BWD + u) for u in range(RET_UNROLL_BWD)]
        qs = [q_ref[rows_of(z), :] for z in zs]
        kvs = [_dot_tn((k_ref[rows_of(z), :].astype(F32) * kdb[...]).astype(BF16), v_ref[rows_of(z), :])
               for z in zs]
        for u, z in enumerate(zs):
            y = y_scr[rows_of(z), :] + _dot(qs[u], state.astype(BF16)) * qdb[...]
            state = cdb[...] * state + kvs[u]
            mu = jnp.mean(y, axis=-1, keepdims=True)
            var = jnp.mean(jnp.square(y - mu), axis=-1, keepdims=True)
            yn = (y - mu) * lax.rsqrt(var + EPS) * gn_ref[...]
            o_ref[rows_of(z), :] = (_silu(g_ref[rows_of(z), :].astype(F32)) * yn).astype(o_ref.dtype)
        return state

    lax.fori_loop(0, nit_b, bwd, sb_ref[...])


def _retention(p_lat, log_g, s_f, s_b, gn_g):
    s = p_lat.shape[0]
    ck = RET_KERNEL_CHUNK
    assert s % (ck * RET_UNROLL) == 0 and s % (ck * RET_UNROLL_BWD) == 0
    col = lambda off: pl.BlockSpec((s, HEAD_DIM), lambda h: (0, off + h))
    state_spec = pl.BlockSpec((None, HEAD_DIM, HEAD_DIM), lambda h: (h, 0, 0))
    vec = pltpu.VMEM((ck, HEAD_DIM), F32)
    sq = pltpu.VMEM((HEAD_DIM, HEAD_DIM), F32)
    return pl.pallas_call(
        _ret_kernel,
        grid=(N_RET_HEADS,),
        in_specs=[pl.BlockSpec(memory_space=pltpu.SMEM),
                  col(COL_RQ), col(COL_RK), col(COL_RV), col(COL_G), state_spec, state_spec,
                  pl.BlockSpec((None, 1, HEAD_DIM), lambda h: (h, 0, 0))],
        out_specs=col(0),
        out_shape=jax.ShapeDtypeStruct((s, RET_W), BF16),
        scratch_shapes=[pltpu.VMEM((s, HEAD_DIM), F32), pltpu.VMEM((ck, ck), F32),
                        vec, vec, vec, vec, sq, sq],
        compiler_params=_cparams("arbitrary"),
        name="retention",
    )(log_g, p_lat, p_lat, p_lat, p_lat, s_f, s_b, gn_g.reshape(N_RET_HEADS, 1, HEAD_DIM))


def _outproj_kernel(x_ref, a_ref, r_ref, wa_ref, wr_ref, gate_ref, *rest, row, n_cast):
    cast_in, o_ref, cast_out = rest[:n_cast], rest[n_cast], rest[n_cast + 1:]
    y = _dot(a_ref[...], wa_ref[...]) + _dot(r_ref[...], wr_ref[...])
    o_ref[...] = x_ref[...] + gate_ref[row:row + 1, :] * y
    for src, dst in zip(cast_in, cast_out):
        dst[...] = src[...].astype(dst.dtype)


def _out_projection(x, att, ret, w_out, mods, layer, row, tm, cast=()):
    m, d = x.shape
    n = m // tm
    slab = lambda w: pl.BlockSpec((w.shape[0] // n, w.shape[1]), lambda i: (i, 0))
    outs = pl.pallas_call(
        functools.partial(_outproj_kernel, row=row, n_cast=len(cast)),
        grid=(n,),
        in_specs=[
            pl.BlockSpec((tm, d), lambda i: (i, 0)),
            pl.BlockSpec((tm, ATT_W), lambda i: (i, 0)),
            pl.BlockSpec((tm, RET_W), lambda i: (i, 0)),
            pl.BlockSpec((None, ATT_W, d), lambda i: (layer, 0, 0)),
            pl.BlockSpec((None, RET_W, d), lambda i: (layer, ATT_W // RET_W, 0)),
            pl.BlockSpec((None, 8, d), lambda i: (layer, 0, 2)),
        ] + [slab(w) for w in cast],
        out_specs=[pl.BlockSpec((tm, d), lambda i: (i, 0))] + [slab(w) for w in cast],
        out_shape=[jax.ShapeDtypeStruct((m, d), F32)] + [jax.ShapeDtypeStruct(w.shape, BF16) for w in cast],
        compiler_params=_cparams("arbitrary"),
        name="out_projection",
    )(x, att, ret, w_out, w_out, mods, *cast)
    return outs[0] if not cast else outs


def _swiglu_accumulate(h_scr, w1_ref, w3_ref, w2_ref, o_ref, groups):
    w1 = w1_ref[...].astype(BF16)
    w3 = w3_ref[...].astype(BF16)
    w2 = w2_ref[...].astype(BF16)
    us = []
    for lo, n in groups:
        h = h_scr[lo:lo + n, :]
        us.append((_silu(_dot(h, w1)) * _dot(h, w3)).astype(BF16))
    for (lo, n), u in zip(groups, us):
        o_ref[lo:lo + n, :] += _dot(u, w2)


def _ffn_kernel(x_ref, sh_ref, sc_ref, gate_ref, g2_ref, w1_ref, w3_ref, w2_ref, o_ref, h_scr,
                *, row, n_parts):
    j = pl.program_id(1)
    part = x_ref.shape[0] // n_parts

    @pl.when(j == 0)
    def _():
        h = _norm_modulate(x_ref[...], g2_ref[...], sh_ref[row:row + 1, :], sc_ref[row:row + 1, :])
        h_scr[...] = h.astype(BF16)
        o_ref[...] = jnp.zeros_like(o_ref)

    _swiglu_accumulate(h_scr, w1_ref, w3_ref, w2_ref, o_ref, tuple((p * part, part) for p in range(n_parts)))

    @pl.when(j == pl.num_programs(1) - 1)
    def _():
        o_ref[...] = x_ref[...] + gate_ref[row:row + 1, :] * o_ref[...]


def _dense_ffn(x, mods, layer, row, g2, w1, w3, w2, tm, tf):
    m, d = x.shape
    f = w1.shape[1]
    mod = lambda k: pl.BlockSpec((None, 8, d), lambda i, j: (layer, 0, k))
    return pl.pallas_call(
        functools.partial(_ffn_kernel, row=row, n_parts=2 if tm >= 1024 else 1),
        grid=(m // tm, f // tf),
        in_specs=[
            pl.BlockSpec((tm, d), lambda i, j: (i, 0)),
            mod(3), mod(4), mod(5),
            pl.BlockSpec((1, d), lambda i, j: (0, 0)),
            pl.BlockSpec((d, tf), lambda i, j: (0, j)),
            pl.BlockSpec((d, tf), lambda i, j: (0, j)),
            pl.BlockSpec((tf, d), lambda i, j: (j, 0)),
        ],
        out_specs=pl.BlockSpec((tm, d), lambda i, j: (i, 0)),
        out_shape=jax.ShapeDtypeStruct((m, d), F32),
        scratch_shapes=[pltpu.VMEM((tm, d), BF16)],
        compiler_params=_cparams("arbitrary", "arbitrary"),
        name="dense_ffn",
    )(x, mods, mods, mods, g2, w1, w3, w2)


def _route_top2(logits, carry, live):
    tm = logits.shape[0]
    lane = lax.broadcasted_iota(jnp.int32, (tm, LANES), 1).astype(F32)
    v1 = jnp.max(logits, axis=-1, keepdims=True)
    e1 = jnp.min(jnp.where(logits == v1, lane, float(LANES)), axis=-1, keepdims=True)
    rest = jnp.where(lane == e1, NEG_INF, logits)
    v2 = jnp.max(rest, axis=-1, keepdims=True)
    e2 = jnp.min(jnp.where(rest == v2, lane, float(LANES)), axis=-1, keepdims=True)
    t = jnp.exp(v2 - v1)
    w1 = 1.0 / (1.0 + t)
    w2 = t / (1.0 + t)
    oh1 = jnp.where(lane == e1, 1.0, 0.0)
    oh2 = jnp.where(lane == e2, 1.0, 0.0)
    oh = oh1 + oh2
    r_i = lax.broadcasted_iota(jnp.int32, (tm, tm), 0)
    c_i = lax.broadcasted_iota(jnp.int32, (tm, tm), 1)
    tri = jnp.where(c_i < r_i, 1.0, 0.0).astype(BF16)
    before = _dot(tri, oh.astype(BF16)) + carry[0:1, :]
    rank1 = jnp.sum(before * oh1, axis=-1, keepdims=True)
    rank2 = jnp.sum(before * oh2, axis=-1, keepdims=True)
    carry[...] = carry[...] + live * jnp.sum(oh, axis=0, keepdims=True)
    info = jnp.where(lane == 0.0, e1, 0.0)
    info = jnp.where(lane == 1.0, e2, info)
    info = jnp.where(lane == 2.0, rank1, info)
    info = jnp.where(lane == 3.0, rank2, info)
    info = jnp.where(lane == 4.0, w1, info)
    info = jnp.where(lane == 5.0, w2, info)
    return info


def _outproj_router_kernel(x_ref, a_ref, r_ref, wa_ref, wr_ref, gate_ref, sh_ref, sc_ref, g2_ref, rw_ref, rb_ref,
                           o_ref, h_ref, info_ref, cnt_ref, xprev, carry, *, row):
    t = pl.program_id(0)
    tm, d = x_ref.shape

    @pl.when(t == 0)
    def _():
        xprev[...] = jnp.zeros_like(xprev)
        carry[...] = jnp.zeros_like(carry)

    y = _dot(a_ref[...], wa_ref[...]) + _dot(r_ref[...], wr_ref[...])
    x_new = x_ref[...] + gate_ref[row:row + 1, :] * y
    o_ref[...] = x_new

    h_ref[...] = _norm_modulate(xprev[...], g2_ref[...], sh_ref[row:row + 1, :], sc_ref[row:row + 1, :])
    lane = lax.broadcasted_iota(jnp.int32, (tm, LANES), 1).astype(F32)
    parts = [None] * N_EXPERTS
    for b in range(d // LANES):
        cols = slice(b * LANES, (b + 1) * LANES)
        hb = h_ref[:, cols]
        for e in range(N_EXPERTS):
            term = hb * rw_ref[e:e + 1, cols]
            parts[e] = term if parts[e] is None else parts[e] + term
    logits = jnp.full((tm, LANES), NEG_INF, F32)
    for e in range(N_EXPERTS):
        logit_e = jnp.sum(parts[e], axis=-1, keepdims=True) + rb_ref[0:1, e:e + 1]
        logits = jnp.where(lane == float(e), logit_e, logits)
    info_ref[...] = _route_top2(logits, carry, jnp.where(t > 0, 1.0, 0.0))
    cnt_ref[...] = carry[...]
    xprev[...] = x_new


def _out_projection_router(x, att, ret, w_out, mods, layer, row, g2, router_w, router_b, tm):
    m, d = x.shape
    n = m // tm
    rw_t = jnp.pad(router_w.astype(F32).T, ((0, 8 - N_EXPERTS), (0, 0)))
    rb_pad = jnp.pad(router_b.astype(F32), (0, LANES - N_EXPERTS)).reshape(1, LANES)
    cur = lambda t: (jnp.minimum(t, n - 1), 0)
    prv = lambda t: (jnp.maximum(t - 1, 0), 0)
    mod = lambda k: pl.BlockSpec((None, 8, d), lambda t: (layer, 0, k))
    once = pl.Buffered(1)
    return pl.pallas_call(
        functools.partial(_outproj_router_kernel, row=row),
        grid=(n + 1,),
        in_specs=[
            pl.BlockSpec((tm, d), cur),
            pl.BlockSpec((tm, ATT_W), cur),
            pl.BlockSpec((tm, RET_W), cur),
            pl.BlockSpec((None, ATT_W, d), lambda t: (layer, 0, 0), pipeline_mode=once),
            pl.BlockSpec((None, RET_W, d), lambda t: (layer, ATT_W // RET_W, 0), pipeline_mode=once),
            mod(2), mod(3), mod(4),
            pl.BlockSpec((1, d), lambda t: (0, 0)),
            pl.BlockSpec((8, d), lambda t: (0, 0)),
            pl.BlockSpec((1, LANES), lambda t: (0, 0)),
        ],
        out_specs=[
            pl.BlockSpec((tm, d), cur),
            pl.BlockSpec((tm, d), prv),
            pl.BlockSpec((tm, LANES), prv),
            pl.BlockSpec((8, LANES), lambda t: (0, 0)),
        ],
        out_shape=[
            jax.ShapeDtypeStruct((m, d), F32),
            jax.ShapeDtypeStruct((m, d), F32),
            jax.ShapeDtypeStruct((m, LANES), F32),
            jax.ShapeDtypeStruct((8, LANES), F32),
        ],
        scratch_shapes=[pltpu.VMEM((tm, d), F32), pltpu.VMEM((8, LANES), F32)],
        compiler_params=_cparams("arbitrary"),
        name="out_projection_router",
    )(x, att, ret, w_out, w_out, mods, mods, mods, g2, rw_t, rb_pad)


GATHER_UNROLL = 8


def _gather_rows(idx_ref, src_hbm, dst, sem, n):
    def body(kb, carry):
        for u in range(GATHER_UNROLL):
            k = kb * GATHER_UNROLL + u
            pltpu.make_async_copy(src_hbm.at[pl.ds(idx_ref[0, 0, k], 1)], dst.at[pl.ds(k, 1)], sem).start()
        return carry
    lax.fori_loop(0, n // GATHER_UNROLL, body, 0)


def _wait_rows(src_hbm, dst, sem, n):
    pltpu.make_async_copy(src_hbm.at[pl.ds(0, n)], dst, sem).wait()


def _issue_rows(idx_ref, src_hbm, dst, sem, start, count):
    for u in range(count):
        k = start + u
        pltpu.make_async_copy(src_hbm.at[pl.ds(idx_ref[0, 0, k], 1)], dst.at[pl.ds(k, 1)], sem).start()


def _moe_ffn_kernel(te_ref, nu_ref, nv_ref, cur_ref, nxt_ref, h_hbm, w1_ref, w3_ref, w2_ref, o_ref,
                    buf, sem, h_scr, *, tm, n_f):
    r = pl.program_id(0)
    j = pl.program_id(1)
    n_used = nu_ref[0]
    half = tm // 2
    quarter = tm // 4
    per_step = tm // n_f
    head = tm - per_step * n_f

    @pl.when(r < n_used)
    def _():
        @pl.when(j == 0)
        def _():
            @pl.when(r == 0)
            def _():
                _gather_rows(cur_ref, h_hbm, buf, sem, tm)

            _wait_rows(h_hbm, buf, sem, tm)
            h_scr[...] = buf[...].astype(BF16)

            @pl.when(r + 1 < n_used)
            def _():
                _issue_rows(nxt_ref, h_hbm, buf, sem, 0, head)

            o_ref[...] = jnp.zeros_like(o_ref)

        more = r + 1 < n_used
        quarters = (nv_ref[r] + quarter - 1) // quarter
        row_groups = {1: ((0, quarter),), 2: ((0, half),), 3: ((0, half), (half, quarter)),
                      4: ((0, half), (half, half))}

        def case(fetch, groups):
            if fetch:
                _issue_rows(nxt_ref, h_hbm, buf, sem, head + j * per_step, per_step)
            _swiglu_accumulate(h_scr, w1_ref, w3_ref, w2_ref, o_ref, groups)

        for nq, groups in row_groups.items():
            pl.when(more & (quarters == nq))(functools.partial(case, True, groups))
            pl.when(jnp.logical_not(more) & (quarters == nq))(functools.partial(case, False, groups))

    @pl.when((r >= n_used) & (j == 0))
    def _():
        o_ref[...] = jnp.zeros_like(o_ref)


def _moe_ffn(h2, tile_expert, n_used, n_valid, src_rows, w1, w3, w2, tm, tf):
    d = h2.shape[1]
    f = w1.shape[2]
    n_tiles = src_rows.shape[0]
    n_f = f // tf
    clamp = lambda r, nu: jnp.maximum(jnp.minimum(r, nu[0] - 1), 0)
    fcol = lambda r, j, nu: jnp.where(r < nu[0], j, n_f - 1)
    grid_spec = pltpu.PrefetchScalarGridSpec(
        num_scalar_prefetch=3,
        grid=(n_tiles, n_f),
        in_specs=[
            pl.BlockSpec((1, 1, tm), lambda r, j, te, nu, nv: (clamp(r, nu), 0, 0), memory_space=pltpu.SMEM),
            pl.BlockSpec((1, 1, tm), lambda r, j, te, nu, nv: (clamp(r + 1, nu), 0, 0), memory_space=pltpu.SMEM),
            pl.BlockSpec(memory_space=pl.ANY),
            pl.BlockSpec((None, d, tf), lambda r, j, te, nu, nv: (te[clamp(r, nu)], 0, fcol(r, j, nu))),
            pl.BlockSpec((None, d, tf), lambda r, j, te, nu, nv: (te[clamp(r, nu)], 0, fcol(r, j, nu))),
            pl.BlockSpec((None, tf, d), lambda r, j, te, nu, nv: (te[clamp(r, nu)], fcol(r, j, nu), 0)),
        ],
        out_specs=pl.BlockSpec((tm, d), lambda r, j, te, nu, nv: (r, 0)),
        scratch_shapes=[
            pltpu.VMEM((tm, d), F32),
            pltpu.SemaphoreType.DMA(()),
            pltpu.VMEM((tm, d), BF16),
        ],
    )
    return pl.pallas_call(
        functools.partial(_moe_ffn_kernel, tm=tm, n_f=n_f),
        grid_spec=grid_spec,
        out_shape=jax.ShapeDtypeStruct((n_tiles * tm, d), F32),
        compiler_params=_cparams("arbitrary", "arbitrary"),
        name="moe_ffn",
    )(tile_expert, n_used, n_valid, src_rows, src_rows, h2, w1, w3, w2)


def _combine_kernel(cur_ref, nxt_ref, x_ref, info_ref, gate_ref, y_hbm, o_ref, buf_a, buf_b, sems, *, tm, row):
    i = pl.program_id(0)

    @pl.when(i == 0)
    def _():
        _gather_rows(cur_ref, y_hbm, buf_a, sems.at[0], 2 * tm)

    def step(cur, cur_sem, nxt, nxt_sem):
        _wait_rows(y_hbm, cur, cur_sem, 2 * tm)

        @pl.when(i + 1 < pl.num_programs(0))
        def _():
            _issue_rows(nxt_ref, y_hbm, nxt, nxt_sem, 0, 2 * tm)

        info = info_ref[...]
        y = info[:, 4:5] * cur[0:tm, :] + info[:, 5:6] * cur[tm:2 * tm, :]
        o_ref[...] = x_ref[...] + gate_ref[row:row + 1, :] * y

    pl.when(i % 2 == 0)(functools.partial(step, buf_a, sems.at[0], buf_b, sems.at[1]))
    pl.when(i % 2 == 1)(functools.partial(step, buf_b, sems.at[1], buf_a, sems.at[0]))


def _moe_combine(x, info, mods, layer, row, y_rows, pos_tiles, tm):
    m, d = x.shape
    nt = m // tm
    return pl.pallas_call(
        functools.partial(_combine_kernel, tm=tm, row=row),
        grid=(nt,),
        in_specs=[
            pl.BlockSpec((1, 1, 2 * tm), lambda i: (i, 0, 0), memory_space=pltpu.SMEM),
            pl.BlockSpec((1, 1, 2 * tm), lambda i: (jnp.minimum(i + 1, nt - 1), 0, 0), memory_space=pltpu.SMEM),
            pl.BlockSpec((tm, d), lambda i: (i, 0)),
            pl.BlockSpec((tm, LANES), lambda i: (i, 0)),
            pl.BlockSpec((None, 8, d), lambda i: (layer, 0, 5)),
            pl.BlockSpec(memory_space=pl.ANY),
        ],
        out_specs=pl.BlockSpec((tm, d), lambda i: (i, 0)),
        out_shape=jax.ShapeDtypeStruct((m, d), F32),
        scratch_shapes=[pltpu.VMEM((2 * tm, d), F32), pltpu.VMEM((2 * tm, d), F32),
                        pltpu.SemaphoreType.DMA((2,))],
        compiler_params=_cparams("arbitrary"),
        name="moe_combine",
    )(pos_tiles, pos_tiles, x, info, mods, y_rows)


def _mix_and_moe(x_in, att, ret, w_out, mods, layer, row, g2, router_w, router_b, w1, w3, w2,
                 tm_route, tm_exp, tf):
    m, d = x_in.shape
    x, h2, info, cnt = _out_projection_router(x_in, att, ret, w_out, mods, layer, row, g2,
                                              router_w, router_b, tm_route)
    counts = cnt[0, :N_EXPERTS].astype(jnp.int32)
    tiles_per = (counts + tm_exp - 1) // tm_exp
    tile_end = jnp.cumsum(tiles_per)
    group_start = (tile_end - tiles_per) * tm_exp
    n_tiles = (2 * m) // tm_exp + N_EXPERTS
    n_used = tile_end[-1:].astype(jnp.int32)
    tile_ids = jnp.arange(n_tiles, dtype=jnp.int32)
    tile_expert = jnp.minimum(jnp.sum(tile_end[None, :] <= tile_ids[:, None], axis=1),
                              N_EXPERTS - 1).astype(jnp.int32)
    rows_before = (tile_ids - (tile_end - tiles_per)[tile_expert]) * tm_exp
    n_valid = jnp.clip(counts[tile_expert] - rows_before, 0, tm_exp).astype(jnp.int32)
    e12 = info[:, 0:2].astype(jnp.int32)
    rank12 = info[:, 2:4].astype(jnp.int32)
    pos = group_start[e12] + rank12
    tok = jnp.broadcast_to(jnp.arange(m, dtype=jnp.int32)[:, None], (m, 2))
    src_rows = jnp.zeros((n_tiles * tm_exp,), jnp.int32).at[pos.reshape(-1)].set(
        tok.reshape(-1), unique_indices=True, mode="promise_in_bounds")
    y_rows = _moe_ffn(h2, tile_expert, n_used, n_valid, src_rows.reshape(n_tiles, 1, tm_exp),
                      w1, w3, w2, tm_exp, tf)
    nt = m // tm_route
    pos_tiles = pos.reshape(nt, tm_route, 2).transpose(0, 2, 1).reshape(nt, 1, 2 * tm_route)
    return _moe_combine(x, info, mods, layer, row, y_rows, pos_tiles, tm_route)


def _rope_tables(n_tokens):
    n_rows = n_tokens // GRID_W
    inv = ROPE_THETA ** (-jnp.arange(ROPE_PAIRS, dtype=F32) / ROPE_PAIRS)
    ang_r = jnp.arange(n_rows).astype(F32)[:, None] * inv
    ang_c = jnp.arange(GRID_W).astype(F32)[:, None] * inv
    cr, sr, cc, sc = jnp.cos(ang_r), jnp.sin(ang_r), jnp.cos(ang_c), jnp.sin(ang_c)
    zr, zc = jnp.zeros_like(cr), jnp.zeros_like(cc)
    rowtab = jnp.stack([jnp.concatenate(p, axis=-1) for p in
                        ((cr, cr, zr, zr), (-sr, zr, zr, zr), (zr, sr, zr, zr))])
    coltab = jnp.stack([jnp.concatenate(p, axis=-1) for p in
                        ((zc, zc, cc, cc), (zc, zc, -sc, zc), (zc, zc, zc, sc))])
    return rowtab, coltab


def _identity_rope_tables(n_tokens):
    n_rows = n_tokens // GRID_W
    rowtab = jnp.stack([jnp.ones((n_rows, HEAD_DIM), F32), jnp.zeros((n_rows, HEAD_DIM), F32),
                        jnp.zeros((n_rows, HEAD_DIM), F32)])
    return rowtab, jnp.zeros((3, GRID_W, HEAD_DIM), F32)


def kernel(x, c, ctx, c_ctx, ada_w, ada_b, norm1_g, norm2_g, w_in, q_norm_g, k_norm_g, attn_sink,
           ret_decay, ret_gn_g, w_out, ffn_w1, ffn_w3, ffn_w2, router_w, router_b, moe_w1, moe_w3, moe_w2):
    b, s, d = x.shape
    n_ctx = ctx.shape[1]
    depth = ada_w.shape[0]
    assert b == 1, "one latent sequence per call"
    x_lat = x[0]
    x_ctx = ctx[0]

    tm_lat = min(1024, s)
    tm_ffn = min(512, s)
    tq = min(512, s)
    tf_ffn = 512
    tm_moe = 1024
    tf_moe = 256

    cond = jnp.stack([c[0], c_ctx]).astype(F32)
    mods = _ada_mods(cond, ada_w, ada_b)
    rope_l = _rope_tables(s)
    rope_c = _identity_rope_tables(n_ctx)
    log_g_all = jax.nn.log_sigmoid(ret_decay.astype(F32))

    w_in_i = w_in.astype(BF16)
    w_out_i = w_out.astype(BF16)

    for i in range(depth):
        last = i == depth - 1
        g1 = norm1_g[i].reshape(1, d)
        g2 = norm2_g[i].reshape(1, d)
        log_g = log_g_all[i]
        gain = jnp.concatenate([
            jnp.tile(q_norm_g[i].astype(F32) * ATT_SCALE, N_ATT_HEADS),
            jnp.tile(k_norm_g[i].astype(F32), N_KV_HEADS),
            jnp.ones((KV_W + RET_W,), F32),
            jnp.full((RET_W,), ATT_SCALE, F32),
            jnp.ones((2 * RET_W,), F32)]).reshape(1, IN_COLS)

        p_lat = _in_projection(x_lat, mods, i, 0, g1, w_in_i, gain, *rope_l, tm_lat)
        p_ctx = _in_projection(x_ctx, mods, i, 1, g1, w_in_i, gain, *rope_c, n_ctx)

        att_c, ret_c, s_f, s_b = _ctx_mixer(p_ctx, attn_sink[i], log_g, ret_gn_g[i])
        att_l = _window_attention(p_lat, p_ctx, attn_sink[i], tq)
        ret_l = _retention(p_lat, log_g, s_f, s_b, ret_gn_g[i])
        j = i // 2
        if i % 2 == 0:
            ffn_w = (ffn_w1[j], ffn_w3[j], ffn_w2[j])
            n_slabs = s // tm_ffn
            if all(w.shape[0] % (16 * n_slabs) == 0 for w in ffn_w):
                x_lat, w1, w3, w2 = _out_projection(x_lat, att_l, ret_l, w_out_i, mods, i, 0, tm_ffn, cast=ffn_w)
            else:
                x_lat = _out_projection(x_lat, att_l, ret_l, w_out_i, mods, i, 0, tm_ffn)
                w1, w3, w2 = (w.astype(BF16) for w in ffn_w)
            x_lat = _dense_ffn(x_lat, mods, i, 0, g2, w1, w3, w2, tm_ffn, tf_ffn)
        else:
            x_lat = _mix_and_moe(x_lat, att_l, ret_l, w_out_i, mods, i, 0, g2, router_w[j], router_b[j],
                                 moe_w1[j], moe_w3[j], moe_w2[j], tm_ffn, min(tm_moe, s), tf_moe)

        if not last:
            if i % 2 == 0:
                x_ctx = _out_projection(x_ctx, att_c, ret_c, w_out_i, mods, i, 1, n_ctx)
                x_ctx = _dense_ffn(x_ctx, mods, i, 1, g2, w1, w3, w2, n_ctx, tf_ffn)
            else:
                x_ctx = _mix_and_moe(x_ctx, att_c, ret_c, w_out_i, mods, i, 1, g2, router_w[j], router_b[j],
                                     moe_w1[j], moe_w3[j], moe_w2[j], n_ctx, n_ctx, tf_moe)
    return x_lat[None]
```
